```python
import jax, jax.numpy as jnp
from jax import lax
import numpy as np

D_MODEL = 1024
BATCH = 8
SEQ = 2048
DEPTH = 1

HEAD_DIM = 64
N_Q_HEADS = 8
N_KV_HEADS = 2
Q_PER_KV = N_Q_HEADS // N_KV_HEADS
ATTN_WIDTH = N_Q_HEADS * HEAD_DIM
KV_WIDTH = N_KV_HEADS * HEAD_DIM
ROPE_PAIRS = HEAD_DIM // 4
ROPE_THETA = 10000.0
Q_BLOCK = 128
POOL_WINDOWS = (2, 4, 8, 16)
N_POOL_GROUPS = len(POOL_WINDOWS)
POOL_WIDTH = D_MODEL - ATTN_WIDTH
POOL_GROUP_DIM = POOL_WIDTH // N_POOL_GROUPS
MIX_WIDTH = ATTN_WIDTH + POOL_WIDTH
IN_WIDTH = ATTN_WIDTH + 2 * KV_WIDTH + POOL_WIDTH
D_FF = ((8 * D_MODEL // 3 + 255) // 256) * 256
GRID_W = 64
EPS = 1e-6

kernel_name = "hybrid_gqa_axialrope_multiscale_pool_swiglu"


def rmsnorm(x, g):
    xf = x.astype(jnp.float32)
    y = xf * lax.rsqrt(jnp.mean(xf * xf, axis=-1, keepdims=True) + EPS)
    return (y * g.astype(jnp.float32)).astype(x.dtype)


def rope_1d(x, ang):
    xf = x.astype(jnp.float32)
    x1, x2 = jnp.split(xf, 2, axis=-1)
    c = jnp.cos(ang)[None, :, None, :]
    s = jnp.sin(ang)[None, :, None, :]
    return jnp.concatenate([x1 * c - x2 * s, x2 * c + x1 * s], axis=-1).astype(x.dtype)


def axial_rope(x, ang_row, ang_col):
    xr, xc = jnp.split(x, 2, axis=-1)
    return jnp.concatenate([rope_1d(xr, ang_row), rope_1d(xc, ang_col)], axis=-1)


def attention_mixer(q, k, v, q_g, k_g, ang_row, ang_col):
    B, S, _ = q.shape
    q = q.reshape(B, S, N_Q_HEADS, HEAD_DIM)
    k = k.reshape(B, S, N_KV_HEADS, HEAD_DIM)
    v = v.reshape(B, S, N_KV_HEADS, HEAD_DIM)
    q = axial_rope(rmsnorm(q, q_g), ang_row, ang_col)
    k = axial_rope(rmsnorm(k, k_g), ang_row, ang_col)
    n_blk = S // Q_BLOCK
    qb = q.reshape(B, n_blk, Q_BLOCK, N_KV_HEADS, Q_PER_KV, HEAD_DIM).transpose(1, 0, 2, 3, 4, 5)
    scale = HEAD_DIM ** -0.5

    def block(qblk):
        s = jnp.einsum('bqkgd,bskd->bkgqs', qblk, k).astype(jnp.float32) * scale
        p = jax.nn.softmax(s, axis=-1)
        return jnp.einsum('bkgqs,bskd->bqkgd', p.astype(v.dtype), v)

    o = lax.map(block, qb)
    return o.transpose(1, 0, 2, 3, 4, 5).reshape(B, S, ATTN_WIDTH)


def pool_mixer(u, pool_w, pool_b, pool_scale):
    B, S, _ = u.shape
    ug = u.reshape(B, S, N_POOL_GROUPS, POOL_GROUP_DIM)
    uf = ug.astype(jnp.float32)
    csum = jnp.concatenate([jnp.zeros_like(uf[:, :1]), jnp.cumsum(uf, axis=1)], axis=1)
    t = jnp.arange(S)
    pooled = []
    for g, w in enumerate(POOL_WINDOWS):
        lo = jnp.clip(t - w // 2, 0, S)
        hi = jnp.clip(t - w // 2 + w, 0, S)
        cnt = (hi - lo).astype(jnp.float32)[None, :, None]
        cg = csum[:, :, g, :]
        mean = (cg[:, hi] - cg[:, lo]) / cnt
        pooled.append(mean - uf[:, :, g, :])
    pooled = jnp.stack(pooled, axis=1).astype(u.dtype)
    y = jnp.einsum('bgsc,gcd->bsgd', pooled, pool_w) + pool_b[None, None]
    return y.reshape(B, S, POOL_WIDTH) * pool_scale


def setup_inputs(seed: int = 0) -> dict:
    key = jax.random.key(seed)
    ks = jax.random.split(key, 14)
    f32 = jnp.float32

    def nrm(k, shape, fan_in):
        return jax.random.normal(k, shape, f32) * fan_in ** -0.5

    def gain(k, shape):
        return 1.0 + 0.05 * jax.random.normal(k, shape, f32)

    res_scale = (2 * DEPTH) ** -0.5
    return {
        "x": jax.random.normal(ks[0], (BATCH, SEQ, D_MODEL), f32),
        "norm1_g": gain(ks[1], (DEPTH, D_MODEL)),
        "w_in": nrm(ks[2], (DEPTH, D_MODEL, IN_WIDTH), D_MODEL),
        "q_norm_g": gain(ks[3], (DEPTH, HEAD_DIM)),
        "k_norm_g": gain(ks[4], (DEPTH, HEAD_DIM)),
        "pool_w": nrm(ks[5], (DEPTH, N_POOL_GROUPS, POOL_GROUP_DIM, POOL_GROUP_DIM), POOL_GROUP_DIM),
        "pool_b": 0.02 * jax.random.normal(ks[6], (DEPTH, N_POOL_GROUPS, POOL_GROUP_DIM), f32),
        "pool_scale": gain(ks[7], (DEPTH, POOL_WIDTH)),
        "w_out": nrm(ks[8], (DEPTH, MIX_WIDTH, D_MODEL), MIX_WIDTH) * res_scale,
        "norm2_g": gain(ks[9], (DEPTH, D_MODEL)),
        "w_gate": nrm(ks[10], (DEPTH, D_MODEL, D_FF), D_MODEL),
        "w_up": nrm(ks[11], (DEPTH, D_MODEL, D_FF), D_MODEL),
        "w_down": nrm(ks[12], (DEPTH, D_FF, D_MODEL), D_FF) * res_scale,
    }


def reference(x, norm1_g, w_in, q_norm_g, k_norm_g, pool_w, pool_b, pool_scale,
              w_out, norm2_g, w_gate, w_up, w_down):
    B, S, _ = x.shape
    rows = S // GRID_W
    row = jnp.repeat(jnp.arange(rows), GRID_W).astype(jnp.float32)
    col = (jnp.arange(S) % GRID_W).astype(jnp.float32)
    inv_freq = ROPE_THETA ** (-jnp.arange(ROPE_PAIRS, dtype=jnp.float32) / ROPE_PAIRS)
    ang_row = row[:, None] * inv_freq[None, :]
    ang_col = col[:, None] * inv_freq[None, :]

    splits = [ATTN_WIDTH, ATTN_WIDTH + KV_WIDTH, ATTN_WIDTH + 2 * KV_WIDTH]
    for l in range(DEPTH):
        h = rmsnorm(x, norm1_g[l])
        proj = h @ w_in[l]
        q, k, v, u = jnp.split(proj, splits, axis=-1)
        a = attention_mixer(q, k, v, q_norm_g[l], k_norm_g[l], ang_row, ang_col)
        p = pool_mixer(u, pool_w[l], pool_b[l], pool_scale[l])
        x = x + jnp.concatenate([a, p], axis=-1) @ w_out[l]
        h = rmsnorm(x, norm2_g[l])
        x = x + (jax.nn.silu(h @ w_gate[l]) * (h @ w_up[l])) @ w_down[l]
    return x
```

```python
import functools
import math

import jax
import jax.numpy as jnp
import numpy as np
from jax import lax
from jax.experimental import pallas as pl
from jax.experimental.pallas import tpu as pltpu

D_MODEL = 1024
HEAD_DIM = 64
N_Q_HEADS = 8
N_KV_HEADS = 2
Q_PER_KV = N_Q_HEADS // N_KV_HEADS
ATTN_WIDTH = N_Q_HEADS * HEAD_DIM
KV_WIDTH = N_KV_HEADS * HEAD_DIM
GROUP_WIDTH = Q_PER_KV * HEAD_DIM
ROPE_PAIRS = HEAD_DIM // 4
ROPE_THETA = 10000.0
POOL_WINDOWS = (2, 4, 8, 16)
N_POOL_GROUPS = len(POOL_WINDOWS)
POOL_WIDTH = D_MODEL - ATTN_WIDTH
POOL_GROUP_DIM = POOL_WIDTH // N_POOL_GROUPS
IN_WIDTH = ATTN_WIDTH + 2 * KV_WIDTH + POOL_WIDTH
GRID_W = 64
EPS = 1e-6

LANES = 128
POOL_HALO = 8
VMEM_LIMIT_BYTES = 56 * 1024 * 1024

ROW_TILE = 512
Q_TILE = 128
FF_CHUNK = 512

F32 = jnp.float32
BF16 = jnp.bfloat16


def _lane_iota(shape):
    return lax.broadcasted_iota(jnp.int32, shape, len(shape) - 1)


def _rope_tables(seq):
    t = np.arange(seq)
    row = (t // GRID_W).astype(np.float64)
    col = (t % GRID_W).astype(np.float64)
    inv_freq = ROPE_THETA ** (-np.arange(ROPE_PAIRS, dtype=np.float64) / ROPE_PAIRS)
    ang_row = row[:, None] * inv_freq[None, :]
    ang_col = col[:, None] * inv_freq[None, :]
    ang = np.concatenate([ang_row, ang_row, ang_col, ang_col], axis=1)
    sign = np.concatenate([-np.ones(ROPE_PAIRS), np.ones(ROPE_PAIRS)] * 2)
    cos = np.tile(np.cos(ang), (1, 2)).astype(np.float32)
    sin = np.tile(np.sin(ang) * sign[None, :], (1, 2)).astype(np.float32)
    return cos, sin


def _segment_ones(width):
    idx = np.arange(width) // HEAD_DIM
    return (idx[:, None] == idx[None, :]).astype(np.float32)


def _head_sumsq(x, seg):
    sq = x * x
    hi = sq.astype(BF16)
    lo = (sq - hi.astype(F32)).astype(BF16)
    return (jnp.dot(hi, seg, preferred_element_type=F32)
            + jnp.dot(lo, seg, preferred_element_type=F32))


def _rope_partner(x):
    lane = _lane_iota(x.shape)
    fwd = pltpu.roll(x, LANES - ROPE_PAIRS, axis=1)
    bwd = pltpu.roll(x, ROPE_PAIRS, axis=1)
    return jnp.where((lane & ROPE_PAIRS) == 0, fwd, bwd)


def _norm_rope(x, sumsq, gain, cos, sin):
    y = x * lax.rsqrt(sumsq * (1.0 / HEAD_DIM) + EPS) * gain
    return y * cos + _rope_partner(y) * sin


def _inproj_kernel(x_ref, g1_ref, win_ref, qg_ref, kg_ref, cos_ref, sin_ref, seg_ref,
                   q_ref, krep_ref, vext_ref, u_ref, *, q_scale):
    x = x_ref[...]
    ms = jnp.mean(x * x, axis=-1, keepdims=True)
    h = (x * lax.rsqrt(ms + EPS) * g1_ref[...]).astype(BF16)
    proj = jnp.dot(h, win_ref[...], preferred_element_type=F32)

    cos = cos_ref[...]
    sin = sin_ref[...]
    seg = seg_ref[...]
    qg = qg_ref[...]
    kg = kg_ref[...]

    for half in range(ATTN_WIDTH // 256):
        qh = proj[:, half * 256:(half + 1) * 256]
        ss = _head_sumsq(qh, seg)
        for j in range(2):
            sl = slice(j * LANES, (j + 1) * LANES)
            qr = _norm_rope(qh[:, sl], ss[:, sl], qg, cos, sin) * q_scale
            c0 = half * 256 + j * LANES
            q_ref[:, c0:c0 + LANES] = qr.astype(BF16)

    k = proj[:, ATTN_WIDTH:ATTN_WIDTH + KV_WIDTH]
    kss = _head_sumsq(k, seg[:KV_WIDTH, :KV_WIDTH])
    kr = _norm_rope(k, kss, kg, cos, sin)
    lane = _lane_iota(kr.shape)
    first = lane < HEAD_DIM
    kswap = pltpu.roll(kr, HEAD_DIM, axis=1)
    k00 = jnp.where(first, kr, kswap).astype(BF16)
    k11 = jnp.where(first, kswap, kr).astype(BF16)
    for j in range(GROUP_WIDTH // LANES):
        krep_ref[:, j * LANES:(j + 1) * LANES] = k00
        krep_ref[:, GROUP_WIDTH + j * LANES:GROUP_WIDTH + (j + 1) * LANES] = k11

    v = proj[:, ATTN_WIDTH + KV_WIDTH:ATTN_WIDTH + 2 * KV_WIDTH]
    vswap = pltpu.roll(v, HEAD_DIM, axis=1)
    ones_col = jnp.where(lane == HEAD_DIM, 1.0, 0.0)
    vext_ref[:, :LANES] = jnp.where(first, v, ones_col).astype(BF16)
    vext_ref[:, LANES:] = jnp.where(first, vswap, ones_col).astype(BF16)

    u_ref[...] = proj[:, ATTN_WIDTH + 2 * KV_WIDTH:]


def _attention_kernel(q_ref, krep_ref, vext_ref, o_ref, *, seq):
    n_tiles = seq // Q_TILE
    lane_q = _lane_iota((Q_TILE, GROUP_WIDTH))
    lane_o = _lane_iota((Q_TILE, LANES))

    def body(i, carry):
        r0 = pl.multiple_of(i * Q_TILE, Q_TILE)
        qt = q_ref[pl.ds(r0, Q_TILE), :]
        zero = jnp.zeros_like(qt)
        qm = jnp.concatenate(
            [jnp.where((lane_q // HEAD_DIM) == hh, qt, zero) for hh in range(Q_PER_KV)], axis=0)
        s = lax.dot_general(qm, krep_ref[...], (((1,), (1,)), ((), ())),
                            preferred_element_type=F32)
        m = jnp.max(s, axis=-1, keepdims=True)
        p = jnp.exp2(s - m).astype(BF16)
        oe = jnp.dot(p, vext_ref[...], preferred_element_type=F32)
        on = oe / oe[:, HEAD_DIM:HEAD_DIM + 1]
        heads = [on[hh * Q_TILE:(hh + 1) * Q_TILE] for hh in range(Q_PER_KV)]
        for j in range(Q_PER_KV // 2):
            pair = jnp.where(lane_o < HEAD_DIM, heads[2 * j],
                             pltpu.roll(heads[2 * j + 1], HEAD_DIM, axis=1))
            o_ref[pl.ds(r0, Q_TILE), j * LANES:(j + 1) * LANES] = pair.astype(BF16)
        return carry

    lax.fori_loop(0, n_tiles, body, 0)


def _mix_ffn_kernel(x_ref, a_ref, u_ref, uprev_ref, unext_ref, pw_ref, pb_ref, ps_ref,
                    wout_ref, g2_ref, wg_ref, wu_ref, wd_ref, o_ref, ext_ref, *, seq, d_ff):
    rows = x_ref.shape[0]
    tiles_per_seq = seq // rows
    st = pl.program_id(0) % tiles_per_seq
    t0 = st * rows

    ext_ref[0:POOL_HALO, :] = jnp.where(st > 0, uprev_ref[...], 0.0)
    ext_ref[POOL_HALO:POOL_HALO + rows, :] = u_ref[...]
    ext_ref[POOL_HALO + rows:, :] = jnp.where(st < tiles_per_seq - 1, unext_ref[...], 0.0)

    t = t0 + lax.broadcasted_iota(jnp.int32, (rows, LANES), 0)
    mixed = [a_ref[...]]
    for g, w in enumerate(POOL_WINDOWS):
        cs = slice(g * POOL_GROUP_DIM, (g + 1) * POOL_GROUP_DIM)
        win = ext_ref[POOL_HALO - w // 2:POOL_HALO - w // 2 + rows, cs]
        for d in range(1 - w // 2, w // 2):
            win = win + ext_ref[POOL_HALO + d:POOL_HALO + d + rows, cs]
        lo = jnp.maximum(t - w // 2, 0)
        hi = jnp.minimum(t - w // 2 + w, seq)
        cnt = (hi - lo).astype(F32)
        pooled = (win / cnt - ext_ref[POOL_HALO:POOL_HALO + rows, cs]).astype(BF16)
        y = jnp.dot(pooled, pw_ref[g], preferred_element_type=F32) + pb_ref[g]
        mixed.append((y * ps_ref[:, cs]).astype(BF16))
    mix = jnp.concatenate(mixed, axis=1)

    x1 = x_ref[...] + jnp.dot(mix, wout_ref[...], preferred_element_type=F32)
    ms = jnp.mean(x1 * x1, axis=-1, keepdims=True)
    h = (x1 * lax.rsqrt(ms + EPS) * g2_ref[...]).astype(BF16)

    acc = x1
    for c0 in range(0, d_ff, FF_CHUNK):
        c1 = min(c0 + FF_CHUNK, d_ff)
        gate = jnp.dot(h, wg_ref[:, c0:c1], preferred_element_type=F32)
        up = jnp.dot(h, wu_ref[:, c0:c1], preferred_element_type=F32)
        act = (gate * jax.nn.sigmoid(gate) * up).astype(BF16)
        acc = acc + jnp.dot(act, wd_ref[c0:c1, :], preferred_element_type=F32)
    o_ref[...] = acc


def _const_spec(shape):
    nd = len(shape)
    return pl.BlockSpec(shape, lambda *_: (0,) * nd)


def _layer(x2, seq, norm1_g, w_in, q_norm_g, k_norm_g, pool_w, pool_b, pool_scale,
           w_out, norm2_g, w_gate, w_up, w_down):
    n_tok = x2.shape[0]
    batch = n_tok // seq
    d_ff = w_gate.shape[1]
    assert seq % ROW_TILE == 0 and seq % Q_TILE == 0 and ROW_TILE % POOL_HALO == 0
    tiles_per_seq = seq // ROW_TILE
    n_row_tiles = n_tok // ROW_TILE

    cos, sin = _rope_tables(seq)
    seg = jnp.asarray(_segment_ones(256), BF16)
    qg = jnp.tile(q_norm_g.astype(F32), LANES // HEAD_DIM)[None, :]
    kg = jnp.tile(k_norm_g.astype(F32), LANES // HEAD_DIM)[None, :]
    q_scale = HEAD_DIM ** -0.5 * math.log2(math.e)

    params = pltpu.CompilerParams(dimension_semantics=("arbitrary",),
                                  vmem_limit_bytes=VMEM_LIMIT_BYTES)

    q, krep, vext, u = pl.pallas_call(
        functools.partial(_inproj_kernel, q_scale=q_scale),
        grid=(n_row_tiles,),
        in_specs=[
            pl.BlockSpec((ROW_TILE, D_MODEL), lambda i: (i, 0)),
            _const_spec((1, D_MODEL)),
            _const_spec((D_MODEL, IN_WIDTH)),
            _const_spec((1, LANES)),
            _const_spec((1, LANES)),
            pl.BlockSpec((ROW_TILE, LANES), lambda i: (i % tiles_per_seq, 0)),
            pl.BlockSpec((ROW_TILE, LANES), lambda i: (i % tiles_per_seq, 0)),
            _const_spec((256, 256)),
        ],
        out_specs=[
            pl.BlockSpec((ROW_TILE, ATTN_WIDTH), lambda i: (i, 0)),
            pl.BlockSpec((ROW_TILE, N_KV_HEADS * GROUP_WIDTH), lambda i: (i, 0)),
            pl.BlockSpec((ROW_TILE, N_KV_HEADS * LANES), lambda i: (i, 0)),
            pl.BlockSpec((ROW_TILE, POOL_WIDTH), lambda i: (i, 0)),
        ],
        out_shape=[
            jax.ShapeDtypeStruct((n_tok, ATTN_WIDTH), BF16),
            jax.ShapeDtypeStruct((n_tok, N_KV_HEADS * GROUP_WIDTH), BF16),
            jax.ShapeDtypeStruct((n_tok, N_KV_HEADS * LANES), BF16),
            jax.ShapeDtypeStruct((n_tok, POOL_WIDTH), F32),
        ],
        compiler_params=params,
        name="inproj",
    )(x2, norm1_g.astype(F32)[None, :], w_in.astype(BF16), qg, kg,
      jnp.asarray(cos), jnp.asarray(sin), seg)

    attn = pl.pallas_call(
        functools.partial(_attention_kernel, seq=seq),
        grid=(batch, N_KV_HEADS),
        in_specs=[
            pl.BlockSpec((seq, GROUP_WIDTH), lambda b, g: (b, g)),
            pl.BlockSpec((seq, GROUP_WIDTH), lambda b, g: (b, g)),
            pl.BlockSpec((seq, LANES), lambda b, g: (b, g)),
        ],
        out_specs=pl.BlockSpec((seq, GROUP_WIDTH), lambda b, g: (b, g)),
        out_shape=jax.ShapeDtypeStruct((n_tok, ATTN_WIDTH), BF16),
        compiler_params=pltpu.CompilerParams(dimension_semantics=("arbitrary", "arbitrary"),
                                             vmem_limit_bytes=VMEM_LIMIT_BYTES),
        name="attention",
    )(q, krep, vext)

    halo_blocks = ROW_TILE // POOL_HALO
    last_halo = n_tok // POOL_HALO - 1
    out = pl.pallas_call(
        functools.partial(_mix_ffn_kernel, seq=seq, d_ff=d_ff),
        grid=(n_row_tiles,),
        in_specs=[
            pl.BlockSpec((ROW_TILE, D_MODEL), lambda i: (i, 0)),
            pl.BlockSpec((ROW_TILE, ATTN_WIDTH), lambda i: (i, 0)),
            pl.BlockSpec((ROW_TILE, POOL_WIDTH), lambda i: (i, 0)),
            pl.BlockSpec((POOL_HALO, POOL_WIDTH),
                         lambda i: (jnp.maximum(i * halo_blocks - 1, 0), 0)),
            pl.BlockSpec((POOL_HALO, POOL_WIDTH),
                         lambda i: (jnp.minimum((i + 1) * halo_blocks, last_halo), 0)),
            _const_spec((N_POOL_GROUPS, POOL_GROUP_DIM, POOL_GROUP_DIM)),
            _const_spec((N_POOL_GROUPS, 1, POOL_GROUP_DIM)),
            _const_spec((1, POOL_WIDTH)),
            _const_spec((D_MODEL, D_MODEL)),
            _const_spec((1, D_MODEL)),
            _const_spec((D_MODEL, d_ff)),
            _const_spec((D_MODEL, d_ff)),
            _const_spec((d_ff, D_MODEL)),
        ],
        out_specs=pl.BlockSpec((ROW_TILE, D_MODEL), lambda i: (i, 0)),
        out_shape=jax.ShapeDtypeStruct((n_tok, D_MODEL), F32),
        scratch_shapes=[pltpu.VMEM((ROW_TILE + 2 * POOL_HALO, POOL_WIDTH), F32)],
        compiler_params=params,
        name="mix_ffn",
    )(x2, attn, u, u, u, pool_w.astype(BF16), pool_b.astype(F32)[:, None, :],
      pool_scale.astype(F32)[None, :], w_out.astype(BF16), norm2_g.astype(F32)[None, :],
      w_gate.astype(BF16), w_up.astype(BF16), w_down.astype(BF16))
    return out


def kernel(x, norm1_g, w_in, q_norm_g, k_norm_g, pool_w, pool_b, pool_scale,
           w_out, norm2_g, w_gate, w_up, w_down):
    batch, seq, d_model = x.shape
    assert d_model == D_MODEL and w_in.shape[-1] == IN_WIDTH
    x2 = x.reshape(batch * seq, d_model)
    for l in range(norm1_g.shape[0]):
        x2 = _layer(x2, seq, norm1_g[l], w_in[l], q_norm_g[l], k_norm_g[l], pool_w[l],
                    pool_b[l], pool_scale[l], w_out[l], norm2_g[l], w_gate[l], w_up[l],
                    w_down[l])
    return x2.reshape(batch, seq, d_model)
```

```python
import functools
import math

import jax
import jax.numpy as jnp
import numpy as np
from jax import lax
from jax.experimental import pallas as pl
from jax.experimental.pallas import tpu as pltpu

D_MODEL = 1024
HEAD_DIM = 64
N_Q_HEADS = 8
N_KV_HEADS = 2
Q_PER_KV = N_Q_HEADS // N_KV_HEADS
ATTN_WIDTH = N_Q_HEADS * HEAD_DIM
KV_WIDTH = N_KV_HEADS * HEAD_DIM
GROUP_WIDTH = Q_PER_KV * HEAD_DIM
ROPE_PAIRS = HEAD_DIM // 4
ROPE_THETA = 10000.0
POOL_WINDOWS = (2, 4, 8, 16)
N_POOL_GROUPS = len(POOL_WINDOWS)
POOL_WIDTH = D_MODEL - ATTN_WIDTH
POOL_GROUP_DIM = POOL_WIDTH // N_POOL_GROUPS
IN_WIDTH = ATTN_WIDTH + 2 * KV_WIDTH + POOL_WIDTH
GRID_W = 64
EPS = 1e-6

LANES = 128
POOL_HALO = 8
VMEM_LIMIT_BYTES = 56 * 1024 * 1024

ROW_TILE = 512
Q_TILE = 128
FF_CHUNK = 512

F32 = jnp.float32
BF16 = jnp.bfloat16


def _lane_iota(shape):
    return lax.broadcasted_iota(jnp.int32, shape, len(shape) - 1)


def _rope_tables(seq):
    t = np.arange(seq)
    row = (t // GRID_W).astype(np.float64)
    col = (t % GRID_W).astype(np.float64)
    inv_freq = ROPE_THETA ** (-np.arange(ROPE_PAIRS, dtype=np.float64) / ROPE_PAIRS)
    ang_row = row[:, None] * inv_freq[None, :]
    ang_col = col[:, None] * inv_freq[None, :]
    ang = np.concatenate([ang_row, ang_row, ang_col, ang_col], axis=1)
    sign = np.concatenate([-np.ones(ROPE_PAIRS), np.ones(ROPE_PAIRS)] * 2)
    cos = np.tile(np.cos(ang), (1, 2)).astype(np.float32)
    sin = np.tile(np.sin(ang) * sign[None, :], (1, 2)).astype(np.float32)
    return cos, sin


def _segment_ones(width):
    idx = np.arange(width) // HEAD_DIM
    return (idx[:, None] == idx[None, :]).astype(np.float32)


def _head_sumsq(x, seg):
    sq = x * x
    hi = sq.astype(BF16)
    lo = (sq - hi.astype(F32)).astype(BF16)
    return (jnp.dot(hi, seg, preferred_element_type=F32)
            + jnp.dot(lo, seg, preferred_element_type=F32))


def _rope_partner(x):
    lane = _lane_iota(x.shape)
    fwd = pltpu.roll(x, LANES - ROPE_PAIRS, axis=1)
    bwd = pltpu.roll(x, ROPE_PAIRS, axis=1)
    return jnp.where((lane & ROPE_PAIRS) == 0, fwd, bwd)


def _norm_rope(x, sumsq, gain, cos, sin):
    y = x * lax.rsqrt(sumsq * (1.0 / HEAD_DIM) + EPS) * gain
    return y * cos + _rope_partner(y) * sin


def _inproj_kernel(x_ref, g1_ref, win_ref, qg_ref, kg_ref, cos_ref, sin_ref, seg_ref,
                   q_ref, krep_ref, vext_ref, u_ref, *, q_scale):
    x = x_ref[...]
    ms = jnp.mean(x * x, axis=-1, keepdims=True)
    h = (x * lax.rsqrt(ms + EPS) * g1_ref[...]).astype(BF16)
    proj = jnp.dot(h, win_ref[...], preferred_element_type=F32)

    cos = cos_ref[...]
    sin = sin_ref[...]
    seg = seg_ref[...]
    qg = qg_ref[...]
    kg = kg_ref[...]

    for g in range(N_KV_HEADS):
        qh = proj[:, g * GROUP_WIDTH:(g + 1) * GROUP_WIDTH]
        ss = _head_sumsq(qh, seg)
        for j in range(GROUP_WIDTH // LANES):
            sl = slice(j * LANES, (j + 1) * LANES)
            qr = _norm_rope(qh[:, sl], ss[:, sl], qg, cos, sin) * q_scale
            q_ref[0, g, :, sl] = qr.astype(BF16)

    k = proj[:, ATTN_WIDTH:ATTN_WIDTH + KV_WIDTH]
    kss = _head_sumsq(k, seg[:KV_WIDTH, :KV_WIDTH])
    kr = _norm_rope(k, kss, kg, cos, sin)
    lane = _lane_iota(kr.shape)
    first = lane < HEAD_DIM
    kswap = pltpu.roll(kr, HEAD_DIM, axis=1)
    k00 = jnp.where(first, kr, kswap).astype(BF16)
    k11 = jnp.where(first, kswap, kr).astype(BF16)
    for j in range(GROUP_WIDTH // LANES):
        krep_ref[0, 0, :, j * LANES:(j + 1) * LANES] = k00
        krep_ref[0, 1, :, j * LANES:(j + 1) * LANES] = k11

    v = proj[:, ATTN_WIDTH + KV_WIDTH:ATTN_WIDTH + 2 * KV_WIDTH]
    vswap = pltpu.roll(v, HEAD_DIM, axis=1)
    ones_col = jnp.where(lane == HEAD_DIM, 1.0, 0.0)
    vext_ref[0, 0] = jnp.where(first, v, ones_col).astype(BF16)
    vext_ref[0, 1] = jnp.where(first, vswap, ones_col).astype(BF16)

    u_ref[...] = proj[:, ATTN_WIDTH + 2 * KV_WIDTH:]


def _attention_kernel(q_ref, krep_ref, vext_ref, o_ref, s0_ref, s1_ref, p0_ref, p1_ref, *, seq):
    tiles_per_group = seq // Q_TILE
    n_tiles = N_KV_HEADS * tiles_per_group
    lane_q = _lane_iota((Q_TILE, GROUP_WIDTH))
    lane_o = _lane_iota((Q_TILE, LANES))

    def tile_pos(t):
        g = t // tiles_per_group
        r0 = pl.multiple_of((t % tiles_per_group) * Q_TILE, Q_TILE)
        return g, r0

    def scores(t, s_ref):
        g, r0 = tile_pos(t)
        qt = q_ref[0, g, pl.ds(r0, Q_TILE), :]
        zero = jnp.zeros_like(qt)
        qm = jnp.concatenate(
            [jnp.where((lane_q // HEAD_DIM) == hh, qt, zero) for hh in range(Q_PER_KV)], axis=0)
        s_ref[...] = lax.dot_general(qm, krep_ref[0, g], (((1,), (1,)), ((), ())),
                                     preferred_element_type=F32)

    def probs(s_ref, p_ref):
        s = s_ref[...]
        m = jnp.max(s, axis=-1, keepdims=True)
        p_ref[...] = jnp.exp2(s - m).astype(BF16)

    def output(t, p_ref):
        g, r0 = tile_pos(t)
        oe = jnp.dot(p_ref[...], vext_ref[0, g], preferred_element_type=F32)
        on = oe / oe[:, HEAD_DIM:HEAD_DIM + 1]
        heads = [on[hh * Q_TILE:(hh + 1) * Q_TILE] for hh in range(Q_PER_KV)]
        for j in range(Q_PER_KV // 2):
            pair = jnp.where(lane_o < HEAD_DIM, heads[2 * j],
                             pltpu.roll(heads[2 * j + 1], HEAD_DIM, axis=1))
            o_ref[0, g, pl.ds(r0, Q_TILE), j * LANES:(j + 1) * LANES] = pair.astype(BF16)

    scores(0, s0_ref)
    scores(1, s1_ref)
    probs(s0_ref, p0_ref)

    def body(jj, carry):
        t = 2 * jj
        scores(t + 2, s0_ref)
        probs(s1_ref, p1_ref)
        output(t, p0_ref)
        scores(t + 3, s1_ref)
        probs(s0_ref, p0_ref)
        output(t + 1, p1_ref)
        return carry

    lax.fori_loop(0, n_tiles // 2 - 1, body, 0)

    probs(s1_ref, p1_ref)
    output(n_tiles - 2, p0_ref)
    output(n_tiles - 1, p1_ref)


def _mix_ffn_kernel(x_ref, a_ref, u_ref, uprev_ref, unext_ref, pw_ref, pb_ref, ps_ref,
                    wout_ref, g2_ref, wg_ref, wu_ref, wd_ref, o_ref, ext_ref, *, seq, d_ff):
    rows = x_ref.shape[0]
    tiles_per_seq = seq // rows
    st = pl.program_id(0) % tiles_per_seq
    t0 = st * rows

    ext_ref[0:POOL_HALO, :] = jnp.where(st > 0, uprev_ref[...], 0.0)
    ext_ref[POOL_HALO:POOL_HALO + rows, :] = u_ref[...]
    ext_ref[POOL_HALO + rows:, :] = jnp.where(st < tiles_per_seq - 1, unext_ref[...], 0.0)

    t = t0 + lax.broadcasted_iota(jnp.int32, (rows, LANES), 0)
    mixed = [a_ref[0, g] for g in range(N_KV_HEADS)]
    for g, w in enumerate(POOL_WINDOWS):
        cs = slice(g * POOL_GROUP_DIM, (g + 1) * POOL_GROUP_DIM)
        win = ext_ref[POOL_HALO - w // 2:POOL_HALO - w // 2 + rows, cs]
        for d in range(1 - w // 2, w // 2):
            win = win + ext_ref[POOL_HALO + d:POOL_HALO + d + rows, cs]
        lo = jnp.maximum(t - w // 2, 0)
        hi = jnp.minimum(t - w // 2 + w, seq)
        cnt = (hi - lo).astype(F32)
        pooled = (win / cnt - ext_ref[POOL_HALO:POOL_HALO + rows, cs]).astype(BF16)
        y = jnp.dot(pooled, pw_ref[g], preferred_element_type=F32) + pb_ref[g]
        mixed.append((y * ps_ref[:, cs]).astype(BF16))
    mix = jnp.concatenate(mixed, axis=1)

    x1 = x_ref[...] + jnp.dot(mix, wout_ref[...], preferred_element_type=F32)
    ms = jnp.mean(x1 * x1, axis=-1, keepdims=True)
    h = (x1 * lax.rsqrt(ms + EPS) * g2_ref[...]).astype(BF16)

    acc = x1
    for c0 in range(0, d_ff, FF_CHUNK):
        c1 = min(c0 + FF_CHUNK, d_ff)
        gate = jnp.dot(h, wg_ref[:, c0:c1], preferred_element_type=F32)
        up = jnp.dot(h, wu_ref[:, c0:c1], preferred_element_type=F32)
        act = (gate * jax.nn.sigmoid(gate) * up).astype(BF16)
        acc = acc + jnp.dot(act, wd_ref[c0:c1, :], preferred_element_type=F32)
    o_ref[...] = acc


def _const_spec(shape):
    nd = len(shape)
    return pl.BlockSpec(shape, lambda *_: (0,) * nd)


def _layer(x2, seq, norm1_g, w_in, q_norm_g, k_norm_g, pool_w, pool_b, pool_scale,
           w_out, norm2_g, w_gate, w_up, w_down):
    n_tok = x2.shape[0]
    batch = n_tok // seq
    d_ff = w_gate.shape[1]
    assert seq % ROW_TILE == 0 and seq % Q_TILE == 0 and ROW_TILE % POOL_HALO == 0
    tiles_per_seq = seq // ROW_TILE
    n_row_tiles = n_tok // ROW_TILE

    cos, sin = _rope_tables(seq)
    seg = jnp.asarray(_segment_ones(256), BF16)
    qg = jnp.tile(q_norm_g.astype(F32), LANES // HEAD_DIM)[None, :]
    kg = jnp.tile(k_norm_g.astype(F32), LANES // HEAD_DIM)[None, :]
    q_scale = HEAD_DIM ** -0.5 * math.log2(math.e)

    params = pltpu.CompilerParams(dimension_semantics=("arbitrary",),
                                  vmem_limit_bytes=VMEM_LIMIT_BYTES)

    def grouped_rows(width):
        return pl.BlockSpec((1, N_KV_HEADS, ROW_TILE, width),
                            lambda i: (i // tiles_per_seq, 0, i % tiles_per_seq, 0))

    def grouped_seq(width):
        return pl.BlockSpec((1, N_KV_HEADS, seq, width), lambda b: (b, 0, 0, 0))

    q, krep, vext, u = pl.pallas_call(
        functools.partial(_inproj_kernel, q_scale=q_scale),
        grid=(n_row_tiles,),
        in_specs=[
            pl.BlockSpec((ROW_TILE, D_MODEL), lambda i: (i, 0)),
            _const_spec((1, D_MODEL)),
            _const_spec((D_MODEL, IN_WIDTH)),
            _const_spec((1, LANES)),
            _const_spec((1, LANES)),
            pl.BlockSpec((ROW_TILE, LANES), lambda i: (i % tiles_per_seq, 0)),
            pl.BlockSpec((ROW_TILE, LANES), lambda i: (i % tiles_per_seq, 0)),
            _const_spec((256, 256)),
        ],
        out_specs=[
            grouped_rows(GROUP_WIDTH),
            grouped_rows(GROUP_WIDTH),
            grouped_rows(LANES),
            pl.BlockSpec((ROW_TILE, POOL_WIDTH), lambda i: (i, 0)),
        ],
        out_shape=[
            jax.ShapeDtypeStruct((batch, N_KV_HEADS, seq, GROUP_WIDTH), BF16),
            jax.ShapeDtypeStruct((batch, N_KV_HEADS, seq, GROUP_WIDTH), BF16),
            jax.ShapeDtypeStruct((batch, N_KV_HEADS, seq, LANES), BF16),
            jax.ShapeDtypeStruct((n_tok, POOL_WIDTH), F32),
        ],
        compiler_params=params,
        name="inproj",
    )(x2, norm1_g.astype(F32)[None, :], w_in.astype(BF16), qg, kg,
      jnp.asarray(cos), jnp.asarray(sin), seg)

    attn = pl.pallas_call(
        functools.partial(_attention_kernel, seq=seq),
        grid=(batch,),
        in_specs=[grouped_seq(GROUP_WIDTH), grouped_seq(GROUP_WIDTH), grouped_seq(LANES)],
        out_specs=grouped_seq(GROUP_WIDTH),
        out_shape=jax.ShapeDtypeStruct((batch, N_KV_HEADS, seq, GROUP_WIDTH), BF16),
        scratch_shapes=[pltpu.VMEM((Q_PER_KV * Q_TILE, seq), F32),
                        pltpu.VMEM((Q_PER_KV * Q_TILE, seq), F32),
                        pltpu.VMEM((Q_PER_KV * Q_TILE, seq), BF16),
                        pltpu.VMEM((Q_PER_KV * Q_TILE, seq), BF16)],
        compiler_params=params,
        name="attention",
    )(q, krep, vext)

    halo_blocks = ROW_TILE // POOL_HALO
    last_halo = n_tok // POOL_HALO - 1
    out = pl.pallas_call(
        functools.partial(_mix_ffn_kernel, seq=seq, d_ff=d_ff),
        grid=(n_row_tiles,),
        in_specs=[
            pl.BlockSpec((ROW_TILE, D_MODEL), lambda i: (i, 0)),
            grouped_rows(GROUP_WIDTH),
            pl.BlockSpec((ROW_TILE, POOL_WIDTH), lambda i: (i, 0)),
            pl.BlockSpec((POOL_HALO, POOL_WIDTH),
                         lambda i: (jnp.maximum(i * halo_blocks - 1, 0), 0)),
            pl.BlockSpec((POOL_HALO, POOL_WIDTH),
                         lambda i: (jnp.minimum((i + 1) * halo_blocks, last_halo), 0)),
            _const_spec((N_POOL_GROUPS, POOL_GROUP_DIM, POOL_GROUP_DIM)),
            _const_spec((N_POOL_GROUPS, 1, POOL_GROUP_DIM)),
            _const_spec((1, POOL_WIDTH)),
            _const_spec((D_MODEL, D_MODEL)),
            _const_spec((1, D_MODEL)),
            _const_spec((D_MODEL, d_ff)),
            _const_spec((D_MODEL, d_ff)),
            _const_spec((d_ff, D_MODEL)),
        ],
        out_specs=pl.BlockSpec((ROW_TILE, D_MODEL), lambda i: (i, 0)),
        out_shape=jax.ShapeDtypeStruct((n_tok, D_MODEL), F32),
        scratch_shapes=[pltpu.VMEM((ROW_TILE + 2 * POOL_HALO, POOL_WIDTH), F32)],
        compiler_params=params,
        name="mix_ffn",
    )(x2, attn, u, u, u, pool_w.astype(BF16), pool_b.astype(F32)[:, None, :],
      pool_scale.astype(F32)[None, :], w_out.astype(BF16), norm2_g.astype(F32)[None, :],
      w_gate.astype(BF16), w_up.astype(BF16), w_down.astype(BF16))
    return out


def kernel(x, norm1_g, w_in, q_norm_g, k_norm_g, pool_w, pool_b, pool_scale,
           w_out, norm2_g, w_gate, w_up, w_down):
    batch, seq, d_model = x.shape
    assert d_model == D_MODEL and w_in.shape[-1] == IN_WIDTH
    x2 = x.reshape(batch * seq, d_model)
    for l in range(norm1_g.shape[0]):
        x2 = _layer(x2, seq, norm1_g[l], w_in[l], q_norm_g[l], k_norm_g[l], pool_w[l],
                    pool_b[l], pool_scale[l], w_out[l], norm2_g[l], w_gate[l], w_up[l],
                    w_down[l])
    return x2.reshape(batch, seq, d_model)
```

```python
import functools
import math

import jax
import jax.numpy as jnp
import numpy as np
from jax import lax
from jax.experimental import pallas as pl
from jax.experimental.pallas import tpu as pltpu

D_MODEL = 1024
HEAD_DIM = 64
N_Q_HEADS = 8
N_KV_HEADS = 2
Q_PER_KV = N_Q_HEADS // N_KV_HEADS
ATTN_WIDTH = N_Q_HEADS * HEAD_DIM
KV_WIDTH = N_KV_HEADS * HEAD_DIM
GROUP_WIDTH = Q_PER_KV * HEAD_DIM
ROPE_PAIRS = HEAD_DIM // 4
ROPE_THETA = 10000.0
POOL_WINDOWS = (2, 4, 8, 16)
N_POOL_GROUPS = len(POOL_WINDOWS)
POOL_WIDTH = D_MODEL - ATTN_WIDTH
POOL_GROUP_DIM = POOL_WIDTH // N_POOL_GROUPS
IN_WIDTH = ATTN_WIDTH + 2 * KV_WIDTH + POOL_WIDTH
GRID_W = 64
EPS = 1e-6

LANES = 128
POOL_HALO = 8
VMEM_LIMIT_BYTES = 56 * 1024 * 1024

ROW_TILE = 512
Q_TILE = 128
ATTN_PAIRS_PER_STEP = 3
FF_CHUNK = 512

F32 = jnp.float32
BF16 = jnp.bfloat16


def _lane_iota(shape):
    return lax.broadcasted_iota(jnp.int32, shape, len(shape) - 1)


def _rope_tables(seq):
    t = np.arange(seq)
    row = (t // GRID_W).astype(np.float64)
    col = (t % GRID_W).astype(np.float64)
    inv_freq = ROPE_THETA ** (-np.arange(ROPE_PAIRS, dtype=np.float64) / ROPE_PAIRS)
    ang_row = row[:, None] * inv_freq[None, :]
    ang_col = col[:, None] * inv_freq[None, :]
    ang = np.concatenate([ang_row, ang_row, ang_col, ang_col], axis=1)
    sign = np.concatenate([-np.ones(ROPE_PAIRS), np.ones(ROPE_PAIRS)] * 2)
    cos = np.tile(np.cos(ang), (1, 2)).astype(np.float32)
    sin = np.tile(np.sin(ang) * sign[None, :], (1, 2)).astype(np.float32)
    return cos, sin


def _segment_ones(width):
    idx = np.arange(width) // HEAD_DIM
    return (idx[:, None] == idx[None, :]).astype(np.float32)


def _head_sumsq(x, seg):
    sq = x * x
    hi = sq.astype(BF16)
    lo = (sq - hi.astype(F32)).astype(BF16)
    return (jnp.dot(hi, seg, preferred_element_type=F32)
            + jnp.dot(lo, seg, preferred_element_type=F32))


def _rope_partner(x):
    lane = _lane_iota(x.shape)
    fwd = pltpu.roll(x, LANES - ROPE_PAIRS, axis=1)
    bwd = pltpu.roll(x, ROPE_PAIRS, axis=1)
    return jnp.where((lane & ROPE_PAIRS) == 0, fwd, bwd)


def _norm_rope(x, sumsq, gain, cos, sin):
    y = x * lax.rsqrt(sumsq * (1.0 / HEAD_DIM) + EPS) * gain
    return y * cos + _rope_partner(y) * sin


def _inproj_kernel(x_ref, g1_ref, win_ref, qg_ref, kg_ref, cos_ref, sin_ref, seg_ref,
                   q_ref, krep_ref, vext_ref, u_ref, *, q_scale):
    x = x_ref[...]
    ms = jnp.mean(x * x, axis=-1, keepdims=True)
    h = (x * lax.rsqrt(ms + EPS) * g1_ref[...]).astype(BF16)
    proj = jnp.dot(h, win_ref[...], preferred_element_type=F32)

    cos = cos_ref[...]
    sin = sin_ref[...]
    seg = seg_ref[...]
    qg = qg_ref[...]
    kg = kg_ref[...]

    for g in range(N_KV_HEADS):
        qh = proj[:, g * GROUP_WIDTH:(g + 1) * GROUP_WIDTH]
        ss = _head_sumsq(qh, seg)
        for j in range(GROUP_WIDTH // LANES):
            sl = slice(j * LANES, (j + 1) * LANES)
            qr = _norm_rope(qh[:, sl], ss[:, sl], qg, cos, sin) * q_scale
            q_ref[0, g, :, sl] = qr.astype(BF16)

    k = proj[:, ATTN_WIDTH:ATTN_WIDTH + KV_WIDTH]
    kss = _head_sumsq(k, seg[:KV_WIDTH, :KV_WIDTH])
    kr = _norm_rope(k, kss, kg, cos, sin)
    lane = _lane_iota(kr.shape)
    first = lane < HEAD_DIM
    kswap = pltpu.roll(kr, HEAD_DIM, axis=1)
    k00 = jnp.where(first, kr, kswap).astype(BF16)
    k11 = jnp.where(first, kswap, kr).astype(BF16)
    for j in range(GROUP_WIDTH // LANES):
        krep_ref[0, 0, :, j * LANES:(j + 1) * LANES] = k00
        krep_ref[0, 1, :, j * LANES:(j + 1) * LANES] = k11

    v = proj[:, ATTN_WIDTH + KV_WIDTH:ATTN_WIDTH + 2 * KV_WIDTH]
    vswap = pltpu.roll(v, HEAD_DIM, axis=1)
    ones_col = jnp.where(lane == HEAD_DIM, 1.0, 0.0)
    vext_ref[0, 0] = jnp.where(first, v, ones_col).astype(BF16)
    vext_ref[0, 1] = jnp.where(first, vswap, ones_col).astype(BF16)

    u_ref[...] = proj[:, ATTN_WIDTH + 2 * KV_WIDTH:]


def _attention_kernel(q_ref, krep_ref, vext_ref, o_ref, s0_ref, s1_ref, p0_ref, p1_ref, *, seq):
    tiles_per_group = seq // Q_TILE
    n_tiles = N_KV_HEADS * tiles_per_group
    lane_q = _lane_iota((Q_TILE, GROUP_WIDTH))
    lane_o = _lane_iota((Q_TILE, LANES))

    def tile_pos(t):
        g = t // tiles_per_group
        r0 = pl.multiple_of((t % tiles_per_group) * Q_TILE, Q_TILE)
        return g, r0

    def scores(t, s_ref):
        g, r0 = tile_pos(t)
        qt = q_ref[0, g, pl.ds(r0, Q_TILE), :]
        zero = jnp.zeros_like(qt)
        qm = jnp.concatenate(
            [jnp.where((lane_q // HEAD_DIM) == hh, qt, zero) for hh in range(Q_PER_KV)], axis=0)
        s_ref[...] = lax.dot_general(qm, krep_ref[0, g], (((1,), (1,)), ((), ())),
                                     preferred_element_type=F32)

    def probs(s_ref, p_ref):
        s = s_ref[...]
        m = jnp.max(s, axis=-1, keepdims=True)
        p_ref[...] = jnp.exp2(s - m).astype(BF16)

    def output(t, p_ref):
        g, r0 = tile_pos(t)
        oe = jnp.dot(p_ref[...], vext_ref[0, g], preferred_element_type=F32)
        on = oe / oe[:, HEAD_DIM:HEAD_DIM + 1]
        heads = [on[hh * Q_TILE:(hh + 1) * Q_TILE] for hh in range(Q_PER_KV)]
        for j in range(Q_PER_KV // 2):
            pair = jnp.where(lane_o < HEAD_DIM, heads[2 * j],
                             pltpu.roll(heads[2 * j + 1], HEAD_DIM, axis=1))
            o_ref[0, g, pl.ds(r0, Q_TILE), j * LANES:(j + 1) * LANES] = pair.astype(BF16)

    scores(0, s0_ref)
    scores(1, s1_ref)
    probs(s0_ref, p0_ref)

    n_pairs = n_tiles // 2 - 1
    assert n_pairs % ATTN_PAIRS_PER_STEP == 0

    def body(jj, carry):
        for k in range(ATTN_PAIRS_PER_STEP):
            t = 2 * (jj * ATTN_PAIRS_PER_STEP + k)
            scores(t + 2, s0_ref)
            probs(s1_ref, p1_ref)
            output(t, p0_ref)
            scores(t + 3, s1_ref)
            probs(s0_ref, p0_ref)
            output(t + 1, p1_ref)
        return carry

    lax.fori_loop(0, n_pairs // ATTN_PAIRS_PER_STEP, body, 0)

    probs(s1_ref, p1_ref)
    output(n_tiles - 2, p0_ref)
    output(n_tiles - 1, p1_ref)


def _mix_ffn_kernel(x_ref, a_ref, u_ref, uprev_ref, unext_ref, pw_ref, pb_ref, ps_ref,
                    wout_ref, g2_ref, wg_ref, wu_ref, wd_ref, o_ref,
                    ext_ref, mixp_ref, x1a_ref, ha_ref, x1b_ref, hb_ref, *, seq, d_ff, n_tiles):
    rows = x_ref.shape[0]
    tiles_per_seq = seq // rows
    step = pl.program_id(0)

    st = step % tiles_per_seq

    def stage_windows():
        ext_ref[0:POOL_HALO, :] = jnp.where(st > 0, uprev_ref[...], 0.0)
        ext_ref[POOL_HALO:POOL_HALO + rows, :] = u_ref[...]
        ext_ref[POOL_HALO + rows:, :] = jnp.where(st < tiles_per_seq - 1, unext_ref[...], 0.0)

    def pool_group(g):
        n_ext = rows + 2 * POOL_HALO
        t = st * rows + lax.broadcasted_iota(jnp.int32, (rows, LANES), 0)
        w = POOL_WINDOWS[g]
        cs = slice(g * POOL_GROUP_DIM, (g + 1) * POOL_GROUP_DIM)
        ext = ext_ref[:, cs]
        b = ext
        k = 1
        while k < w:
            b = b + pltpu.roll(b, n_ext - k, axis=0)
            k *= 2
        win = pltpu.roll(b, w // 2, axis=0)[POOL_HALO:POOL_HALO + rows]
        lo = jnp.maximum(t - w // 2, 0)
        hi = jnp.minimum(t - w // 2 + w, seq)
        cnt = (hi - lo).astype(F32)
        pooled = (win / cnt - ext[POOL_HALO:POOL_HALO + rows]).astype(BF16)
        y = jnp.dot(pooled, pw_ref[g], preferred_element_type=F32) + pb_ref[g]
        mixp_ref[:, cs] = (y * ps_ref[:, cs]).astype(BF16)

    def project_out(x1_ref, h_ref):
        mix = jnp.concatenate([a_ref[0, g] for g in range(N_KV_HEADS)] + [mixp_ref[...]],
                              axis=1)
        x1 = x_ref[...] + jnp.dot(mix, wout_ref[...], preferred_element_type=F32)
        ms = jnp.mean(x1 * x1, axis=-1, keepdims=True)
        x1_ref[...] = x1
        h_ref[...] = (x1 * lax.rsqrt(ms + EPS) * g2_ref[...]).astype(BF16)

    def mixer_pieces(x1_ref, h_ref):
        return ([functools.partial(pool_group, g) for g in range(N_POOL_GROUPS)]
                + [functools.partial(project_out, x1_ref, h_ref)])

    bounds = [(c0, min(c0 + FF_CHUNK, d_ff)) for c0 in range(0, d_ff, FF_CHUNK)]

    def ffn(x1_ref, h_ref, between=()):
        assert len(between) < len(bounds)
        h = h_ref[...]
        acc = x1_ref[...]
        for c, (c0, c1) in enumerate(bounds):
            gate = jnp.dot(h, wg_ref[:, c0:c1], preferred_element_type=F32)
            up = jnp.dot(h, wu_ref[:, c0:c1], preferred_element_type=F32)
            act = (gate * jax.nn.sigmoid(gate) * up).astype(BF16)
            acc = acc + jnp.dot(act, wd_ref[c0:c1, :], preferred_element_type=F32)
            if c < len(between):
                between[c]()
        o_ref[...] = acc

    even = step % 2 == 0
    assert n_tiles % 2 == 0

    @pl.when(step == 0)
    def _():
        stage_windows()
        for piece in mixer_pieces(x1a_ref, ha_ref):
            piece()

    @pl.when(jnp.logical_and(even, jnp.logical_and(step > 0, step < n_tiles)))
    def _():
        stage_windows()
        ffn(x1b_ref, hb_ref, mixer_pieces(x1a_ref, ha_ref))

    @pl.when(jnp.logical_not(even))
    def _():
        stage_windows()
        ffn(x1a_ref, ha_ref, mixer_pieces(x1b_ref, hb_ref))

    @pl.when(step == n_tiles)
    def _():
        ffn(x1b_ref, hb_ref)


def _const_spec(shape):
    nd = len(shape)
    return pl.BlockSpec(shape, lambda *_: (0,) * nd)


def _layer(x2, seq, norm1_g, w_in, q_norm_g, k_norm_g, pool_w, pool_b, pool_scale,
           w_out, norm2_g, w_gate, w_up, w_down):
    n_tok = x2.shape[0]
    batch = n_tok // seq
    d_ff = w_gate.shape[1]
    assert seq % ROW_TILE == 0 and seq % Q_TILE == 0 and ROW_TILE % POOL_HALO == 0
    tiles_per_seq = seq // ROW_TILE
    n_row_tiles = n_tok // ROW_TILE

    cos, sin = _rope_tables(seq)
    seg = jnp.asarray(_segment_ones(256), BF16)
    qg = jnp.tile(q_norm_g.astype(F32), LANES // HEAD_DIM)[None, :]
    kg = jnp.tile(k_norm_g.astype(F32), LANES // HEAD_DIM)[None, :]
    q_scale = HEAD_DIM ** -0.5 * math.log2(math.e)

    params = pltpu.CompilerParams(dimension_semantics=("arbitrary",),
                                  vmem_limit_bytes=VMEM_LIMIT_BYTES)

    def grouped_rows(width, tile=lambda i: i):
        return pl.BlockSpec((1, N_KV_HEADS, ROW_TILE, width),
                            lambda i: (tile(i) // tiles_per_seq, 0, tile(i) % tiles_per_seq, 0))

    def mix_tile(i):
        return jnp.minimum(i, n_row_tiles - 1)

    def grouped_seq(width):
        return pl.BlockSpec((1, N_KV_HEADS, seq, width), lambda b: (b, 0, 0, 0))

    q, krep, vext, u = pl.pallas_call(
        functools.partial(_inproj_kernel, q_scale=q_scale),
        grid=(n_row_tiles,),
        in_specs=[
            pl.BlockSpec((ROW_TILE, D_MODEL), lambda i: (i, 0)),
            _const_spec((1, D_MODEL)),
            _const_spec((D_MODEL, IN_WIDTH)),
            _const_spec((1, LANES)),
            _const_spec((1, LANES)),
            pl.BlockSpec((ROW_TILE, LANES), lambda i: (i % tiles_per_seq, 0)),
            pl.BlockSpec((ROW_TILE, LANES), lambda i: (i % tiles_per_seq, 0)),
            _const_spec((256, 256)),
        ],
        out_specs=[
            grouped_rows(GROUP_WIDTH),
            grouped_rows(GROUP_WIDTH),
            grouped_rows(LANES),
            pl.BlockSpec((ROW_TILE, POOL_WIDTH), lambda i: (i, 0)),
        ],
        out_shape=[
            jax.ShapeDtypeStruct((batch, N_KV_HEADS, seq, GROUP_WIDTH), BF16),
            jax.ShapeDtypeStruct((batch, N_KV_HEADS, seq, GROUP_WIDTH), BF16),
            jax.ShapeDtypeStruct((batch, N_KV_HEADS, seq, LANES), BF16),
            jax.ShapeDtypeStruct((n_tok, POOL_WIDTH), F32),
        ],
        compiler_params=params,
        name="inproj",
    )(x2, norm1_g.astype(F32)[None, :], w_in.astype(BF16), qg, kg,
      jnp.asarray(cos), jnp.asarray(sin), seg)

    attn = pl.pallas_call(
        functools.partial(_attention_kernel, seq=seq),
        grid=(batch,),
        in_specs=[grouped_seq(GROUP_WIDTH), grouped_seq(GROUP_WIDTH), grouped_seq(LANES)],
        out_specs=grouped_seq(GROUP_WIDTH),
        out_shape=jax.ShapeDtypeStruct((batch, N_KV_HEADS, seq, GROUP_WIDTH), BF16),
        scratch_shapes=[pltpu.VMEM((Q_PER_KV * Q_TILE, seq), F32),
                        pltpu.VMEM((Q_PER_KV * Q_TILE, seq), F32),
                        pltpu.VMEM((Q_PER_KV * Q_TILE, seq), BF16),
                        pltpu.VMEM((Q_PER_KV * Q_TILE, seq), BF16)],
        compiler_params=params,
        name="attention",
    )(q, krep, vext)

    halo_blocks = ROW_TILE // POOL_HALO
    last_halo = n_tok // POOL_HALO - 1
    out = pl.pallas_call(
        functools.partial(_mix_ffn_kernel, seq=seq, d_ff=d_ff, n_tiles=n_row_tiles),
        grid=(n_row_tiles + 1,),
        in_specs=[
            pl.BlockSpec((ROW_TILE, D_MODEL), lambda i: (mix_tile(i), 0)),
            grouped_rows(GROUP_WIDTH, mix_tile),
            pl.BlockSpec((ROW_TILE, POOL_WIDTH), lambda i: (mix_tile(i), 0)),
            pl.BlockSpec((POOL_HALO, POOL_WIDTH),
                         lambda i: (jnp.maximum(mix_tile(i) * halo_blocks - 1, 0), 0)),
            pl.BlockSpec((POOL_HALO, POOL_WIDTH),
                         lambda i: (jnp.minimum((mix_tile(i) + 1) * halo_blocks, last_halo), 0)),
            _const_spec((N_POOL_GROUPS, POOL_GROUP_DIM, POOL_GROUP_DIM)),
            _const_spec((N_POOL_GROUPS, 1, POOL_GROUP_DIM)),
            _const_spec((1, POOL_WIDTH)),
            _const_spec((D_MODEL, D_MODEL)),
            _const_spec((1, D_MODEL)),
            _const_spec((D_MODEL, d_ff)),
            _const_spec((D_MODEL, d_ff)),
            _const_spec((d_ff, D_MODEL)),
        ],
        out_specs=pl.BlockSpec((ROW_TILE, D_MODEL), lambda i: (jnp.maximum(i - 1, 0), 0)),
        out_shape=jax.ShapeDtypeStruct((n_tok, D_MODEL), F32),
        scratch_shapes=[pltpu.VMEM((ROW_TILE + 2 * POOL_HALO, POOL_WIDTH), F32),
                        pltpu.VMEM((ROW_TILE, POOL_WIDTH), BF16),
                        pltpu.VMEM((ROW_TILE, D_MODEL), F32), pltpu.VMEM((ROW_TILE, D_MODEL), BF16),
                        pltpu.VMEM((ROW_TILE, D_MODEL), F32), pltpu.VMEM((ROW_TILE, D_MODEL), BF16)],
        compiler_params=params,
        name="mix_ffn",
    )(x2, attn, u, u, u, pool_w.astype(BF16), pool_b.astype(F32)[:, None, :],
      pool_scale.astype(F32)[None, :], w_out.astype(BF16), norm2_g.astype(F32)[None, :],
      w_gate.astype(BF16), w_up.astype(BF16), w_down.astype(BF16))
    return out


def kernel(x, norm1_g, w_in, q_norm_g, k_norm_g, pool_w, pool_b, pool_scale,
           w_out, norm2_g, w_gate, w_up, w_down):
    batch, seq, d_model = x.shape
    assert d_model == D_MODEL and w_in.shape[-1] == IN_WIDTH
    x2 = x.reshape(batch * seq, d_model)
    for l in range(norm1_g.shape[0]):
        x2 = _layer(x2, seq, norm1_g[l], w_in[l], q_norm_g[l], k_norm_g[l], pool_w[l],
                    pool_b[l], pool_scale[l], w_out[l], norm2_g[l], w_gate[l], w_up[l],
                    w_down[l])
    return x2.reshape(batch, seq, d_model)
```

```python
import functools
import math

import jax
import jax.numpy as jnp
import numpy as np
from jax import lax
from jax.experimental import pallas as pl
from jax.experimental.pallas import tpu as pltpu

D_MODEL = 1024
HEAD_DIM = 64
N_Q_HEADS = 8
N_KV_HEADS = 2
Q_PER_KV = N_Q_HEADS // N_KV_HEADS
ATTN_WIDTH = N_Q_HEADS * HEAD_DIM
KV_WIDTH = N_KV_HEADS * HEAD_DIM
GROUP_WIDTH = Q_PER_KV * HEAD_DIM
ROPE_PAIRS = HEAD_DIM // 4
ROPE_THETA = 10000.0
POOL_WINDOWS = (2, 4, 8, 16)
N_POOL_GROUPS = len(POOL_WINDOWS)
POOL_WIDTH = D_MODEL - ATTN_WIDTH
POOL_GROUP_DIM = POOL_WIDTH // N_POOL_GROUPS
IN_WIDTH = ATTN_WIDTH + 2 * KV_WIDTH + POOL_WIDTH
GRID_W = 64
EPS = 1e-6

LANES = 128
POOL_HALO = 8
VMEM_LIMIT_BYTES = 56 * 1024 * 1024

ROW_TILE = 512
Q_TILE = 128
ATTN_PAIRS_PER_STEP = 3
FF_CHUNK = 512

F32 = jnp.float32
BF16 = jnp.bfloat16


def _lane_iota(shape):
    return lax.broadcasted_iota(jnp.int32, shape, len(shape) - 1)


def _rope_tables(seq):
    t = np.arange(seq)
    row = (t // GRID_W).astype(np.float64)
    col = (t % GRID_W).astype(np.float64)
    inv_freq = ROPE_THETA ** (-np.arange(ROPE_PAIRS, dtype=np.float64) / ROPE_PAIRS)
    ang_row = row[:, None] * inv_freq[None, :]
    ang_col = col[:, None] * inv_freq[None, :]
    ang = np.concatenate([ang_row, ang_row, ang_col, ang_col], axis=1)
    sign = np.concatenate([-np.ones(ROPE_PAIRS), np.ones(ROPE_PAIRS)] * 2)
    cos = np.tile(np.cos(ang), (1, 2)).astype(np.float32)
    sin = np.tile(np.sin(ang) * sign[None, :], (1, 2)).astype(np.float32)
    return cos, sin


def _segment_ones(width):
    idx = np.arange(width) // HEAD_DIM
    return (idx[:, None] == idx[None, :]).astype(np.float32)


def _head_sumsq(x, seg):
    sq = x * x
    hi = sq.astype(BF16)
    lo = (sq - hi.astype(F32)).astype(BF16)
    return (jnp.dot(hi, seg, preferred_element_type=F32)
            + jnp.dot(lo, seg, preferred_element_type=F32))


def _rope_partner(x):
    lane = _lane_iota(x.shape)
    fwd = pltpu.roll(x, LANES - ROPE_PAIRS, axis=1)
    bwd = pltpu.roll(x, ROPE_PAIRS, axis=1)
    return jnp.where((lane & ROPE_PAIRS) == 0, fwd, bwd)


def _norm_rope(x, sumsq, gain, cos, sin):
    y = x * lax.rsqrt(sumsq * (1.0 / HEAD_DIM) + EPS) * gain
    return y * cos + _rope_partner(y) * sin


def _inproj_kernel(x_ref, g1_ref, win_ref, qg_ref, kg_ref, cos_ref, sin_ref, seg_ref,
                   q_ref, krep_ref, vext_ref, u_ref, *, q_scale):
    x = x_ref[...]
    ms = jnp.mean(x * x, axis=-1, keepdims=True)
    h = (x * lax.rsqrt(ms + EPS) * g1_ref[...]).astype(BF16)
    proj = jnp.dot(h, win_ref[...], preferred_element_type=F32)

    cos = cos_ref[...]
    sin = sin_ref[...]
    seg = seg_ref[...]
    qg = qg_ref[...]
    kg = kg_ref[...]

    for g in range(N_KV_HEADS):
        qh = proj[:, g * GROUP_WIDTH:(g + 1) * GROUP_WIDTH]
        ss = _head_sumsq(qh, seg)
        for j in range(GROUP_WIDTH // LANES):
            sl = slice(j * LANES, (j + 1) * LANES)
            qr = _norm_rope(qh[:, sl], ss[:, sl], qg, cos, sin) * q_scale
            q_ref[0, g, :, sl] = qr.astype(BF16)

    k = proj[:, ATTN_WIDTH:ATTN_WIDTH + KV_WIDTH]
    kss = _head_sumsq(k, seg[:KV_WIDTH, :KV_WIDTH])
    kr = _norm_rope(k, kss, kg, cos, sin)
    lane = _lane_iota(kr.shape)
    first = lane < HEAD_DIM
    kswap = pltpu.roll(kr, HEAD_DIM, axis=1)
    k00 = jnp.where(first, kr, kswap).astype(BF16)
    k11 = jnp.where(first, kswap, kr).astype(BF16)
    for j in range(GROUP_WIDTH // LANES):
        krep_ref[0, 0, :, j * LANES:(j + 1) * LANES] = k00
        krep_ref[0, 1, :, j * LANES:(j + 1) * LANES] = k11

    v = proj[:, ATTN_WIDTH + KV_WIDTH:ATTN_WIDTH + 2 * KV_WIDTH]
    vswap = pltpu.roll(v, HEAD_DIM, axis=1)
    ones_col = jnp.where(lane == HEAD_DIM, 1.0, 0.0)
    vext_ref[0, 0] = jnp.where(first, v, ones_col).astype(BF16)
    vext_ref[0, 1] = jnp.where(first, vswap, ones_col).astype(BF16)

    u_ref[...] = proj[:, ATTN_WIDTH + 2 * KV_WIDTH:]


def _attention_kernel(q_ref, krep_ref, vext_ref, o_ref, s0_ref, s1_ref, p0_ref, p1_ref, *, seq):
    tiles_per_group = seq // Q_TILE
    n_tiles = N_KV_HEADS * tiles_per_group
    lane_q = _lane_iota((Q_TILE, GROUP_WIDTH))
    lane_o = _lane_iota((Q_TILE, LANES))

    def tile_pos(t):
        g = t // tiles_per_group
        r0 = pl.multiple_of((t % tiles_per_group) * Q_TILE, Q_TILE)
        return g, r0

    def scores(t, s_ref):
        g, r0 = tile_pos(t)
        qt = q_ref[0, g, pl.ds(r0, Q_TILE), :]
        zero = jnp.zeros_like(qt)
        qm = jnp.concatenate(
            [jnp.where((lane_q // HEAD_DIM) == hh, qt, zero) for hh in range(Q_PER_KV)], axis=0)
        s_ref[...] = lax.dot_general(qm, krep_ref[0, g], (((1,), (1,)), ((), ())),
                                     preferred_element_type=F32)

    def probs(s_ref, p_ref):
        s = s_ref[...]
        m = jnp.max(s, axis=-1, keepdims=True)
        p_ref[...] = jnp.exp2(s - m).astype(BF16)

    def output(t, p_ref):
        g, r0 = tile_pos(t)
        oe = jnp.dot(p_ref[...], vext_ref[0, g], preferred_element_type=F32)
        on = oe / oe[:, HEAD_DIM:HEAD_DIM + 1]
        heads = [on[hh * Q_TILE:(hh + 1) * Q_TILE] for hh in range(Q_PER_KV)]
        for j in range(Q_PER_KV // 2):
            pair = jnp.where(lane_o < HEAD_DIM, heads[2 * j],
                             pltpu.roll(heads[2 * j + 1], HEAD_DIM, axis=1))
            o_ref[0, g, pl.ds(r0, Q_TILE), j * LANES:(j + 1) * LANES] = pair.astype(BF16)

    scores(0, s0_ref)
    scores(1, s1_ref)
    probs(s0_ref, p0_ref)

    n_pairs = n_tiles // 2 - 1
    assert n_pairs % ATTN_PAIRS_PER_STEP == 0

    def body(jj, carry):
        for k in range(ATTN_PAIRS_PER_STEP):
            t = 2 * (jj * ATTN_PAIRS_PER_STEP + k)
            scores(t + 2, s0_ref)
            probs(s1_ref, p1_ref)
            output(t, p0_ref)
            scores(t + 3, s1_ref)
            probs(s0_ref, p0_ref)
            output(t + 1, p1_ref)
        return carry

    lax.fori_loop(0, n_pairs // ATTN_PAIRS_PER_STEP, body, 0)

    probs(s1_ref, p1_ref)
    output(n_tiles - 2, p0_ref)
    output(n_tiles - 1, p1_ref)


def _mix_ffn_kernel(x_ref, a_ref, u_ref, uprev_ref, unext_ref, pw_ref, pb_ref, ps_ref,
                    wout_ref, g2_ref, wg_ref, wu_ref, wd_ref, o_ref,
                    ext_ref, mixp_ref, x1_ref, h_ref, *, seq, d_ff, n_tiles):
    rows = x_ref.shape[0]
    tiles_per_seq = seq // rows
    step = pl.program_id(0)

    st = jnp.minimum(step, n_tiles - 1) % tiles_per_seq

    def stage_windows():
        ext_ref[0:POOL_HALO, :] = jnp.where(st > 0, uprev_ref[...], 0.0)
        ext_ref[POOL_HALO:POOL_HALO + rows, :] = u_ref[...]
        ext_ref[POOL_HALO + rows:, :] = jnp.where(st < tiles_per_seq - 1, unext_ref[...], 0.0)

    def pool_group(g):
        n_ext = rows + 2 * POOL_HALO
        t = st * rows + lax.broadcasted_iota(jnp.int32, (rows, LANES), 0)
        w = POOL_WINDOWS[g]
        cs = slice(g * POOL_GROUP_DIM, (g + 1) * POOL_GROUP_DIM)
        ext = ext_ref[:, cs]
        b = ext
        k = 1
        while k < w:
            b = b + pltpu.roll(b, n_ext - k, axis=0)
            k *= 2
        win = pltpu.roll(b, w // 2, axis=0)[POOL_HALO:POOL_HALO + rows]
        lo = jnp.maximum(t - w // 2, 0)
        hi = jnp.minimum(t - w // 2 + w, seq)
        cnt = (hi - lo).astype(F32)
        pooled = (win / cnt - ext[POOL_HALO:POOL_HALO + rows]).astype(BF16)
        y = jnp.dot(pooled, pw_ref[g], preferred_element_type=F32) + pb_ref[g]
        mixp_ref[:, cs] = (y * ps_ref[:, cs]).astype(BF16)

    def project_out():
        mix = jnp.concatenate([a_ref[0, g] for g in range(N_KV_HEADS)] + [mixp_ref[...]],
                              axis=1)
        x1 = x_ref[...] + jnp.dot(mix, wout_ref[...], preferred_element_type=F32)
        ms = jnp.mean(x1 * x1, axis=-1, keepdims=True)
        x1_ref[...] = x1
        return (x1 * lax.rsqrt(ms + EPS) * g2_ref[...]).astype(BF16)

    bounds = [(c0, min(c0 + FF_CHUNK, d_ff)) for c0 in range(0, d_ff, FF_CHUNK)]
    assert len(bounds) > N_POOL_GROUPS + 1

    @pl.when(step == 0)
    def _():
        stage_windows()
        for g in range(N_POOL_GROUPS):
            pool_group(g)
        h_ref[...] = project_out()

    @pl.when(step > 0)
    def _():
        stage_windows()
        h = h_ref[...]
        acc = x1_ref[...]
        h_next = None
        for c, (c0, c1) in enumerate(bounds):
            gate = jnp.dot(h, wg_ref[:, c0:c1], preferred_element_type=F32)
            up = jnp.dot(h, wu_ref[:, c0:c1], preferred_element_type=F32)
            if c == len(bounds) - 1:
                h_ref[...] = h_next
            act = (gate * jax.nn.sigmoid(gate) * up).astype(BF16)
            acc = acc + jnp.dot(act, wd_ref[c0:c1, :], preferred_element_type=F32)
            if c < N_POOL_GROUPS:
                pool_group(c)
            elif c == N_POOL_GROUPS:
                h_next = project_out()
        o_ref[...] = acc


def _const_spec(shape):
    nd = len(shape)
    return pl.BlockSpec(shape, lambda *_: (0,) * nd)


def _layer(x2, seq, norm1_g, w_in, q_norm_g, k_norm_g, pool_w, pool_b, pool_scale,
           w_out, norm2_g, w_gate, w_up, w_down):
    n_tok = x2.shape[0]
    batch = n_tok // seq
    d_ff = w_gate.shape[1]
    assert seq % ROW_TILE == 0 and seq % Q_TILE == 0 and ROW_TILE % POOL_HALO == 0
    tiles_per_seq = seq // ROW_TILE
    n_row_tiles = n_tok // ROW_TILE

    cos, sin = _rope_tables(seq)
    seg = jnp.asarray(_segment_ones(256), BF16)
    qg = jnp.tile(q_norm_g.astype(F32), LANES // HEAD_DIM)[None, :]
    kg = jnp.tile(k_norm_g.astype(F32), LANES // HEAD_DIM)[None, :]
    q_scale = HEAD_DIM ** -0.5 * math.log2(math.e)

    params = pltpu.CompilerParams(dimension_semantics=("arbitrary",),
                                  vmem_limit_bytes=VMEM_LIMIT_BYTES)

    def grouped_rows(width, tile=lambda i: i):
        return pl.BlockSpec((1, N_KV_HEADS, ROW_TILE, width),
                            lambda i: (tile(i) // tiles_per_seq, 0, tile(i) % tiles_per_seq, 0))

    def mix_tile(i):
        return jnp.minimum(i, n_row_tiles - 1)

    def grouped_seq(width):
        return pl.BlockSpec((1, N_KV_HEADS, seq, width), lambda b: (b, 0, 0, 0))

    q, krep, vext, u = pl.pallas_call(
        functools.partial(_inproj_kernel, q_scale=q_scale),
        grid=(n_row_tiles,),
        in_specs=[
            pl.BlockSpec((ROW_TILE, D_MODEL), lambda i: (i, 0)),
            _const_spec((1, D_MODEL)),
            _const_spec((D_MODEL, IN_WIDTH)),
            _const_spec((1, LANES)),
            _const_spec((1, LANES)),
            pl.BlockSpec((ROW_TILE, LANES), lambda i: (i % tiles_per_seq, 0)),
            pl.BlockSpec((ROW_TILE, LANES), lambda i: (i % tiles_per_seq, 0)),
            _const_spec((256, 256)),
        ],
        out_specs=[
            grouped_rows(GROUP_WIDTH),
            grouped_rows(GROUP_WIDTH),
            grouped_rows(LANES),
            pl.BlockSpec((ROW_TILE, POOL_WIDTH), lambda i: (i, 0)),
        ],
        out_shape=[
            jax.ShapeDtypeStruct((batch, N_KV_HEADS, seq, GROUP_WIDTH), BF16),
            jax.ShapeDtypeStruct((batch, N_KV_HEADS, seq, GROUP_WIDTH), BF16),
            jax.ShapeDtypeStruct((batch, N_KV_HEADS, seq, LANES), BF16),
            jax.ShapeDtypeStruct((n_tok, POOL_WIDTH), F32),
        ],
        compiler_params=params,
        name="inproj",
    )(x2, norm1_g.astype(F32)[None, :], w_in.astype(BF16), qg, kg,
      jnp.asarray(cos), jnp.asarray(sin), seg)

    attn = pl.pallas_call(
        functools.partial(_attention_kernel, seq=seq),
        grid=(batch,),
        in_specs=[grouped_seq(GROUP_WIDTH), grouped_seq(GROUP_WIDTH), grouped_seq(LANES)],
        out_specs=grouped_seq(GROUP_WIDTH),
        out_shape=jax.ShapeDtypeStruct((batch, N_KV_HEADS, seq, GROUP_WIDTH), BF16),
        scratch_shapes=[pltpu.VMEM((Q_PER_KV * Q_TILE, seq), F32),
                        pltpu.VMEM((Q_PER_KV * Q_TILE, seq), F32),
                        pltpu.VMEM((Q_PER_KV * Q_TILE, seq), BF16),
                        pltpu.VMEM((Q_PER_KV * Q_TILE, seq), BF16)],
        compiler_params=params,
        name="attention",
    )(q, krep, vext)

    halo_blocks = ROW_TILE // POOL_HALO
    last_halo = n_tok // POOL_HALO - 1
    out = pl.pallas_call(
        functools.partial(_mix_ffn_kernel, seq=seq, d_ff=d_ff, n_tiles=n_row_tiles),
        grid=(n_row_tiles + 1,),
        in_specs=[
            pl.BlockSpec((ROW_TILE, D_MODEL), lambda i: (mix_tile(i), 0)),
            grouped_rows(GROUP_WIDTH, mix_tile),
            pl.BlockSpec((ROW_TILE, POOL_WIDTH), lambda i: (mix_tile(i), 0)),
            pl.BlockSpec((POOL_HALO, POOL_WIDTH),
                         lambda i: (jnp.maximum(mix_tile(i) * halo_blocks - 1, 0), 0)),
            pl.BlockSpec((POOL_HALO, POOL_WIDTH),
                         lambda i: (jnp.minimum((mix_tile(i) + 1) * halo_blocks, last_halo), 0)),
            _const_spec((N_POOL_GROUPS, POOL_GROUP_DIM, POOL_GROUP_DIM)),
            _const_spec((N_POOL_GROUPS, 1, POOL_GROUP_DIM)),
            _const_spec((1, POOL_WIDTH)),
            _const_spec((D_MODEL, D_MODEL)),
            _const_spec((1, D_MODEL)),
            _const_spec((D_MODEL, d_ff)),
            _const_spec((D_MODEL, d_ff)),
            _const_spec((d_ff, D_MODEL)),
        ],
        out_specs=pl.BlockSpec((ROW_TILE, D_MODEL), lambda i: (jnp.maximum(i - 1, 0), 0)),
        out_shape=jax.ShapeDtypeStruct((n_tok, D_MODEL), F32),
        scratch_shapes=[pltpu.VMEM((ROW_TILE + 2 * POOL_HALO, POOL_WIDTH), F32),
                        pltpu.VMEM((ROW_TILE, POOL_WIDTH), BF16),
                        pltpu.VMEM((ROW_TILE, D_MODEL), F32), pltpu.VMEM((ROW_TILE, D_MODEL), BF16)],
        compiler_params=params,
        name="mix_ffn",
    )(x2, attn, u, u, u, pool_w.astype(BF16), pool_b.astype(F32)[:, None, :],
      pool_scale.astype(F32)[None, :], w_out.astype(BF16), norm2_g.astype(F32)[None, :],
      w_gate.astype(BF16), w_up.astype(BF16), w_down.astype(BF16))
    return out


def kernel(x, norm1_g, w_in, q_norm_g, k_norm_g, pool_w, pool_b, pool_scale,
           w_out, norm2_g, w_gate, w_up, w_down):
    batch, seq, d_model = x.shape
    assert d_model == D_MODEL and w_in.shape[-1] == IN_WIDTH
    x2 = x.reshape(batch * seq, d_model)
    for l in range(norm1_g.shape[0]):
        x2 = _layer(x2, seq, norm1_g[l], w_in[l], q_norm_g[l], k_norm_g[l], pool_w[l],
                    pool_b[l], pool_scale[l], w_out[l], norm2_g[l], w_gate[l], w_up[l],
                    w_down[l])
    return x2.reshape(batch, seq, d_model)
```

```python
import functools
import math

import jax
import jax.numpy as jnp
import numpy as np
from jax import lax
from jax.experimental import pallas as pl
from jax.experimental.pallas import tpu as pltpu

D_MODEL = 1024
HEAD_DIM = 64
N_Q_HEADS = 8
N_KV_HEADS = 2
Q_PER_KV = N_Q_HEADS // N_KV_HEADS
ATTN_WIDTH = N_Q_HEADS * HEAD_DIM
KV_WIDTH = N_KV_HEADS * HEAD_DIM
GROUP_WIDTH = Q_PER_KV * HEAD_DIM
ROPE_PAIRS = HEAD_DIM // 4
ROPE_THETA = 10000.0
POOL_WINDOWS = (2, 4, 8, 16)
N_POOL_GROUPS = len(POOL_WINDOWS)
POOL_WIDTH = D_MODEL - ATTN_WIDTH
POOL_GROUP_DIM = POOL_WIDTH // N_POOL_GROUPS
IN_WIDTH = ATTN_WIDTH + 2 * KV_WIDTH + POOL_WIDTH
GRID_W = 64
EPS = 1e-6

LANES = 128
POOL_HALO = 8
VMEM_LIMIT_BYTES = 56 * 1024 * 1024

ROW_TILE = 512
Q_TILE = 128
ATTN_ROTATIONS_PER_STEP = 5
FF_CHUNK = 512

F32 = jnp.float32
BF16 = jnp.bfloat16


def _lane_iota(shape):
    return lax.broadcasted_iota(jnp.int32, shape, len(shape) - 1)


def _rope_tables(seq):
    t = np.arange(seq)
    row = (t // GRID_W).astype(np.float64)
    col = (t % GRID_W).astype(np.float64)
    inv_freq = ROPE_THETA ** (-np.arange(ROPE_PAIRS, dtype=np.float64) / ROPE_PAIRS)
    ang_row = row[:, None] * inv_freq[None, :]
    ang_col = col[:, None] * inv_freq[None, :]
    ang = np.concatenate([ang_row, ang_row, ang_col, ang_col], axis=1)
    sign = np.concatenate([-np.ones(ROPE_PAIRS), np.ones(ROPE_PAIRS)] * 2)
    cos = np.tile(np.cos(ang), (1, 2)).astype(np.float32)
    sin = np.tile(np.sin(ang) * sign[None, :], (1, 2)).astype(np.float32)
    return cos, sin


def _segment_ones(width):
    idx = np.arange(width) // HEAD_DIM
    return (idx[:, None] == idx[None, :]).astype(np.float32)


def _head_sumsq(x, seg):
    sq = x * x
    hi = sq.astype(BF16)
    lo = (sq - hi.astype(F32)).astype(BF16)
    return (jnp.dot(hi, seg, preferred_element_type=F32)
            + jnp.dot(lo, seg, preferred_element_type=F32))


def _rope_partner(x):
    lane = _lane_iota(x.shape)
    fwd = pltpu.roll(x, LANES - ROPE_PAIRS, axis=1)
    bwd = pltpu.roll(x, ROPE_PAIRS, axis=1)
    return jnp.where((lane & ROPE_PAIRS) == 0, fwd, bwd)


def _norm_rope(x, sumsq, gain, cos, sin):
    y = x * lax.rsqrt(sumsq * (1.0 / HEAD_DIM) + EPS) * gain
    return y * cos + _rope_partner(y) * sin


def _inproj_kernel(x_ref, g1_ref, win_ref, qg_ref, kg_ref, cos_ref, sin_ref, seg_ref,
                   q_ref, krep_ref, vext_ref, u_ref, *, q_scale):
    x = x_ref[...]
    ms = jnp.mean(x * x, axis=-1, keepdims=True)
    h = (x * lax.rsqrt(ms + EPS) * g1_ref[...]).astype(BF16)
    proj = jnp.dot(h, win_ref[...], preferred_element_type=F32)

    cos = cos_ref[...]
    sin = sin_ref[...]
    seg = seg_ref[...]
    qg = qg_ref[...]
    kg = kg_ref[...]

    for g in range(N_KV_HEADS):
        qh = proj[:, g * GROUP_WIDTH:(g + 1) * GROUP_WIDTH]
        ss = _head_sumsq(qh, seg)
        for j in range(GROUP_WIDTH // LANES):
            sl = slice(j * LANES, (j + 1) * LANES)
            qr = _norm_rope(qh[:, sl], ss[:, sl], qg, cos, sin) * q_scale
            q_ref[0, g, :, sl] = qr.astype(BF16)

    k = proj[:, ATTN_WIDTH:ATTN_WIDTH + KV_WIDTH]
    kss = _head_sumsq(k, seg[:KV_WIDTH, :KV_WIDTH])
    kr = _norm_rope(k, kss, kg, cos, sin)
    lane = _lane_iota(kr.shape)
    first = lane < HEAD_DIM
    kswap = pltpu.roll(kr, HEAD_DIM, axis=1)
    k00 = jnp.where(first, kr, kswap).astype(BF16)
    k11 = jnp.where(first, kswap, kr).astype(BF16)
    for j in range(GROUP_WIDTH // LANES):
        krep_ref[0, 0, :, j * LANES:(j + 1) * LANES] = k00
        krep_ref[0, 1, :, j * LANES:(j + 1) * LANES] = k11

    v = proj[:, ATTN_WIDTH + KV_WIDTH:ATTN_WIDTH + 2 * KV_WIDTH]
    vswap = pltpu.roll(v, HEAD_DIM, axis=1)
    vext_ref[0, 0] = jnp.where(first, v, 1.0).astype(BF16)
    vext_ref[0, 1] = jnp.where(first, vswap, 1.0).astype(BF16)

    u_ref[...] = proj[:, ATTN_WIDTH + 2 * KV_WIDTH:]


def _attention_kernel(q_ref, krep_ref, vext_ref, o_ref,
                      s0_ref, s1_ref, s2_ref, m0_ref, m1_ref, m2_ref, *, seq):
    tiles_per_group = seq // Q_TILE
    n_tiles = N_KV_HEADS * tiles_per_group
    lane_q = _lane_iota((Q_TILE, GROUP_WIDTH))
    lane_o = _lane_iota((Q_TILE, LANES))

    def tile_pos(t):
        g = t // tiles_per_group
        r0 = pl.multiple_of((t % tiles_per_group) * Q_TILE, Q_TILE)
        return g, r0

    def scores(t, s_ref, m_ref):
        g, r0 = tile_pos(t)
        qt = q_ref[0, g, pl.ds(r0, Q_TILE), :]
        zero = jnp.zeros_like(qt)
        qm = jnp.concatenate(
            [jnp.where((lane_q // HEAD_DIM) == hh, qt, zero) for hh in range(Q_PER_KV)], axis=0)
        s = lax.dot_general(qm, krep_ref[0, g], (((1,), (1,)), ((), ())),
                            preferred_element_type=F32)
        s_ref[...] = s
        m_ref[...] = jnp.max(s, axis=-1, keepdims=True)

    def output(t, s_ref, m_ref):
        g, r0 = tile_pos(t)
        p = jnp.exp2(s_ref[...] - m_ref[...]).astype(BF16)
        oe = jnp.dot(p, vext_ref[0, g], preferred_element_type=F32)
        heads = [oe[hh * Q_TILE:(hh + 1) * Q_TILE] for hh in range(Q_PER_KV)]
        low = lane_o < HEAD_DIM
        for j in range(Q_PER_KV // 2):
            even, odd = heads[2 * j], heads[2 * j + 1]
            even_sw = pltpu.roll(even, HEAD_DIM, axis=1)
            odd_sw = pltpu.roll(odd, HEAD_DIM, axis=1)
            pair = jnp.where(low, even, odd_sw) / jnp.where(low, even_sw, odd)
            o_ref[0, g, pl.ds(r0, Q_TILE), j * LANES:(j + 1) * LANES] = pair.astype(BF16)

    bufs = [(s0_ref, m0_ref), (s1_ref, m1_ref), (s2_ref, m2_ref)]
    lead = len(bufs) - 1
    n_steady = n_tiles - lead
    assert n_steady % (len(bufs) * ATTN_ROTATIONS_PER_STEP) == 0

    for t in range(lead):
        scores(t, *bufs[t])

    def body(jj, carry):
        for k in range(len(bufs) * ATTN_ROTATIONS_PER_STEP):
            t = jj * len(bufs) * ATTN_ROTATIONS_PER_STEP + k
            scores(t + lead, *bufs[(k + lead) % len(bufs)])
            output(t, *bufs[k % len(bufs)])
        return carry

    lax.fori_loop(0, n_steady // (len(bufs) * ATTN_ROTATIONS_PER_STEP), body, 0)

    for t in range(n_steady, n_tiles):
        output(t, *bufs[t % len(bufs)])


def _mix_ffn_kernel(x_ref, a_ref, u_ref, uprev_ref, unext_ref, pw_ref, pb_ref, ps_ref,
                    wout_ref, g2_ref, wg_ref, wu_ref, wd_ref, o_ref,
                    ext_ref, mixp_ref, x1_ref, h_ref, *, seq, d_ff, n_tiles):
    rows = x_ref.shape[0]
    tiles_per_seq = seq // rows
    step = pl.program_id(0)

    st = jnp.minimum(step, n_tiles - 1) % tiles_per_seq

    def stage_windows():
        ext_ref[0:POOL_HALO, :] = jnp.where(st > 0, uprev_ref[...], 0.0)
        ext_ref[POOL_HALO:POOL_HALO + rows, :] = u_ref[...]
        ext_ref[POOL_HALO + rows:, :] = jnp.where(st < tiles_per_seq - 1, unext_ref[...], 0.0)

    def pool_group(g):
        n_ext = rows + 2 * POOL_HALO
        t = st * rows + lax.broadcasted_iota(jnp.int32, (rows, LANES), 0)
        w = POOL_WINDOWS[g]
        cs = slice(g * POOL_GROUP_DIM, (g + 1) * POOL_GROUP_DIM)
        ext = ext_ref[:, cs]
        b = ext
        k = 1
        while k < w:
            b = b + pltpu.roll(b, n_ext - k, axis=0)
            k *= 2
        win = pltpu.roll(b, w // 2, axis=0)[POOL_HALO:POOL_HALO + rows]
        lo = jnp.maximum(t - w // 2, 0)
        hi = jnp.minimum(t - w // 2 + w, seq)
        cnt = (hi - lo).astype(F32)
        pooled = (win / cnt - ext[POOL_HALO:POOL_HALO + rows]).astype(BF16)
        y = jnp.dot(pooled, pw_ref[g], preferred_element_type=F32) + pb_ref[g]
        mixp_ref[:, cs] = (y * ps_ref[:, cs]).astype(BF16)

    def project_out():
        mix = jnp.concatenate([a_ref[0, g] for g in range(N_KV_HEADS)] + [mixp_ref[...]],
                              axis=1)
        x1 = x_ref[...] + jnp.dot(mix, wout_ref[...], preferred_element_type=F32)
        ms = jnp.mean(x1 * x1, axis=-1, keepdims=True)
        x1_ref[...] = x1
        return (x1 * lax.rsqrt(ms + EPS) * g2_ref[...]).astype(BF16)

    bounds = [(c0, min(c0 + FF_CHUNK, d_ff)) for c0 in range(0, d_ff, FF_CHUNK)]
    assert len(bounds) > N_POOL_GROUPS + 1

    @pl.when(step == 0)
    def _():
        stage_windows()
        for g in range(N_POOL_GROUPS):
            pool_group(g)
        h_ref[...] = project_out()

    @pl.when(step > 0)
    def _():
        stage_windows()
        h = h_ref[...]
        acc = x1_ref[...]
        h_next = None
        for c, (c0, c1) in enumerate(bounds):
            gate = jnp.dot(h, wg_ref[:, c0:c1], preferred_element_type=F32)
            up = jnp.dot(h, wu_ref[:, c0:c1], preferred_element_type=F32)
            if c == len(bounds) - 1:
                h_ref[...] = h_next
            act = (gate * jax.nn.sigmoid(gate) * up).astype(BF16)
            acc = acc + jnp.dot(act, wd_ref[c0:c1, :], preferred_element_type=F32)
            if c < N_POOL_GROUPS:
                pool_group(c)
            elif c == N_POOL_GROUPS:
                h_next = project_out()
        o_ref[...] = acc


def _const_spec(shape):
    nd = len(shape)
    return pl.BlockSpec(shape, lambda *_: (0,) * nd)


def _layer(x2, seq, norm1_g, w_in, q_norm_g, k_norm_g, pool_w, pool_b, pool_scale,
           w_out, norm2_g, w_gate, w_up, w_down):
    n_tok = x2.shape[0]
    batch = n_tok // seq
    d_ff = w_gate.shape[1]
    assert seq % ROW_TILE == 0 and seq % Q_TILE == 0 and ROW_TILE % POOL_HALO == 0
    tiles_per_seq = seq // ROW_TILE
    n_row_tiles = n_tok // ROW_TILE

    cos, sin = _rope_tables(seq)
    seg = jnp.asarray(_segment_ones(256), BF16)
    qg = jnp.tile(q_norm_g.astype(F32), LANES // HEAD_DIM)[None, :]
    kg = jnp.tile(k_norm_g.astype(F32), LANES // HEAD_DIM)[None, :]
    q_scale = HEAD_DIM ** -0.5 * math.log2(math.e)

    params = pltpu.CompilerParams(dimension_semantics=("arbitrary",),
                                  vmem_limit_bytes=VMEM_LIMIT_BYTES)

    def grouped_rows(width, tile=lambda i: i):
        return pl.BlockSpec((1, N_KV_HEADS, ROW_TILE, width),
                            lambda i: (tile(i) // tiles_per_seq, 0, tile(i) % tiles_per_seq, 0))

    def mix_tile(i):
        return jnp.minimum(i, n_row_tiles - 1)

    def grouped_seq(width):
        return pl.BlockSpec((1, N_KV_HEADS, seq, width), lambda b: (b, 0, 0, 0))

    q, krep, vext, u = pl.pallas_call(
        functools.partial(_inproj_kernel, q_scale=q_scale),
        grid=(n_row_tiles,),
        in_specs=[
            pl.BlockSpec((ROW_TILE, D_MODEL), lambda i: (i, 0)),
            _const_spec((1, D_MODEL)),
            _const_spec((D_MODEL, IN_WIDTH)),
            _const_spec((1, LANES)),
            _const_spec((1, LANES)),
            pl.BlockSpec((ROW_TILE, LANES), lambda i: (i % tiles_per_seq, 0)),
            pl.BlockSpec((ROW_TILE, LANES), lambda i: (i % tiles_per_seq, 0)),
            _const_spec((256, 256)),
        ],
        out_specs=[
            grouped_rows(GROUP_WIDTH),
            grouped_rows(GROUP_WIDTH),
            grouped_rows(LANES),
            pl.BlockSpec((ROW_TILE, POOL_WIDTH), lambda i: (i, 0)),
        ],
        out_shape=[
            jax.ShapeDtypeStruct((batch, N_KV_HEADS, seq, GROUP_WIDTH), BF16),
            jax.ShapeDtypeStruct((batch, N_KV_HEADS, seq, GROUP_WIDTH), BF16),
            jax.ShapeDtypeStruct((batch, N_KV_HEADS, seq, LANES), BF16),
            jax.ShapeDtypeStruct((n_tok, POOL_WIDTH), F32),
        ],
        compiler_params=params,
        name="inproj",
    )(x2, norm1_g.astype(F32)[None, :], w_in.astype(BF16), qg, kg,
      jnp.asarray(cos), jnp.asarray(sin), seg)

    attn = pl.pallas_call(
        functools.partial(_attention_kernel, seq=seq),
        grid=(batch,),
        in_specs=[grouped_seq(GROUP_WIDTH), grouped_seq(GROUP_WIDTH), grouped_seq(LANES)],
        out_specs=grouped_seq(GROUP_WIDTH),
        out_shape=jax.ShapeDtypeStruct((batch, N_KV_HEADS, seq, GROUP_WIDTH), BF16),
        scratch_shapes=([pltpu.VMEM((Q_PER_KV * Q_TILE, seq), F32)] * 3
                        + [pltpu.VMEM((Q_PER_KV * Q_TILE, 1), F32)] * 3),
        compiler_params=params,
        name="attention",
    )(q, krep, vext)

    halo_blocks = ROW_TILE // POOL_HALO
    last_halo = n_tok // POOL_HALO - 1
    out = pl.pallas_call(
        functools.partial(_mix_ffn_kernel, seq=seq, d_ff=d_ff, n_tiles=n_row_tiles),
        grid=(n_row_tiles + 1,),
        in_specs=[
            pl.BlockSpec((ROW_TILE, D_MODEL), lambda i: (mix_tile(i), 0)),
            grouped_rows(GROUP_WIDTH, mix_tile),
            pl.BlockSpec((ROW_TILE, POOL_WIDTH), lambda i: (mix_tile(i), 0)),
            pl.BlockSpec((POOL_HALO, POOL_WIDTH),
                         lambda i: (jnp.maximum(mix_tile(i) * halo_blocks - 1, 0), 0)),
            pl.BlockSpec((POOL_HALO, POOL_WIDTH),
                         lambda i: (jnp.minimum((mix_tile(i) + 1) * halo_blocks, last_halo), 0)),
            _const_spec((N_POOL_GROUPS, POOL_GROUP_DIM, POOL_GROUP_DIM)),
            _const_spec((N_POOL_GROUPS, 1, POOL_GROUP_DIM)),
            _const_spec((1, POOL_WIDTH)),
            _const_spec((D_MODEL, D_MODEL)),
            _const_spec((1, D_MODEL)),
            _const_spec((D_MODEL, d_ff)),
            _const_spec((D_MODEL, d_ff)),
            _const_spec((d_ff, D_MODEL)),
        ],
        out_specs=pl.BlockSpec((ROW_TILE, D_MODEL), lambda i: (jnp.maximum(i - 1, 0), 0)),
        out_shape=jax.ShapeDtypeStruct((n_tok, D_MODEL), F32),
        scratch_shapes=[pltpu.VMEM((ROW_TILE + 2 * POOL_HALO, POOL_WIDTH), F32),
                        pltpu.VMEM((ROW_TILE, POOL_WIDTH), BF16),
                        pltpu.VMEM((ROW_TILE, D_MODEL), F32), pltpu.VMEM((ROW_TILE, D_MODEL), BF16)],
        compiler_params=params,
        name="mix_ffn",
    )(x2, attn, u, u, u, pool_w.astype(BF16), pool_b.astype(F32)[:, None, :],
      pool_scale.astype(F32)[None, :], w_out.astype(BF16), norm2_g.astype(F32)[None, :],
      w_gate.astype(BF16), w_up.astype(BF16), w_down.astype(BF16))
    return out


def kernel(x, norm1_g, w_in, q_norm_g, k_norm_g, pool_w, pool_b, pool_scale,
           w_out, norm2_g, w_gate, w_up, w_down):
    batch, seq, d_model = x.shape
    assert d_model == D_MODEL and w_in.shape[-1] == IN_WIDTH
    x2 = x.reshape(batch * seq, d_model)
    for l in range(norm1_g.shape[0]):
        x2 = _layer(x2, seq, norm1_g[l], w_in[l], q_norm_g[l], k_norm_g[l], pool_w[l],
                    pool_b[l], pool_scale[l], w_out[l], norm2_g[l], w_gate[l], w_up[l],
                    w_down[l])
    return x2.reshape(batch, seq, d_model)
```

```python
import functools
import math

import jax
import jax.numpy as jnp
import numpy as np
from jax import lax
from jax.experimental import pallas as pl
from jax.experimental.pallas import tpu as pltpu

D_MODEL = 1024
HEAD_DIM = 64
N_Q_HEADS = 8
N_KV_HEADS = 2
Q_PER_KV = N_Q_HEADS // N_KV_HEADS
ATTN_WIDTH = N_Q_HEADS * HEAD_DIM
KV_WIDTH = N_KV_HEADS * HEAD_DIM
GROUP_WIDTH = Q_PER_KV * HEAD_DIM
ROPE_PAIRS = HEAD_DIM // 4
ROPE_THETA = 10000.0
POOL_WINDOWS = (2, 4, 8, 16)
N_POOL_GROUPS = len(POOL_WINDOWS)
POOL_WIDTH = D_MODEL - ATTN_WIDTH
POOL_GROUP_DIM = POOL_WIDTH // N_POOL_GROUPS
IN_WIDTH = ATTN_WIDTH + 2 * KV_WIDTH + POOL_WIDTH
GRID_W = 64
EPS = 1e-6

LANES = 128
POOL_HALO = 8
VMEM_LIMIT_BYTES = 56 * 1024 * 1024

ROW_TILE = 512
Q_TILE = 128
ATTN_ROTATIONS_PER_STEP = 5
FF_CHUNK = 512
INPROJ_CHUNK = 256

F32 = jnp.float32
BF16 = jnp.bfloat16


def _lane_iota(shape):
    return lax.broadcasted_iota(jnp.int32, shape, len(shape) - 1)


def _rope_tables(seq):
    t = np.arange(seq)
    row = (t // GRID_W).astype(np.float64)
    col = (t % GRID_W).astype(np.float64)
    inv_freq = ROPE_THETA ** (-np.arange(ROPE_PAIRS, dtype=np.float64) / ROPE_PAIRS)
    ang_row = row[:, None] * inv_freq[None, :]
    ang_col = col[:, None] * inv_freq[None, :]
    ang = np.concatenate([ang_row, ang_row, ang_col, ang_col], axis=1)
    sign = np.concatenate([-np.ones(ROPE_PAIRS), np.ones(ROPE_PAIRS)] * 2)
    cos = np.tile(np.cos(ang), (1, 2)).astype(np.float32)
    sin = np.tile(np.sin(ang) * sign[None, :], (1, 2)).astype(np.float32)
    return cos, sin


def _segment_ones(width):
    idx = np.arange(width) // HEAD_DIM
    return (idx[:, None] == idx[None, :]).astype(np.float32)


def _head_sumsq(x, seg):
    sq = x * x
    hi = sq.astype(BF16)
    lo = (sq - hi.astype(F32)).astype(BF16)
    return (jnp.dot(hi, seg, preferred_element_type=F32)
            + jnp.dot(lo, seg, preferred_element_type=F32))


def _rope_partner(x):
    lane = _lane_iota(x.shape)
    fwd = pltpu.roll(x, LANES - ROPE_PAIRS, axis=1)
    bwd = pltpu.roll(x, ROPE_PAIRS, axis=1)
    return jnp.where((lane & ROPE_PAIRS) == 0, fwd, bwd)


def _norm_rope(x, sumsq, gain, cos, sin):
    y = x * lax.rsqrt(sumsq * (1.0 / HEAD_DIM) + EPS) * gain
    return y * cos + _rope_partner(y) * sin


def _inproj_kernel(x_ref, g1_ref, win_ref, qg_ref, kg_ref, cos_ref, sin_ref, seg_ref,
                   q_ref, krep_ref, vext_ref, u_ref, proja_ref, projb_ref, *, q_scale, n_tiles):
    step = pl.program_id(0)
    kv0 = ATTN_WIDTH
    u0 = ATTN_WIDTH + 2 * KV_WIDTH

    def normed_input():
        x = x_ref[...]
        ms = jnp.mean(x * x, axis=-1, keepdims=True)
        return (x * lax.rsqrt(ms + EPS) * g1_ref[...]).astype(BF16)

    def finish_q(src_ref, g):
        qh = src_ref[:, g * GROUP_WIDTH:(g + 1) * GROUP_WIDTH]
        ss = _head_sumsq(qh, seg_ref[...])
        for j in range(GROUP_WIDTH // LANES):
            sl = slice(j * LANES, (j + 1) * LANES)
            qr = _norm_rope(qh[:, sl], ss[:, sl], qg_ref[...], cos_ref[...], sin_ref[...]) * q_scale
            q_ref[0, g, :, sl] = qr.astype(BF16)

    def finish_kv(src_ref):
        k = src_ref[:, kv0:kv0 + KV_WIDTH]
        kss = _head_sumsq(k, seg_ref[:KV_WIDTH, :KV_WIDTH])
        kr = _norm_rope(k, kss, kg_ref[...], cos_ref[...], sin_ref[...])
        first = _lane_iota(kr.shape) < HEAD_DIM
        kswap = pltpu.roll(kr, HEAD_DIM, axis=1)
        k00 = jnp.where(first, kr, kswap).astype(BF16)
        k11 = jnp.where(first, kswap, kr).astype(BF16)
        for j in range(GROUP_WIDTH // LANES):
            krep_ref[0, 0, :, j * LANES:(j + 1) * LANES] = k00
            krep_ref[0, 1, :, j * LANES:(j + 1) * LANES] = k11
        v = src_ref[:, kv0 + KV_WIDTH:kv0 + 2 * KV_WIDTH]
        vswap = pltpu.roll(v, HEAD_DIM, axis=1)
        vext_ref[0, 0] = jnp.where(first, v, 1.0).astype(BF16)
        vext_ref[0, 1] = jnp.where(first, vswap, 1.0).astype(BF16)

    def finish_u(src_ref, half):
        cs = slice(half * (POOL_WIDTH // 2), (half + 1) * (POOL_WIDTH // 2))
        u_ref[:, cs] = src_ref[:, u0 + cs.start:u0 + cs.stop]

    def finish_pieces(src_ref):
        return [functools.partial(finish_q, src_ref, 0), functools.partial(finish_q, src_ref, 1),
                functools.partial(finish_kv, src_ref),
                functools.partial(finish_u, src_ref, 0), functools.partial(finish_u, src_ref, 1)]

    def project(dst_ref, between=()):
        h = normed_input()
        for c in range(IN_WIDTH // INPROJ_CHUNK):
            cs = slice(c * INPROJ_CHUNK, (c + 1) * INPROJ_CHUNK)
            dst_ref[:, cs] = jnp.dot(h, win_ref[:, cs], preferred_element_type=F32)
            if c < len(between):
                between[c]()

    even = step % 2 == 0
    assert n_tiles % 2 == 0

    @pl.when(step == 0)
    def _():
        project(proja_ref)

    @pl.when(jnp.logical_and(even, jnp.logical_and(step > 0, step < n_tiles)))
    def _():
        project(proja_ref, finish_pieces(projb_ref))

    @pl.when(jnp.logical_not(even))
    def _():
        project(projb_ref, finish_pieces(proja_ref))

    @pl.when(step == n_tiles)
    def _():
        for piece in finish_pieces(projb_ref):
            piece()


def _attention_kernel(q_ref, krep_ref, vext_ref, o_ref,
                      s0_ref, s1_ref, s2_ref, m0_ref, m1_ref, m2_ref, *, seq):
    tiles_per_group = seq // Q_TILE
    n_tiles = N_KV_HEADS * tiles_per_group
    lane_q = _lane_iota((Q_TILE, GROUP_WIDTH))
    lane_o = _lane_iota((Q_TILE, LANES))

    def tile_pos(t):
        g = t // tiles_per_group
        r0 = pl.multiple_of((t % tiles_per_group) * Q_TILE, Q_TILE)
        return g, r0

    def scores(t, s_ref, m_ref):
        g, r0 = tile_pos(t)
        qt = q_ref[0, g, pl.ds(r0, Q_TILE), :]
        zero = jnp.zeros_like(qt)
        qm = jnp.concatenate(
            [jnp.where((lane_q // HEAD_DIM) == hh, qt, zero) for hh in range(Q_PER_KV)], axis=0)
        s = lax.dot_general(qm, krep_ref[0, g], (((1,), (1,)), ((), ())),
                            preferred_element_type=F32)
        s_ref[...] = s
        m_ref[...] = jnp.max(s, axis=-1, keepdims=True)

    def output(t, s_ref, m_ref):
        g, r0 = tile_pos(t)
        p = jnp.exp2(s_ref[...] - m_ref[...]).astype(BF16)
        oe = jnp.dot(p, vext_ref[0, g], preferred_element_type=F32)
        heads = [oe[hh * Q_TILE:(hh + 1) * Q_TILE] for hh in range(Q_PER_KV)]
        low = lane_o < HEAD_DIM
        for j in range(Q_PER_KV // 2):
            even, odd = heads[2 * j], heads[2 * j + 1]
            even_sw = pltpu.roll(even, HEAD_DIM, axis=1)
            odd_sw = pltpu.roll(odd, HEAD_DIM, axis=1)
            pair = jnp.where(low, even, odd_sw) / jnp.where(low, even_sw, odd)
            o_ref[0, g, pl.ds(r0, Q_TILE), j * LANES:(j + 1) * LANES] = pair.astype(BF16)

    bufs = [(s0_ref, m0_ref), (s1_ref, m1_ref), (s2_ref, m2_ref)]
    lead = len(bufs) - 1
    n_steady = n_tiles - lead
    assert n_steady % (len(bufs) * ATTN_ROTATIONS_PER_STEP) == 0

    for t in range(lead):
        scores(t, *bufs[t])

    def body(jj, carry):
        for k in range(len(bufs) * ATTN_ROTATIONS_PER_STEP):
            t = jj * len(bufs) * ATTN_ROTATIONS_PER_STEP + k
            scores(t + lead, *bufs[(k + lead) % len(bufs)])
            output(t, *bufs[k % len(bufs)])
        return carry

    lax.fori_loop(0, n_steady // (len(bufs) * ATTN_ROTATIONS_PER_STEP), body, 0)

    for t in range(n_steady, n_tiles):
        output(t, *bufs[t % len(bufs)])


def _mix_ffn_kernel(x_ref, a_ref, u_ref, uprev_ref, unext_ref, pw_ref, pb_ref, ps_ref,
                    wout_ref, g2_ref, wg_ref, wu_ref, wd_ref, o_ref,
                    ext_ref, mixp_ref, x1_ref, h_ref, *, seq, d_ff, n_tiles):
    rows = x_ref.shape[0]
    tiles_per_seq = seq // rows
    step = pl.program_id(0)

    st = jnp.minimum(step, n_tiles - 1) % tiles_per_seq

    def stage_windows():
        ext_ref[0:POOL_HALO, :] = jnp.where(st > 0, uprev_ref[...], 0.0)
        ext_ref[POOL_HALO:POOL_HALO + rows, :] = u_ref[...]
        ext_ref[POOL_HALO + rows:, :] = jnp.where(st < tiles_per_seq - 1, unext_ref[...], 0.0)

    def pool_group(g):
        n_ext = rows + 2 * POOL_HALO
        t = st * rows + lax.broadcasted_iota(jnp.int32, (rows, LANES), 0)
        w = POOL_WINDOWS[g]
        cs = slice(g * POOL_GROUP_DIM, (g + 1) * POOL_GROUP_DIM)
        ext = ext_ref[:, cs]
        b = ext
        k = 1
        while k < w:
            b = b + pltpu.roll(b, n_ext - k, axis=0)
            k *= 2
        win = pltpu.roll(b, w // 2, axis=0)[POOL_HALO:POOL_HALO + rows]
        lo = jnp.maximum(t - w // 2, 0)
        hi = jnp.minimum(t - w // 2 + w, seq)
        cnt = (hi - lo).astype(F32)
        pooled = (win / cnt - ext[POOL_HALO:POOL_HALO + rows]).astype(BF16)
        y = jnp.dot(pooled, pw_ref[g], preferred_element_type=F32) + pb_ref[g]
        mixp_ref[:, cs] = (y * ps_ref[:, cs]).astype(BF16)

    def project_out():
        mix = jnp.concatenate([a_ref[0, g] for g in range(N_KV_HEADS)] + [mixp_ref[...]],
                              axis=1)
        x1 = x_ref[...] + jnp.dot(mix, wout_ref[...], preferred_element_type=F32)
        ms = jnp.mean(x1 * x1, axis=-1, keepdims=True)
        x1_ref[...] = x1
        return (x1 * lax.rsqrt(ms + EPS) * g2_ref[...]).astype(BF16)

    bounds = [(c0, min(c0 + FF_CHUNK, d_ff)) for c0 in range(0, d_ff, FF_CHUNK)]
    assert len(bounds) > N_POOL_GROUPS + 1

    @pl.when(step == 0)
    def _():
        stage_windows()
        for g in range(N_POOL_GROUPS):
            pool_group(g)
        h_ref[...] = project_out()

    @pl.when(step > 0)
    def _():
        stage_windows()
        h = h_ref[...]
        acc = x1_ref[...]
        h_next = None
        for c, (c0, c1) in enumerate(bounds):
            gate = jnp.dot(h, wg_ref[:, c0:c1], preferred_element_type=F32)
            up = jnp.dot(h, wu_ref[:, c0:c1], preferred_element_type=F32)
            if c == len(bounds) - 1:
                h_ref[...] = h_next
            act = (gate * jax.nn.sigmoid(gate) * up).astype(BF16)
            acc = acc + jnp.dot(act, wd_ref[c0:c1, :], preferred_element_type=F32)
            if c < N_POOL_GROUPS:
                pool_group(c)
            elif c == N_POOL_GROUPS:
                h_next = project_out()
        o_ref[...] = acc


def _const_spec(shape):
    nd = len(shape)
    return pl.BlockSpec(shape, lambda *_: (0,) * nd)


def _layer(x2, seq, norm1_g, w_in, q_norm_g, k_norm_g, pool_w, pool_b, pool_scale,
           w_out, norm2_g, w_gate, w_up, w_down):
    n_tok = x2.shape[0]
    batch = n_tok // seq
    d_ff = w_gate.shape[1]
    assert seq % ROW_TILE == 0 and seq % Q_TILE == 0 and ROW_TILE % POOL_HALO == 0
    tiles_per_seq = seq // ROW_TILE
    n_row_tiles = n_tok // ROW_TILE

    cos, sin = _rope_tables(seq)
    seg = jnp.asarray(_segment_ones(256), BF16)
    qg = jnp.tile(q_norm_g.astype(F32), LANES // HEAD_DIM)[None, :]
    kg = jnp.tile(k_norm_g.astype(F32), LANES // HEAD_DIM)[None, :]
    q_scale = HEAD_DIM ** -0.5 * math.log2(math.e)

    params = pltpu.CompilerParams(dimension_semantics=("arbitrary",),
                                  vmem_limit_bytes=VMEM_LIMIT_BYTES)

    def grouped_rows(width, tile=lambda i: i):
        return pl.BlockSpec((1, N_KV_HEADS, ROW_TILE, width),
                            lambda i: (tile(i) // tiles_per_seq, 0, tile(i) % tiles_per_seq, 0))

    def mix_tile(i):
        return jnp.minimum(i, n_row_tiles - 1)

    def lag_tile(i):
        return jnp.maximum(i - 1, 0)

    def grouped_seq(width):
        return pl.BlockSpec((1, N_KV_HEADS, seq, width), lambda b: (b, 0, 0, 0))

    q, krep, vext, u = pl.pallas_call(
        functools.partial(_inproj_kernel, q_scale=q_scale, n_tiles=n_row_tiles),
        grid=(n_row_tiles + 1,),
        in_specs=[
            pl.BlockSpec((ROW_TILE, D_MODEL), lambda i: (mix_tile(i), 0)),
            _const_spec((1, D_MODEL)),
            _const_spec((D_MODEL, IN_WIDTH)),
            _const_spec((1, LANES)),
            _const_spec((1, LANES)),
            pl.BlockSpec((ROW_TILE, LANES), lambda i: (lag_tile(i) % tiles_per_seq, 0)),
            pl.BlockSpec((ROW_TILE, LANES), lambda i: (lag_tile(i) % tiles_per_seq, 0)),
            _const_spec((256, 256)),
        ],
        out_specs=[
            grouped_rows(GROUP_WIDTH, lag_tile),
            grouped_rows(GROUP_WIDTH, lag_tile),
            grouped_rows(LANES, lag_tile),
            pl.BlockSpec((ROW_TILE, POOL_WIDTH), lambda i: (lag_tile(i), 0)),
        ],
        out_shape=[
            jax.ShapeDtypeStruct((batch, N_KV_HEADS, seq, GROUP_WIDTH), BF16),
            jax.ShapeDtypeStruct((batch, N_KV_HEADS, seq, GROUP_WIDTH), BF16),
            jax.ShapeDtypeStruct((batch, N_KV_HEADS, seq, LANES), BF16),
            jax.ShapeDtypeStruct((n_tok, POOL_WIDTH), F32),
        ],
        scratch_shapes=[pltpu.VMEM((ROW_TILE, IN_WIDTH), F32), pltpu.VMEM((ROW_TILE, IN_WIDTH), F32)],
        compiler_params=params,
        name="inproj",
    )(x2, norm1_g.astype(F32)[None, :], w_in.astype(BF16), qg, kg,
      jnp.asarray(cos), jnp.asarray(sin), seg)

    attn = pl.pallas_call(
        functools.partial(_attention_kernel, seq=seq),
        grid=(batch,),
        in_specs=[grouped_seq(GROUP_WIDTH), grouped_seq(GROUP_WIDTH), grouped_seq(LANES)],
        out_specs=grouped_seq(GROUP_WIDTH),
        out_shape=jax.ShapeDtypeStruct((batch, N_KV_HEADS, seq, GROUP_WIDTH), BF16),
        scratch_shapes=([pltpu.VMEM((Q_PER_KV * Q_TILE, seq), F32)] * 3
                        + [pltpu.VMEM((Q_PER_KV * Q_TILE, 1), F32)] * 3),
        compiler_params=params,
        name="attention",
    )(q, krep, vext)

    halo_blocks = ROW_TILE // POOL_HALO
    last_halo = n_tok // POOL_HALO - 1
    out = pl.pallas_call(
        functools.partial(_mix_ffn_kernel, seq=seq, d_ff=d_ff, n_tiles=n_row_tiles),
        grid=(n_row_tiles + 1,),
        in_specs=[
            pl.BlockSpec((ROW_TILE, D_MODEL), lambda i: (mix_tile(i), 0)),
            grouped_rows(GROUP_WIDTH, mix_tile),
            pl.BlockSpec((ROW_TILE, POOL_WIDTH), lambda i: (mix_tile(i), 0)),
            pl.BlockSpec((POOL_HALO, POOL_WIDTH),
                         lambda i: (jnp.maximum(mix_tile(i) * halo_blocks - 1, 0), 0)),
            pl.BlockSpec((POOL_HALO, POOL_WIDTH),
                         lambda i: (jnp.minimum((mix_tile(i) + 1) * halo_blocks, last_halo), 0)),
            _const_spec((N_POOL_GROUPS, POOL_GROUP_DIM, POOL_GROUP_DIM)),
            _const_spec((N_POOL_GROUPS, 1, POOL_GROUP_DIM)),
            _const_spec((1, POOL_WIDTH)),
            _const_spec((D_MODEL, D_MODEL)),
            _const_spec((1, D_MODEL)),
            _const_spec((D_MODEL, d_ff)),
            _const_spec((D_MODEL, d_ff)),
            _const_spec((d_ff, D_MODEL)),
        ],
        out_specs=pl.BlockSpec((ROW_TILE, D_MODEL), lambda i: (jnp.maximum(i - 1, 0), 0)),
        out_shape=jax.ShapeDtypeStruct((n_tok, D_MODEL), F32),
        scratch_shapes=[pltpu.VMEM((ROW_TILE + 2 * POOL_HALO, POOL_WIDTH), F32),
                        pltpu.VMEM((ROW_TILE, POOL_WIDTH), BF16),
                        pltpu.VMEM((ROW_TILE, D_MODEL), F32), pltpu.VMEM((ROW_TILE, D_MODEL), BF16)],
        compiler_params=params,
        name="mix_ffn",
    )(x2, attn, u, u, u, pool_w.astype(BF16), pool_b.astype(F32)[:, None, :],
      pool_scale.astype(F32)[None, :], w_out.astype(BF16), norm2_g.astype(F32)[None, :],
      w_gate.astype(BF16), w_up.astype(BF16), w_down.astype(BF16))
    return out


def kernel(x, norm1_g, w_in, q_norm_g, k_norm_g, pool_w, pool_b, pool_scale,
           w_out, norm2_g, w_gate, w_up, w_down):
    batch, seq, d_model = x.shape
    assert d_model == D_MODEL and w_in.shape[-1] == IN_WIDTH
    x2 = x.reshape(batch * seq, d_model)
    for l in range(norm1_g.shape[0]):
        x2 = _layer(x2, seq, norm1_g[l], w_in[l], q_norm_g[l], k_norm_g[l], pool_w[l],
                    pool_b[l], pool_scale[l], w_out[l], norm2_g[l], w_gate[l], w_up[l],
                    w_down[l])
    return x2.reshape(batch, seq, d_model)
```

```python
import functools
import math

import jax
import jax.numpy as jnp
import numpy as np
from jax import lax
from jax.experimental import pallas as pl
from jax.experimental.pallas import tpu as pltpu

D_MODEL = 1024
HEAD_DIM = 64
N_Q_HEADS = 8
N_KV_HEADS = 2
Q_PER_KV = N_Q_HEADS // N_KV_HEADS
ATTN_WIDTH = N_Q_HEADS * HEAD_DIM
KV_WIDTH = N_KV_HEADS * HEAD_DIM
GROUP_WIDTH = Q_PER_KV * HEAD_DIM
ROPE_PAIRS = HEAD_DIM // 4
ROPE_THETA = 10000.0
POOL_WINDOWS = (2, 4, 8, 16)
N_POOL_GROUPS = len(POOL_WINDOWS)
POOL_WIDTH = D_MODEL - ATTN_WIDTH
POOL_GROUP_DIM = POOL_WIDTH // N_POOL_GROUPS
IN_WIDTH = ATTN_WIDTH + 2 * KV_WIDTH + POOL_WIDTH
GRID_W = 64
EPS = 1e-6

LANES = 128
BF16_SUBLANES = 16
POOL_HALO = 8
VMEM_LIMIT_BYTES = 56 * 1024 * 1024

ROW_TILE = 512
Q_TILE = 128
ATTN_ROTATIONS_PER_STEP = 5
FF_CHUNK = 512
INPROJ_CHUNK = 256

F32 = jnp.float32
BF16 = jnp.bfloat16


def _lane_iota(shape):
    return lax.broadcasted_iota(jnp.int32, shape, len(shape) - 1)


def _rope_tables(seq):
    t = np.arange(seq)
    row = (t // GRID_W).astype(np.float64)
    col = (t % GRID_W).astype(np.float64)
    inv_freq = ROPE_THETA ** (-np.arange(ROPE_PAIRS, dtype=np.float64) / ROPE_PAIRS)
    ang_row = row[:, None] * inv_freq[None, :]
    ang_col = col[:, None] * inv_freq[None, :]
    ang = np.concatenate([ang_row, ang_row, ang_col, ang_col], axis=1)
    sign = np.concatenate([-np.ones(ROPE_PAIRS), np.ones(ROPE_PAIRS)] * 2)
    cos = np.tile(np.cos(ang), (1, 2)).astype(np.float32)
    sin = np.tile(np.sin(ang) * sign[None, :], (1, 2)).astype(np.float32)
    return cos, sin


def _segment_ones(width):
    idx = np.arange(width) // HEAD_DIM
    return (idx[:, None] == idx[None, :]).astype(np.float32)


def _head_sumsq(x, seg):
    sq = x * x
    hi = sq.astype(BF16)
    lo = (sq - hi.astype(F32)).astype(BF16)
    return (jnp.dot(hi, seg, preferred_element_type=F32)
            + jnp.dot(lo, seg, preferred_element_type=F32))


def _rope_partner(x):
    lane = _lane_iota(x.shape)
    fwd = pltpu.roll(x, LANES - ROPE_PAIRS, axis=1)
    bwd = pltpu.roll(x, ROPE_PAIRS, axis=1)
    return jnp.where((lane & ROPE_PAIRS) == 0, fwd, bwd)


def _norm_rope(x, sumsq, gain, cos, sin):
    y = x * lax.rsqrt(sumsq * (1.0 / HEAD_DIM) + EPS) * gain
    return y * cos + _rope_partner(y) * sin


def _inproj_kernel(x_ref, g1_ref, win_ref, qg_ref, kg_ref, cos_ref, sin_ref, seg_ref,
                   q_ref, krep_ref, vext_ref, u_ref, proja_ref, projb_ref, wbf_ref,
                   *, q_scale, n_tiles, seq):
    step = pl.program_id(0)
    rows = x_ref.shape[0]
    kv0 = ATTN_WIDTH
    u0 = ATTN_WIDTH + 2 * KV_WIDTH
    pos0 = pl.multiple_of((jnp.maximum(step - 1, 0) % (seq // rows)) * rows, rows)

    def rope_tables():
        return cos_ref[pl.ds(pos0, rows), :], sin_ref[pl.ds(pos0, rows), :]

    def normed_input():
        x = x_ref[...]
        ms = jnp.mean(x * x, axis=-1, keepdims=True)
        return (x * lax.rsqrt(ms + EPS) * g1_ref[...]).astype(BF16)

    def finish_q(src_ref, g):
        qh = src_ref[:, g * GROUP_WIDTH:(g + 1) * GROUP_WIDTH]
        ss = _head_sumsq(qh, seg_ref[...])
        for j in range(GROUP_WIDTH // LANES):
            sl = slice(j * LANES, (j + 1) * LANES)
            qr = _norm_rope(qh[:, sl], ss[:, sl], qg_ref[...], *rope_tables()) * q_scale
            q_ref[0, g, :, sl] = qr.astype(BF16)

    def finish_kv(src_ref):
        k = src_ref[:, kv0:kv0 + KV_WIDTH]
        kss = _head_sumsq(k, seg_ref[:KV_WIDTH, :KV_WIDTH])
        kr = _norm_rope(k, kss, kg_ref[...], *rope_tables())
        first = _lane_iota(kr.shape) < HEAD_DIM
        kswap = pltpu.roll(kr, HEAD_DIM, axis=1)
        k00 = jnp.where(first, kr, kswap).astype(BF16)
        k11 = jnp.where(first, kswap, kr).astype(BF16)
        for j in range(GROUP_WIDTH // LANES):
            krep_ref[0, 0, :, j * LANES:(j + 1) * LANES] = k00
            krep_ref[0, 1, :, j * LANES:(j + 1) * LANES] = k11
        v = src_ref[:, kv0 + KV_WIDTH:kv0 + 2 * KV_WIDTH]
        vswap = pltpu.roll(v, HEAD_DIM, axis=1)
        vext_ref[0, 0] = jnp.where(first, v, 1.0).astype(BF16)
        vext_ref[0, 1] = jnp.where(first, vswap, 1.0).astype(BF16)

    def finish_u(src_ref, half):
        cs = slice(half * (POOL_WIDTH // 2), (half + 1) * (POOL_WIDTH // 2))
        u_ref[:, cs] = src_ref[:, u0 + cs.start:u0 + cs.stop]

    def finish_pieces(src_ref):
        return [functools.partial(finish_q, src_ref, 0), functools.partial(finish_q, src_ref, 1),
                functools.partial(finish_kv, src_ref),
                functools.partial(finish_u, src_ref, 0), functools.partial(finish_u, src_ref, 1)]

    def project(dst_ref, between=()):
        h = normed_input()
        for c in range(IN_WIDTH // INPROJ_CHUNK):
            cs = slice(c * INPROJ_CHUNK, (c + 1) * INPROJ_CHUNK)
            dst_ref[:, cs] = jnp.dot(h, wbf_ref[:, cs], preferred_element_type=F32)
            if c < len(between):
                between[c]()

    even = step % 2 == 0
    assert n_tiles % 2 == 0

    @pl.when(step == 0)
    def _():
        wbf_ref[...] = win_ref[...].astype(BF16)
        project(proja_ref)

    @pl.when(jnp.logical_and(even, jnp.logical_and(step > 0, step < n_tiles)))
    def _():
        project(proja_ref, finish_pieces(projb_ref))

    @pl.when(jnp.logical_not(even))
    def _():
        project(projb_ref, finish_pieces(proja_ref))

    @pl.when(step == n_tiles)
    def _():
        for piece in finish_pieces(projb_ref):
            piece()


def _attention_kernel(q_ref, krep_ref, vext_ref, wf0_ref, wf1_ref, wf2_ref, wf3_ref,
                      o_ref, wb0_ref, wb1_ref, wb2_ref, wb3_ref,
                      s0_ref, s1_ref, s2_ref, m0_ref, m1_ref, m2_ref, *, seq):
    for wf_ref, wb_ref in ((wf0_ref, wb0_ref), (wf1_ref, wb1_ref),
                           (wf2_ref, wb2_ref), (wf3_ref, wb3_ref)):
        wb_ref[...] = wf_ref[...].astype(BF16)

    tiles_per_group = seq // Q_TILE
    n_tiles = N_KV_HEADS * tiles_per_group
    lane_q = _lane_iota((Q_TILE, GROUP_WIDTH))
    lane_o = _lane_iota((Q_TILE, LANES))

    def tile_pos(t):
        g = t // tiles_per_group
        r0 = pl.multiple_of((t % tiles_per_group) * Q_TILE, Q_TILE)
        return g, r0

    def scores(t, s_ref, m_ref):
        g, r0 = tile_pos(t)
        qt = q_ref[0, g, pl.ds(r0, Q_TILE), :]
        zero = jnp.zeros_like(qt)
        qm = jnp.concatenate(
            [jnp.where((lane_q // HEAD_DIM) == hh, qt, zero) for hh in range(Q_PER_KV)], axis=0)
        s = lax.dot_general(qm, krep_ref[0, g], (((1,), (1,)), ((), ())),
                            preferred_element_type=F32)
        s_ref[...] = s
        m_ref[...] = jnp.max(s, axis=-1, keepdims=True)

    def output(t, s_ref, m_ref):
        g, r0 = tile_pos(t)
        p = jnp.exp2(s_ref[...] - m_ref[...]).astype(BF16)
        oe = jnp.dot(p, vext_ref[0, g], preferred_element_type=F32)
        heads = [oe[hh * Q_TILE:(hh + 1) * Q_TILE] for hh in range(Q_PER_KV)]
        low = lane_o < HEAD_DIM
        for j in range(Q_PER_KV // 2):
            even, odd = heads[2 * j], heads[2 * j + 1]
            even_sw = pltpu.roll(even, HEAD_DIM, axis=1)
            odd_sw = pltpu.roll(odd, HEAD_DIM, axis=1)
            pair = jnp.where(low, even, odd_sw) / jnp.where(low, even_sw, odd)
            o_ref[0, g, pl.ds(r0, Q_TILE), j * LANES:(j + 1) * LANES] = pair.astype(BF16)

    bufs = [(s0_ref, m0_ref), (s1_ref, m1_ref), (s2_ref, m2_ref)]
    lead = len(bufs) - 1
    n_steady = n_tiles - lead
    assert n_steady % (len(bufs) * ATTN_ROTATIONS_PER_STEP) == 0

    for t in range(lead):
        scores(t, *bufs[t])

    def body(jj, carry):
        for k in range(len(bufs) * ATTN_ROTATIONS_PER_STEP):
            t = jj * len(bufs) * ATTN_ROTATIONS_PER_STEP + k
            scores(t + lead, *bufs[(k + lead) % len(bufs)])
            output(t, *bufs[k % len(bufs)])
        return carry

    lax.fori_loop(0, n_steady // (len(bufs) * ATTN_ROTATIONS_PER_STEP), body, 0)

    for t in range(n_steady, n_tiles):
        output(t, *bufs[t % len(bufs)])


def _mix_ffn_kernel(x_ref, a_ref, u_ref, uprev_ref, unext_ref, pw_ref, pb_ref, ps_ref,
                    wout_ref, g2_ref, wg_ref, wu_ref, wd_ref, o_ref,
                    ext_ref, mixp_ref, x1_ref, h_ref, *, seq, d_ff, n_tiles):
    rows = x_ref.shape[0]
    tiles_per_seq = seq // rows
    step = pl.program_id(0)

    st = jnp.minimum(step, n_tiles - 1) % tiles_per_seq

    def stage_windows():
        ext_ref[0:POOL_HALO, :] = jnp.where(st > 0, uprev_ref[...], 0.0)
        ext_ref[POOL_HALO:POOL_HALO + rows, :] = u_ref[...]
        ext_ref[POOL_HALO + rows:, :] = jnp.where(st < tiles_per_seq - 1, unext_ref[...], 0.0)

    def pool_group(g):
        n_ext = rows + 2 * POOL_HALO
        t = st * rows + lax.broadcasted_iota(jnp.int32, (rows, LANES), 0)
        w = POOL_WINDOWS[g]
        cs = slice(g * POOL_GROUP_DIM, (g + 1) * POOL_GROUP_DIM)
        ext = ext_ref[:, cs]
        b = ext
        k = 1
        while k < w:
            b = b + pltpu.roll(b, n_ext - k, axis=0)
            k *= 2
        win = pltpu.roll(b, w // 2, axis=0)[POOL_HALO:POOL_HALO + rows]
        lo = jnp.maximum(t - w // 2, 0)
        hi = jnp.minimum(t - w // 2 + w, seq)
        cnt = (hi - lo).astype(F32)
        pooled = (win / cnt - ext[POOL_HALO:POOL_HALO + rows]).astype(BF16)
        y = jnp.dot(pooled, pw_ref[g], preferred_element_type=F32) + pb_ref[g]
        mixp_ref[:, cs] = (y * ps_ref[:, cs]).astype(BF16)

    def project_out():
        mix = jnp.concatenate([a_ref[0, g] for g in range(N_KV_HEADS)] + [mixp_ref[...]],
                              axis=1)
        x1 = x_ref[...] + jnp.dot(mix, wout_ref[...], preferred_element_type=F32)
        ms = jnp.mean(x1 * x1, axis=-1, keepdims=True)
        x1_ref[...] = x1
        return (x1 * lax.rsqrt(ms + EPS) * g2_ref[...]).astype(BF16)

    bounds = [(c0, min(c0 + FF_CHUNK, d_ff)) for c0 in range(0, d_ff, FF_CHUNK)]
    assert len(bounds) > N_POOL_GROUPS + 1

    @pl.when(step == 0)
    def _():
        stage_windows()
        for g in range(N_POOL_GROUPS):
            pool_group(g)
        h_ref[...] = project_out()

    @pl.when(step > 0)
    def _():
        stage_windows()
        h = h_ref[...]
        acc = x1_ref[...]
        h_next = None
        for c, (c0, c1) in enumerate(bounds):
            gate = jnp.dot(h, wg_ref[:, c0:c1], preferred_element_type=F32)
            up = jnp.dot(h, wu_ref[:, c0:c1], preferred_element_type=F32)
            if c == len(bounds) - 1:
                h_ref[...] = h_next
            act = (gate * jax.nn.sigmoid(gate) * up).astype(BF16)
            acc = acc + jnp.dot(act, wd_ref[c0:c1, :], preferred_element_type=F32)
            if c < N_POOL_GROUPS:
                pool_group(c)
            elif c == N_POOL_GROUPS:
                h_next = project_out()
        o_ref[...] = acc


def _const_spec(shape):
    nd = len(shape)
    return pl.BlockSpec(shape, lambda *_: (0,) * nd)


def _layer(x2, seq, norm1_g, w_in, q_norm_g, k_norm_g, pool_w, pool_b, pool_scale,
           w_out, norm2_g, w_gate, w_up, w_down):
    n_tok = x2.shape[0]
    batch = n_tok // seq
    d_ff = w_gate.shape[1]
    assert seq % ROW_TILE == 0 and seq % Q_TILE == 0 and ROW_TILE % POOL_HALO == 0
    tiles_per_seq = seq // ROW_TILE
    n_row_tiles = n_tok // ROW_TILE

    cos, sin = _rope_tables(seq)
    seg = jnp.asarray(_segment_ones(256), BF16)
    qg = jnp.tile(q_norm_g.astype(F32), LANES // HEAD_DIM)[None, :]
    kg = jnp.tile(k_norm_g.astype(F32), LANES // HEAD_DIM)[None, :]
    q_scale = HEAD_DIM ** -0.5 * math.log2(math.e)

    params = pltpu.CompilerParams(dimension_semantics=("arbitrary",),
                                  vmem_limit_bytes=VMEM_LIMIT_BYTES)

    def grouped_rows(width, tile=lambda i: i):
        return pl.BlockSpec((1, N_KV_HEADS, ROW_TILE, width),
                            lambda i: (tile(i) // tiles_per_seq, 0, tile(i) % tiles_per_seq, 0))

    def mix_tile(i):
        return jnp.minimum(i, n_row_tiles - 1)

    def lag_tile(i):
        return jnp.maximum(i - 1, 0)

    def grouped_seq(width):
        return pl.BlockSpec((1, N_KV_HEADS, seq, width), lambda b: (b, 0, 0, 0))

    q, krep, vext, u = pl.pallas_call(
        functools.partial(_inproj_kernel, q_scale=q_scale, n_tiles=n_row_tiles, seq=seq),
        grid=(n_row_tiles + 1,),
        in_specs=[
            pl.BlockSpec((ROW_TILE, D_MODEL), lambda i: (mix_tile(i), 0)),
            _const_spec((1, D_MODEL)),
            _const_spec((D_MODEL, IN_WIDTH)),
            _const_spec((1, LANES)),
            _const_spec((1, LANES)),
            _const_spec((seq, LANES)),
            _const_spec((seq, LANES)),
            _const_spec((256, 256)),
        ],
        out_specs=[
            grouped_rows(GROUP_WIDTH, lag_tile),
            grouped_rows(GROUP_WIDTH, lag_tile),
            grouped_rows(LANES, lag_tile),
            pl.BlockSpec((ROW_TILE, POOL_WIDTH), lambda i: (lag_tile(i), 0)),
        ],
        out_shape=[
            jax.ShapeDtypeStruct((batch, N_KV_HEADS, seq, GROUP_WIDTH), BF16),
            jax.ShapeDtypeStruct((batch, N_KV_HEADS, seq, GROUP_WIDTH), BF16),
            jax.ShapeDtypeStruct((batch, N_KV_HEADS, seq, LANES), BF16),
            jax.ShapeDtypeStruct((n_tok, POOL_WIDTH), F32),
        ],
        scratch_shapes=[pltpu.VMEM((ROW_TILE, IN_WIDTH), F32), pltpu.VMEM((ROW_TILE, IN_WIDTH), F32),
                        pltpu.VMEM((D_MODEL, IN_WIDTH), BF16)],
        compiler_params=params,
        name="inproj",
    )(x2, norm1_g.astype(F32)[None, :], w_in.astype(F32), qg, kg,
      jnp.asarray(cos), jnp.asarray(sin), seg)

    late_weights = [w_out, w_gate, w_up, w_down]
    for w in late_weights:
        assert w.shape[0] % (batch * BF16_SUBLANES) == 0

    def row_slab(w):
        return pl.BlockSpec((w.shape[0] // batch, w.shape[1]), lambda b: (b, 0))

    attn, w_out_b, w_gate_b, w_up_b, w_down_b = pl.pallas_call(
        functools.partial(_attention_kernel, seq=seq),
        grid=(batch,),
        in_specs=([grouped_seq(GROUP_WIDTH), grouped_seq(GROUP_WIDTH), grouped_seq(LANES)]
                  + [row_slab(w) for w in late_weights]),
        out_specs=[grouped_seq(GROUP_WIDTH)] + [row_slab(w) for w in late_weights],
        out_shape=([jax.ShapeDtypeStruct((batch, N_KV_HEADS, seq, GROUP_WIDTH), BF16)]
                   + [jax.ShapeDtypeStruct(w.shape, BF16) for w in late_weights]),
        scratch_shapes=([pltpu.VMEM((Q_PER_KV * Q_TILE, seq), F32)] * 3
                        + [pltpu.VMEM((Q_PER_KV * Q_TILE, 1), F32)] * 3),
        compiler_params=params,
        name="attention",
    )(q, krep, vext, *[w.astype(F32) for w in late_weights])

    halo_blocks = ROW_TILE // POOL_HALO
    last_halo = n_tok // POOL_HALO - 1
    out = pl.pallas_call(
        functools.partial(_mix_ffn_kernel, seq=seq, d_ff=d_ff, n_tiles=n_row_tiles),
        grid=(n_row_tiles + 1,),
        in_specs=[
            pl.BlockSpec((ROW_TILE, D_MODEL), lambda i: (mix_tile(i), 0)),
            grouped_rows(GROUP_WIDTH, mix_tile),
            pl.BlockSpec((ROW_TILE, POOL_WIDTH), lambda i: (mix_tile(i), 0)),
            pl.BlockSpec((POOL_HALO, POOL_WIDTH),
                         lambda i: (jnp.maximum(mix_tile(i) * halo_blocks - 1, 0), 0)),
            pl.BlockSpec((POOL_HALO, POOL_WIDTH),
                         lambda i: (jnp.minimum((mix_tile(i) + 1) * halo_blocks, last_halo), 0)),
            _const_spec((N_POOL_GROUPS, POOL_GROUP_DIM, POOL_GROUP_DIM)),
            _const_spec((N_POOL_GROUPS, 1, POOL_GROUP_DIM)),
            _const_spec((1, POOL_WIDTH)),
            _const_spec((D_MODEL, D_MODEL)),
            _const_spec((1, D_MODEL)),
            _const_spec((D_MODEL, d_ff)),
            _const_spec((D_MODEL, d_ff)),
            _const_spec((d_ff, D_MODEL)),
        ],
        out_specs=pl.BlockSpec((ROW_TILE, D_MODEL), lambda i: (jnp.maximum(i - 1, 0), 0)),
        out_shape=jax.ShapeDtypeStruct((n_tok, D_MODEL), F32),
        scratch_shapes=[pltpu.VMEM((ROW_TILE + 2 * POOL_HALO, POOL_WIDTH), F32),
                        pltpu.VMEM((ROW_TILE, POOL_WIDTH), BF16),
                        pltpu.VMEM((ROW_TILE, D_MODEL), F32), pltpu.VMEM((ROW_TILE, D_MODEL), BF16)],
        compiler_params=params,
        name="mix_ffn",
    )(x2, attn, u, u, u, pool_w.astype(BF16), pool_b.astype(F32)[:, None, :],
      pool_scale.astype(F32)[None, :], w_out_b, norm2_g.astype(F32)[None, :],
      w_gate_b, w_up_b, w_down_b)
    return out


def kernel(x, norm1_g, w_in, q_norm_g, k_norm_g, pool_w, pool_b, pool_scale,
           w_out, norm2_g, w_gate, w_up, w_down):
    batch, seq, d_model = x.shape
    assert d_model == D_MODEL and w_in.shape[-1] == IN_WIDTH
    x2 = x.reshape(batch * seq, d_model)
    for l in range(norm1_g.shape[0]):
        x2 = _layer(x2, seq, norm1_g[l], w_in[l], q_norm_g[l], k_norm_g[l], pool_w[l],
                    pool_b[l], pool_scale[l], w_out[l], norm2_g[l], w_gate[l], w_up[l],
                    w_down[l])
    return x2.reshape(batch, seq, d_model)
```

```python
import functools
import math

import jax
import jax.numpy as jnp
import numpy as np
from jax import lax
from jax.experimental import pallas as pl
from jax.experimental.pallas import tpu as pltpu

D_MODEL = 1024
HEAD_DIM = 64
N_Q_HEADS = 8
N_KV_HEADS = 2
Q_PER_KV = N_Q_HEADS // N_KV_HEADS
ATTN_WIDTH = N_Q_HEADS * HEAD_DIM
KV_WIDTH = N_KV_HEADS * HEAD_DIM
GROUP_WIDTH = Q_PER_KV * HEAD_DIM
ROPE_PAIRS = HEAD_DIM // 4
ROPE_THETA = 10000.0
POOL_WINDOWS = (2, 4, 8, 16)
N_POOL_GROUPS = len(POOL_WINDOWS)
POOL_WIDTH = D_MODEL - ATTN_WIDTH
POOL_GROUP_DIM = POOL_WIDTH // N_POOL_GROUPS
IN_WIDTH = ATTN_WIDTH + 2 * KV_WIDTH + POOL_WIDTH
GRID_W = 64
EPS = 1e-6

LANES = 128
BF16_SUBLANES = 16
POOL_HALO = 8
VMEM_LIMIT_BYTES = 56 * 1024 * 1024

ROW_TILE = 512
INPROJ_ROWS = 1024
Q_TILE = 128
ATTN_ROTATIONS_PER_STEP = 5
FF_CHUNK = 512
INPROJ_CHUNK = 256

F32 = jnp.float32
BF16 = jnp.bfloat16


def _lane_iota(shape):
    return lax.broadcasted_iota(jnp.int32, shape, len(shape) - 1)


def _rope_tables(seq):
    t = np.arange(seq)
    row = (t // GRID_W).astype(np.float64)
    col = (t % GRID_W).astype(np.float64)
    inv_freq = ROPE_THETA ** (-np.arange(ROPE_PAIRS, dtype=np.float64) / ROPE_PAIRS)
    ang_row = row[:, None] * inv_freq[None, :]
    ang_col = col[:, None] * inv_freq[None, :]
    ang = np.concatenate([ang_row, ang_row, ang_col, ang_col], axis=1)
    sign = np.concatenate([-np.ones(ROPE_PAIRS), np.ones(ROPE_PAIRS)] * 2)
    cos = np.tile(np.cos(ang), (1, 2)).astype(np.float32)
    sin = np.tile(np.sin(ang) * sign[None, :], (1, 2)).astype(np.float32)
    return cos, sin


def _segment_ones(width):
    idx = np.arange(width) // HEAD_DIM
    return (idx[:, None] == idx[None, :]).astype(np.float32)


def _head_sumsq(x, seg):
    sq = x * x
    hi = sq.astype(BF16)
    lo = (sq - hi.astype(F32)).astype(BF16)
    return (jnp.dot(hi, seg, preferred_element_type=F32)
            + jnp.dot(lo, seg, preferred_element_type=F32))


def _rope_partner(x):
    lane = _lane_iota(x.shape)
    fwd = pltpu.roll(x, LANES - ROPE_PAIRS, axis=1)
    bwd = pltpu.roll(x, ROPE_PAIRS, axis=1)
    return jnp.where((lane & ROPE_PAIRS) == 0, fwd, bwd)


def _norm_rope(x, sumsq, gain, cos, sin):
    y = x * lax.rsqrt(sumsq * (1.0 / HEAD_DIM) + EPS) * gain
    return y * cos + _rope_partner(y) * sin


def _inproj_kernel(x_ref, g1_ref, win_ref, qg_ref, kg_ref, cos_ref, sin_ref, seg_ref,
                   q_ref, krep_ref, vext_ref, u_ref, proja_ref, projb_ref, wbf_ref,
                   *, q_scale, n_tiles, seq):
    step = pl.program_id(0)
    rows = x_ref.shape[0]
    kv0 = ATTN_WIDTH
    u0 = ATTN_WIDTH + 2 * KV_WIDTH
    pos0 = pl.multiple_of((jnp.maximum(step - 1, 0) % (seq // rows)) * rows, rows)

    def rope_tables():
        return cos_ref[pl.ds(pos0, rows), :], sin_ref[pl.ds(pos0, rows), :]

    def normed_input():
        x = x_ref[...]
        ms = jnp.mean(x * x, axis=-1, keepdims=True)
        return (x * lax.rsqrt(ms + EPS) * g1_ref[...]).astype(BF16)

    def finish_q(src_ref, g):
        qh = src_ref[:, g * GROUP_WIDTH:(g + 1) * GROUP_WIDTH]
        ss = _head_sumsq(qh, seg_ref[...])
        for j in range(GROUP_WIDTH // LANES):
            sl = slice(j * LANES, (j + 1) * LANES)
            qr = _norm_rope(qh[:, sl], ss[:, sl], qg_ref[...], *rope_tables()) * q_scale
            q_ref[0, g, :, sl] = qr.astype(BF16)

    def finish_kv(src_ref):
        k = src_ref[:, kv0:kv0 + KV_WIDTH]
        kss = _head_sumsq(k, seg_ref[:KV_WIDTH, :KV_WIDTH])
        kr = _norm_rope(k, kss, kg_ref[...], *rope_tables())
        first = _lane_iota(kr.shape) < HEAD_DIM
        kswap = pltpu.roll(kr, HEAD_DIM, axis=1)
        k00 = jnp.where(first, kr, kswap).astype(BF16)
        k11 = jnp.where(first, kswap, kr).astype(BF16)
        for j in range(GROUP_WIDTH // LANES):
            krep_ref[0, 0, :, j * LANES:(j + 1) * LANES] = k00
            krep_ref[0, 1, :, j * LANES:(j + 1) * LANES] = k11
        v = src_ref[:, kv0 + KV_WIDTH:kv0 + 2 * KV_WIDTH]
        vswap = pltpu.roll(v, HEAD_DIM, axis=1)
        vext_ref[0, 0] = jnp.where(first, v, 1.0).astype(BF16)
        vext_ref[0, 1] = jnp.where(first, vswap, 1.0).astype(BF16)

    def finish_u(src_ref, half):
        cs = slice(half * (POOL_WIDTH // 2), (half + 1) * (POOL_WIDTH // 2))
        u_ref[:, cs] = src_ref[:, u0 + cs.start:u0 + cs.stop]

    def finish_pieces(src_ref):
        return [functools.partial(finish_q, src_ref, 0), functools.partial(finish_q, src_ref, 1),
                functools.partial(finish_kv, src_ref),
                functools.partial(finish_u, src_ref, 0), functools.partial(finish_u, src_ref, 1)]

    def project(dst_ref, between=()):
        h = normed_input()
        for c in range(IN_WIDTH // INPROJ_CHUNK):
            cs = slice(c * INPROJ_CHUNK, (c + 1) * INPROJ_CHUNK)
            dst_ref[:, cs] = jnp.dot(h, wbf_ref[:, cs], preferred_element_type=F32)
            if c < len(between):
                between[c]()

    even = step % 2 == 0
    assert n_tiles % 2 == 0

    @pl.when(step == 0)
    def _():
        wbf_ref[...] = win_ref[...].astype(BF16)
        project(proja_ref)

    @pl.when(jnp.logical_and(even, jnp.logical_and(step > 0, step < n_tiles)))
    def _():
        project(proja_ref, finish_pieces(projb_ref))

    @pl.when(jnp.logical_not(even))
    def _():
        project(projb_ref, finish_pieces(proja_ref))

    @pl.when(step == n_tiles)
    def _():
        for piece in finish_pieces(projb_ref):
            piece()


def _attention_kernel(q_ref, krep_ref, vext_ref, wf0_ref, wf1_ref, wf2_ref, wf3_ref,
                      o_ref, wb0_ref, wb1_ref, wb2_ref, wb3_ref,
                      s0_ref, s1_ref, s2_ref, m0_ref, m1_ref, m2_ref, *, seq):
    for wf_ref, wb_ref in ((wf0_ref, wb0_ref), (wf1_ref, wb1_ref),
                           (wf2_ref, wb2_ref), (wf3_ref, wb3_ref)):
        wb_ref[...] = wf_ref[...].astype(BF16)

    tiles_per_group = seq // Q_TILE
    n_tiles = N_KV_HEADS * tiles_per_group
    lane_q = _lane_iota((Q_TILE, GROUP_WIDTH))
    lane_o = _lane_iota((Q_TILE, LANES))

    def tile_pos(t):
        g = t // tiles_per_group
        r0 = pl.multiple_of((t % tiles_per_group) * Q_TILE, Q_TILE)
        return g, r0

    def scores(t, s_ref, m_ref):
        g, r0 = tile_pos(t)
        qt = q_ref[0, g, pl.ds(r0, Q_TILE), :]
        zero = jnp.zeros_like(qt)
        qm = jnp.concatenate(
            [jnp.where((lane_q // HEAD_DIM) == hh, qt, zero) for hh in range(Q_PER_KV)], axis=0)
        s = lax.dot_general(qm, krep_ref[0, g], (((1,), (1,)), ((), ())),
                            preferred_element_type=F32)
        s_ref[...] = s
        m_ref[...] = jnp.max(s, axis=-1, keepdims=True)

    def output(t, s_ref, m_ref):
        g, r0 = tile_pos(t)
        p = jnp.exp2(s_ref[...] - m_ref[...]).astype(BF16)
        oe = jnp.dot(p, vext_ref[0, g], preferred_element_type=F32)
        heads = [oe[hh * Q_TILE:(hh + 1) * Q_TILE] for hh in range(Q_PER_KV)]
        low = lane_o < HEAD_DIM
        for j in range(Q_PER_KV // 2):
            even, odd = heads[2 * j], heads[2 * j + 1]
            even_sw = pltpu.roll(even, HEAD_DIM, axis=1)
            odd_sw = pltpu.roll(odd, HEAD_DIM, axis=1)
            pair = jnp.where(low, even, odd_sw) / jnp.where(low, even_sw, odd)
            o_ref[0, g, pl.ds(r0, Q_TILE), j * LANES:(j + 1) * LANES] = pair.astype(BF16)

    bufs = [(s0_ref, m0_ref), (s1_ref, m1_ref), (s2_ref, m2_ref)]
    lead = len(bufs) - 1
    n_steady = n_tiles - lead
    assert n_steady % (len(bufs) * ATTN_ROTATIONS_PER_STEP) == 0

    for t in range(lead):
        scores(t, *bufs[t])

    def body(jj, carry):
        for k in range(len(bufs) * ATTN_ROTATIONS_PER_STEP):
            t = jj * len(bufs) * ATTN_ROTATIONS_PER_STEP + k
            scores(t + lead, *bufs[(k + lead) % len(bufs)])
            output(t, *bufs[k % len(bufs)])
        return carry

    lax.fori_loop(0, n_steady // (len(bufs) * ATTN_ROTATIONS_PER_STEP), body, 0)

    for t in range(n_steady, n_tiles):
        output(t, *bufs[t % len(bufs)])


def _mix_ffn_kernel(x_ref, a_ref, u_ref, uprev_ref, unext_ref, pw_ref, pb_ref, ps_ref,
                    wout_ref, g2_ref, wg_ref, wu_ref, wd_ref, o_ref,
                    ext_ref, mixp_ref, x1_ref, h_ref, *, seq, d_ff, n_tiles):
    rows = x_ref.shape[0]
    tiles_per_seq = seq // rows
    step = pl.program_id(0)

    st = jnp.minimum(step, n_tiles - 1) % tiles_per_seq

    def stage_windows():
        ext_ref[0:POOL_HALO, :] = jnp.where(st > 0, uprev_ref[...], 0.0)
        ext_ref[POOL_HALO:POOL_HALO + rows, :] = u_ref[...]
        ext_ref[POOL_HALO + rows:, :] = jnp.where(st < tiles_per_seq - 1, unext_ref[...], 0.0)

    def pool_group(g):
        n_ext = rows + 2 * POOL_HALO
        t = st * rows + lax.broadcasted_iota(jnp.int32, (rows, LANES), 0)
        w = POOL_WINDOWS[g]
        cs = slice(g * POOL_GROUP_DIM, (g + 1) * POOL_GROUP_DIM)
        ext = ext_ref[:, cs]
        b = ext
        k = 1
        while k < w:
            b = b + pltpu.roll(b, n_ext - k, axis=0)
            k *= 2
        win = pltpu.roll(b, w // 2, axis=0)[POOL_HALO:POOL_HALO + rows]
        lo = jnp.maximum(t - w // 2, 0)
        hi = jnp.minimum(t - w // 2 + w, seq)
        cnt = (hi - lo).astype(F32)
        pooled = (win / cnt - ext[POOL_HALO:POOL_HALO + rows]).astype(BF16)
        y = jnp.dot(pooled, pw_ref[g], preferred_element_type=F32) + pb_ref[g]
        mixp_ref[:, cs] = (y * ps_ref[:, cs]).astype(BF16)

    def project_out():
        mix = jnp.concatenate([a_ref[0, g] for g in range(N_KV_HEADS)] + [mixp_ref[...]],
                              axis=1)
        x1 = x_ref[...] + jnp.dot(mix, wout_ref[...], preferred_element_type=F32)
        ms = jnp.mean(x1 * x1, axis=-1, keepdims=True)
        x1_ref[...] = x1
        return (x1 * lax.rsqrt(ms + EPS) * g2_ref[...]).astype(BF16)

    bounds = [(c0, min(c0 + FF_CHUNK, d_ff)) for c0 in range(0, d_ff, FF_CHUNK)]
    assert len(bounds) > N_POOL_GROUPS + 1

    @pl.when(step == 0)
    def _():
        stage_windows()
        for g in range(N_POOL_GROUPS):
            pool_group(g)
        h_ref[...] = project_out()

    @pl.when(step > 0)
    def _():
        stage_windows()
        h = h_ref[...]
        acc = x1_ref[...]
        h_next = None
        for c, (c0, c1) in enumerate(bounds):
            gate = jnp.dot(h, wg_ref[:, c0:c1], preferred_element_type=F32)
            up = jnp.dot(h, wu_ref[:, c0:c1], preferred_element_type=F32)
            if c == len(bounds) - 1:
                h_ref[...] = h_next
            act = (gate * jax.nn.sigmoid(gate) * up).astype(BF16)
            acc = acc + jnp.dot(act, wd_ref[c0:c1, :], preferred_element_type=F32)
            if c < N_POOL_GROUPS:
                pool_group(c)
            elif c == N_POOL_GROUPS:
                h_next = project_out()
        o_ref[...] = acc


def _const_spec(shape):
    nd = len(shape)
    return pl.BlockSpec(shape, lambda *_: (0,) * nd)


def _layer(x2, seq, norm1_g, w_in, q_norm_g, k_norm_g, pool_w, pool_b, pool_scale,
           w_out, norm2_g, w_gate, w_up, w_down):
    n_tok = x2.shape[0]
    batch = n_tok // seq
    d_ff = w_gate.shape[1]
    assert seq % ROW_TILE == 0 and seq % Q_TILE == 0 and ROW_TILE % POOL_HALO == 0
    assert seq % INPROJ_ROWS == 0
    n_row_tiles = n_tok // ROW_TILE
    n_in_tiles = n_tok // INPROJ_ROWS

    cos, sin = _rope_tables(seq)
    seg = jnp.asarray(_segment_ones(256), BF16)
    qg = jnp.tile(q_norm_g.astype(F32), LANES // HEAD_DIM)[None, :]
    kg = jnp.tile(k_norm_g.astype(F32), LANES // HEAD_DIM)[None, :]
    q_scale = HEAD_DIM ** -0.5 * math.log2(math.e)

    params = pltpu.CompilerParams(dimension_semantics=("arbitrary",),
                                  vmem_limit_bytes=VMEM_LIMIT_BYTES)

    def grouped_rows(width, rows, tile):
        per_seq = seq // rows
        return pl.BlockSpec((1, N_KV_HEADS, rows, width),
                            lambda i: (tile(i) // per_seq, 0, tile(i) % per_seq, 0))

    def lead_tile(n):
        return lambda i: jnp.minimum(i, n - 1)

    mix_tile = lead_tile(n_row_tiles)
    in_tile = lead_tile(n_in_tiles)

    def lag_tile(i):
        return jnp.maximum(i - 1, 0)

    def grouped_seq(width):
        return pl.BlockSpec((1, N_KV_HEADS, seq, width), lambda b: (b, 0, 0, 0))

    q, krep, vext, u = pl.pallas_call(
        functools.partial(_inproj_kernel, q_scale=q_scale, n_tiles=n_in_tiles, seq=seq),
        grid=(n_in_tiles + 1,),
        in_specs=[
            pl.BlockSpec((INPROJ_ROWS, D_MODEL), lambda i: (in_tile(i), 0)),
            _const_spec((1, D_MODEL)),
            _const_spec((D_MODEL, IN_WIDTH)),
            _const_spec((1, LANES)),
            _const_spec((1, LANES)),
            _const_spec((seq, LANES)),
            _const_spec((seq, LANES)),
            _const_spec((256, 256)),
        ],
        out_specs=[
            grouped_rows(GROUP_WIDTH, INPROJ_ROWS, lag_tile),
            grouped_rows(GROUP_WIDTH, INPROJ_ROWS, lag_tile),
            grouped_rows(LANES, INPROJ_ROWS, lag_tile),
            pl.BlockSpec((INPROJ_ROWS, POOL_WIDTH), lambda i: (lag_tile(i), 0)),
        ],
        out_shape=[
            jax.ShapeDtypeStruct((batch, N_KV_HEADS, seq, GROUP_WIDTH), BF16),
            jax.ShapeDtypeStruct((batch, N_KV_HEADS, seq, GROUP_WIDTH), BF16),
            jax.ShapeDtypeStruct((batch, N_KV_HEADS, seq, LANES), BF16),
            jax.ShapeDtypeStruct((n_tok, POOL_WIDTH), F32),
        ],
        scratch_shapes=[pltpu.VMEM((INPROJ_ROWS, IN_WIDTH), F32),
                        pltpu.VMEM((INPROJ_ROWS, IN_WIDTH), F32),
                        pltpu.VMEM((D_MODEL, IN_WIDTH), BF16)],
        compiler_params=params,
        name="inproj",
    )(x2, norm1_g.astype(F32)[None, :], w_in.astype(F32), qg, kg,
      jnp.asarray(cos), jnp.asarray(sin), seg)

    late_weights = [w_out, w_gate, w_up, w_down]
    for w in late_weights:
        assert w.shape[0] % (batch * BF16_SUBLANES) == 0

    def row_slab(w):
        return pl.BlockSpec((w.shape[0] // batch, w.shape[1]), lambda b: (b, 0))

    attn, w_out_b, w_gate_b, w_up_b, w_down_b = pl.pallas_call(
        functools.partial(_attention_kernel, seq=seq),
        grid=(batch,),
        in_specs=([grouped_seq(GROUP_WIDTH), grouped_seq(GROUP_WIDTH), grouped_seq(LANES)]
                  + [row_slab(w) for w in late_weights]),
        out_specs=[grouped_seq(GROUP_WIDTH)] + [row_slab(w) for w in late_weights],
        out_shape=([jax.ShapeDtypeStruct((batch, N_KV_HEADS, seq, GROUP_WIDTH), BF16)]
                   + [jax.ShapeDtypeStruct(w.shape, BF16) for w in late_weights]),
        scratch_shapes=([pltpu.VMEM((Q_PER_KV * Q_TILE, seq), F32)] * 3
                        + [pltpu.VMEM((Q_PER_KV * Q_TILE, 1), F32)] * 3),
        compiler_params=params,
        name="attention",
    )(q, krep, vext, *[w.astype(F32) for w in late_weights])

    halo_blocks = ROW_TILE // POOL_HALO
    last_halo = n_tok // POOL_HALO - 1
    out = pl.pallas_call(
        functools.partial(_mix_ffn_kernel, seq=seq, d_ff=d_ff, n_tiles=n_row_tiles),
        grid=(n_row_tiles + 1,),
        in_specs=[
            pl.BlockSpec((ROW_TILE, D_MODEL), lambda i: (mix_tile(i), 0)),
            grouped_rows(GROUP_WIDTH, ROW_TILE, mix_tile),
            pl.BlockSpec((ROW_TILE, POOL_WIDTH), lambda i: (mix_tile(i), 0)),
            pl.BlockSpec((POOL_HALO, POOL_WIDTH),
                         lambda i: (jnp.maximum(mix_tile(i) * halo_blocks - 1, 0), 0)),
            pl.BlockSpec((POOL_HALO, POOL_WIDTH),
                         lambda i: (jnp.minimum((mix_tile(i) + 1) * halo_blocks, last_halo), 0)),
            _const_spec((N_POOL_GROUPS, POOL_GROUP_DIM, POOL_GROUP_DIM)),
            _const_spec((N_POOL_GROUPS, 1, POOL_GROUP_DIM)),
            _const_spec((1, POOL_WIDTH)),
            _const_spec((D_MODEL, D_MODEL)),
            _const_spec((1, D_MODEL)),
            _const_spec((D_MODEL, d_ff)),
            _const_spec((D_MODEL, d_ff)),
            _const_spec((d_ff, D_MODEL)),
        ],
        out_specs=pl.BlockSpec((ROW_TILE, D_MODEL), lambda i: (jnp.maximum(i - 1, 0), 0)),
        out_shape=jax.ShapeDtypeStruct((n_tok, D_MODEL), F32),
        scratch_shapes=[pltpu.VMEM((ROW_TILE + 2 * POOL_HALO, POOL_WIDTH), F32),
                        pltpu.VMEM((ROW_TILE, POOL_WIDTH), BF16),
                        pltpu.VMEM((ROW_TILE, D_MODEL), F32), pltpu.VMEM((ROW_TILE, D_MODEL), BF16)],
        compiler_params=params,
        name="mix_ffn",
    )(x2, attn, u, u, u, pool_w.astype(BF16), pool_b.astype(F32)[:, None, :],
      pool_scale.astype(F32)[None, :], w_out_b, norm2_g.astype(F32)[None, :],
      w_gate_b, w_up_b, w_down_b)
    return out


def kernel(x, norm1_g, w_in, q_norm_g, k_norm_g, pool_w, pool_b, pool_scale,
           w_out, norm2_g, w_gate, w_up, w_down):
    batch, seq, d_model = x.shape
    assert d_model == D_MODEL and w_in.shape[-1] == IN_WIDTH
    x2 = x.reshape(batch * seq, d_model)
    for l in range(norm1_g.shape[0]):
        x2 = _layer(x2, seq, norm1_g[l], w_in[l], q_norm_g[l], k_norm_g[l], pool_w[l],
                    pool_b[l], pool_scale[l], w_out[l], norm2_g[l], w_gate[l], w_up[l],
                    w_down[l])
    return x2.reshape(batch, seq, d_model)
```

```python
import functools
import math

import jax
import jax.numpy as jnp
import numpy as np
from jax import lax
from jax.experimental import pallas as pl
from jax.experimental.pallas import tpu as pltpu

D_MODEL = 1024
HEAD_DIM = 64
N_Q_HEADS = 8
N_KV_HEADS = 2
Q_PER_KV = N_Q_HEADS // N_KV_HEADS
ATTN_WIDTH = N_Q_HEADS * HEAD_DIM
KV_WIDTH = N_KV_HEADS * HEAD_DIM
GROUP_WIDTH = Q_PER_KV * HEAD_DIM
ROPE_PAIRS = HEAD_DIM // 4
ROPE_THETA = 10000.0
POOL_WINDOWS = (2, 4, 8, 16)
N_POOL_GROUPS = len(POOL_WINDOWS)
POOL_WIDTH = D_MODEL - ATTN_WIDTH
POOL_GROUP_DIM = POOL_WIDTH // N_POOL_GROUPS
IN_WIDTH = ATTN_WIDTH + 2 * KV_WIDTH + POOL_WIDTH
GRID_W = 64
EPS = 1e-6

LANES = 128
BF16_SUBLANES = 16
POOL_HALO = 8
VMEM_LIMIT_BYTES = 56 * 1024 * 1024

ROW_TILE = 512
INPROJ_ROWS = 1024
Q_TILE = 128
ATTN_ROTATIONS_PER_STEP = 3
FF_CHUNK = 512
INPROJ_CHUNK = 256

F32 = jnp.float32
BF16 = jnp.bfloat16


def _lane_iota(shape):
    return lax.broadcasted_iota(jnp.int32, shape, len(shape) - 1)


def _rope_tables(seq):
    t = np.arange(seq)
    row = (t // GRID_W).astype(np.float64)
    col = (t % GRID_W).astype(np.float64)
    inv_freq = ROPE_THETA ** (-np.arange(ROPE_PAIRS, dtype=np.float64) / ROPE_PAIRS)
    ang_row = row[:, None] * inv_freq[None, :]
    ang_col = col[:, None] * inv_freq[None, :]
    ang = np.concatenate([ang_row, ang_row, ang_col, ang_col], axis=1)
    sign = np.concatenate([-np.ones(ROPE_PAIRS), np.ones(ROPE_PAIRS)] * 2)
    cos = np.tile(np.cos(ang), (1, 2)).astype(np.float32)
    sin = np.tile(np.sin(ang) * sign[None, :], (1, 2)).astype(np.float32)
    return cos, sin


def _segment_ones(width):
    idx = np.arange(width) // HEAD_DIM
    return (idx[:, None] == idx[None, :]).astype(np.float32)


def _head_sumsq(x, seg):
    sq = x * x
    hi = sq.astype(BF16)
    lo = (sq - hi.astype(F32)).astype(BF16)
    return (jnp.dot(hi, seg, preferred_element_type=F32)
            + jnp.dot(lo, seg, preferred_element_type=F32))


def _rope_partner(x):
    lane = _lane_iota(x.shape)
    fwd = pltpu.roll(x, LANES - ROPE_PAIRS, axis=1)
    bwd = pltpu.roll(x, ROPE_PAIRS, axis=1)
    return jnp.where((lane & ROPE_PAIRS) == 0, fwd, bwd)


def _norm_rope(x, sumsq, gain, cos, sin):
    y = x * lax.rsqrt(sumsq * (1.0 / HEAD_DIM) + EPS) * gain
    return y * cos + _rope_partner(y) * sin


def _inproj_kernel(x_ref, g1_ref, win_ref, qg_ref, kg_ref, cos_ref, sin_ref, seg_ref,
                   q_ref, krep_ref, vext_ref, u_ref, proja_ref, projb_ref, wbf_ref,
                   *, q_scale, n_tiles, seq):
    step = pl.program_id(0)
    rows = x_ref.shape[0]
    kv0 = ATTN_WIDTH
    u0 = ATTN_WIDTH + 2 * KV_WIDTH
    pos0 = pl.multiple_of((jnp.maximum(step - 1, 0) % (seq // rows)) * rows, rows)

    def rope_tables():
        return cos_ref[pl.ds(pos0, rows), :], sin_ref[pl.ds(pos0, rows), :]

    def normed_input():
        x = x_ref[...]
        ms = jnp.mean(x * x, axis=-1, keepdims=True)
        return (x * lax.rsqrt(ms + EPS) * g1_ref[...]).astype(BF16)

    def finish_q(src_ref, g):
        qh = src_ref[:, g * GROUP_WIDTH:(g + 1) * GROUP_WIDTH]
        ss = _head_sumsq(qh, seg_ref[...])
        for j in range(GROUP_WIDTH // LANES):
            sl = slice(j * LANES, (j + 1) * LANES)
            qr = _norm_rope(qh[:, sl], ss[:, sl], qg_ref[...], *rope_tables()) * q_scale
            q_ref[0, g, :, sl] = qr.astype(BF16)

    def finish_kv(src_ref):
        k = src_ref[:, kv0:kv0 + KV_WIDTH]
        kss = _head_sumsq(k, seg_ref[:KV_WIDTH, :KV_WIDTH])
        kr = _norm_rope(k, kss, kg_ref[...], *rope_tables())
        first = _lane_iota(kr.shape) < HEAD_DIM
        kswap = pltpu.roll(kr, HEAD_DIM, axis=1)
        k00 = jnp.where(first, kr, kswap).astype(BF16)
        k11 = jnp.where(first, kswap, kr).astype(BF16)
        for j in range(GROUP_WIDTH // LANES):
            krep_ref[0, 0, :, j * LANES:(j + 1) * LANES] = k00
            krep_ref[0, 1, :, j * LANES:(j + 1) * LANES] = k11
        v = src_ref[:, kv0 + KV_WIDTH:kv0 + 2 * KV_WIDTH]
        vswap = pltpu.roll(v, HEAD_DIM, axis=1)
        vext_ref[0, 0] = jnp.where(first, v, 1.0).astype(BF16)
        vext_ref[0, 1] = jnp.where(first, vswap, 1.0).astype(BF16)

    def finish_u(src_ref, half):
        cs = slice(half * (POOL_WIDTH // 2), (half + 1) * (POOL_WIDTH // 2))
        u_ref[:, cs] = src_ref[:, u0 + cs.start:u0 + cs.stop]

    def finish_pieces(src_ref):
        return [functools.partial(finish_q, src_ref, 0), functools.partial(finish_q, src_ref, 1),
                functools.partial(finish_kv, src_ref),
                functools.partial(finish_u, src_ref, 0), functools.partial(finish_u, src_ref, 1)]

    def project(dst_ref, between=()):
        h = normed_input()
        for c in range(IN_WIDTH // INPROJ_CHUNK):
            cs = slice(c * INPROJ_CHUNK, (c + 1) * INPROJ_CHUNK)
            dst_ref[:, cs] = jnp.dot(h, wbf_ref[:, cs], preferred_element_type=F32)
            if c < len(between):
                between[c]()

    even = step % 2 == 0
    assert n_tiles % 2 == 0

    @pl.when(step == 0)
    def _():
        wbf_ref[...] = win_ref[...].astype(BF16)
        project(proja_ref)

    @pl.when(jnp.logical_and(even, jnp.logical_and(step > 0, step < n_tiles)))
    def _():
        project(proja_ref, finish_pieces(projb_ref))

    @pl.when(jnp.logical_not(even))
    def _():
        project(projb_ref, finish_pieces(proja_ref))

    @pl.when(step == n_tiles)
    def _():
        for piece in finish_pieces(projb_ref):
            piece()


def _attention_kernel(q_ref, krep_ref, vext_ref, wf0_ref, wf1_ref, wf2_ref, wf3_ref,
                      o_ref, wb0_ref, wb1_ref, wb2_ref, wb3_ref,
                      s0_ref, s1_ref, s2_ref, mv0_ref, mv1_ref, mv2_ref,
                      m0_ref, m1_ref, m2_ref, oe0_ref, oe1_ref, oe2_ref, *, seq):
    for wf_ref, wb_ref in ((wf0_ref, wb0_ref), (wf1_ref, wb1_ref),
                           (wf2_ref, wb2_ref), (wf3_ref, wb3_ref)):
        wb_ref[...] = wf_ref[...].astype(BF16)

    tiles_per_group = seq // Q_TILE
    n_tiles = N_KV_HEADS * tiles_per_group
    lane_q = _lane_iota((Q_TILE, GROUP_WIDTH))
    lane_o = _lane_iota((Q_TILE, LANES))

    def tile_pos(t):
        g = t // tiles_per_group
        r0 = pl.multiple_of((t % tiles_per_group) * Q_TILE, Q_TILE)
        return g, r0

    def scores(t, s_ref, mv_ref):
        g, r0 = tile_pos(t)
        qt = q_ref[0, g, pl.ds(r0, Q_TILE), :]
        zero = jnp.zeros_like(qt)
        qm = jnp.concatenate(
            [jnp.where((lane_q // HEAD_DIM) == hh, qt, zero) for hh in range(Q_PER_KV)], axis=0)
        s = lax.dot_general(qm, krep_ref[0, g], (((1,), (1,)), ((), ())),
                            preferred_element_type=F32)
        s_ref[...] = s
        mv_ref[...] = functools.reduce(
            jnp.maximum, [s[:, j * LANES:(j + 1) * LANES] for j in range(seq // LANES)])

    def row_max(mv_ref, m_ref):
        m_ref[...] = jnp.max(mv_ref[...], axis=-1, keepdims=True)

    def output(t, s_ref, m_ref, oe_ref):
        g, _ = tile_pos(t)
        p = jnp.exp2(s_ref[...] - m_ref[...]).astype(BF16)
        oe_ref[...] = jnp.dot(p, vext_ref[0, g], preferred_element_type=F32)

    def finalize(t, oe_ref):
        g, r0 = tile_pos(t)
        low = lane_o < HEAD_DIM
        for j in range(Q_PER_KV // 2):
            even = oe_ref[2 * j * Q_TILE:(2 * j + 1) * Q_TILE, :]
            odd = oe_ref[(2 * j + 1) * Q_TILE:(2 * j + 2) * Q_TILE, :]
            even_sw = pltpu.roll(even, HEAD_DIM, axis=1)
            odd_sw = pltpu.roll(odd, HEAD_DIM, axis=1)
            pair = jnp.where(low, even, odd_sw) / jnp.where(low, even_sw, odd)
            o_ref[0, g, pl.ds(r0, Q_TILE), j * LANES:(j + 1) * LANES] = pair.astype(BF16)

    n_buf = 3
    s_refs, mv_refs = (s0_ref, s1_ref, s2_ref), (mv0_ref, mv1_ref, mv2_ref)
    m_refs, oe_refs = (m0_ref, m1_ref, m2_ref), (oe0_ref, oe1_ref, oe2_ref)

    def slot(t, k, first=False):
        if t_plus(t, 1) is not None:
            row_max(mv_refs[(k + 1) % n_buf], m_refs[(k + 1) % n_buf])
        if not first:
            finalize(t - 1, oe_refs[(k - 1) % n_buf])
        if t_plus(t, 2) is not None:
            scores(t + 2, s_refs[(k + 2) % n_buf], mv_refs[(k + 2) % n_buf])
        output(t, s_refs[k % n_buf], m_refs[k % n_buf], oe_refs[k % n_buf])

    def t_plus(t, d):
        return t + d if not isinstance(t, int) or t + d < n_tiles else None

    per_body = n_buf * ATTN_ROTATIONS_PER_STEP
    n_loop = (n_tiles - 3) // per_body * per_body
    assert n_loop > 0

    scores(0, s_refs[0], mv_refs[0])
    scores(1, s_refs[1], mv_refs[1])
    row_max(mv_refs[0], m_refs[0])
    slot(0, 0, first=True)

    def body(jj, carry):
        for i in range(per_body):
            slot(1 + jj * per_body + i, (1 + i) % n_buf)
        return carry

    lax.fori_loop(0, n_loop // per_body, body, 0)

    for t in range(1 + n_loop, n_tiles):
        slot(t, t % n_buf)
    finalize(n_tiles - 1, oe_refs[(n_tiles - 1) % n_buf])


def _mix_ffn_kernel(x_ref, a_ref, u_ref, uprev_ref, unext_ref, pw_ref, pb_ref, ps_ref,
                    wout_ref, g2_ref, wg_ref, wu_ref, wd_ref, o_ref,
                    ext_ref, mixp_ref, x1_ref, h_ref, *, seq, d_ff, n_tiles):
    rows = x_ref.shape[0]
    tiles_per_seq = seq // rows
    step = pl.program_id(0)

    st = jnp.minimum(step, n_tiles - 1) % tiles_per_seq

    def stage_windows():
        ext_ref[0:POOL_HALO, :] = jnp.where(st > 0, uprev_ref[...], 0.0)
        ext_ref[POOL_HALO:POOL_HALO + rows, :] = u_ref[...]
        ext_ref[POOL_HALO + rows:, :] = jnp.where(st < tiles_per_seq - 1, unext_ref[...], 0.0)

    def pool_group(g):
        n_ext = rows + 2 * POOL_HALO
        t = st * rows + lax.broadcasted_iota(jnp.int32, (rows, LANES), 0)
        w = POOL_WINDOWS[g]
        cs = slice(g * POOL_GROUP_DIM, (g + 1) * POOL_GROUP_DIM)
        ext = ext_ref[:, cs]
        b = ext
        k = 1
        while k < w:
            b = b + pltpu.roll(b, n_ext - k, axis=0)
            k *= 2
        win = pltpu.roll(b, w // 2, axis=0)[POOL_HALO:POOL_HALO + rows]
        lo = jnp.maximum(t - w // 2, 0)
        hi = jnp.minimum(t - w // 2 + w, seq)
        cnt = (hi - lo).astype(F32)
        pooled = (win / cnt - ext[POOL_HALO:POOL_HALO + rows]).astype(BF16)
        y = jnp.dot(pooled, pw_ref[g], preferred_element_type=F32) + pb_ref[g]
        mixp_ref[:, cs] = (y * ps_ref[:, cs]).astype(BF16)

    def project_out():
        mix = jnp.concatenate([a_ref[0, g] for g in range(N_KV_HEADS)] + [mixp_ref[...]],
                              axis=1)
        x1 = x_ref[...] + jnp.dot(mix, wout_ref[...], preferred_element_type=F32)
        ms = jnp.mean(x1 * x1, axis=-1, keepdims=True)
        x1_ref[...] = x1
        return (x1 * lax.rsqrt(ms + EPS) * g2_ref[...]).astype(BF16)

    bounds = [(c0, min(c0 + FF_CHUNK, d_ff)) for c0 in range(0, d_ff, FF_CHUNK)]
    assert len(bounds) > N_POOL_GROUPS + 1

    @pl.when(step == 0)
    def _():
        stage_windows()
        for g in range(N_POOL_GROUPS):
            pool_group(g)
        h_ref[...] = project_out()

    @pl.when(step > 0)
    def _():
        stage_windows()
        h = h_ref[...]
        acc = x1_ref[...]
        h_next = None
        for c, (c0, c1) in enumerate(bounds):
            gate = jnp.dot(h, wg_ref[:, c0:c1], preferred_element_type=F32)
            up = jnp.dot(h, wu_ref[:, c0:c1], preferred_element_type=F32)
            if c == len(bounds) - 1:
                h_ref[...] = h_next
            act = (gate * jax.nn.sigmoid(gate) * up).astype(BF16)
            acc = acc + jnp.dot(act, wd_ref[c0:c1, :], preferred_element_type=F32)
            if c < N_POOL_GROUPS:
                pool_group(c)
            elif c == N_POOL_GROUPS:
                h_next = project_out()
        o_ref[...] = acc


def _const_spec(shape):
    nd = len(shape)
    return pl.BlockSpec(shape, lambda *_: (0,) * nd)


def _layer(x2, seq, norm1_g, w_in, q_norm_g, k_norm_g, pool_w, pool_b, pool_scale,
           w_out, norm2_g, w_gate, w_up, w_down):
    n_tok = x2.shape[0]
    batch = n_tok // seq
    d_ff = w_gate.shape[1]
    assert seq % ROW_TILE == 0 and seq % Q_TILE == 0 and ROW_TILE % POOL_HALO == 0
    assert seq % INPROJ_ROWS == 0
    n_row_tiles = n_tok // ROW_TILE
    n_in_tiles = n_tok // INPROJ_ROWS

    cos, sin = _rope_tables(seq)
    seg = jnp.asarray(_segment_ones(256), BF16)
    qg = jnp.tile(q_norm_g.astype(F32), LANES // HEAD_DIM)[None, :]
    kg = jnp.tile(k_norm_g.astype(F32), LANES // HEAD_DIM)[None, :]
    q_scale = HEAD_DIM ** -0.5 * math.log2(math.e)

    params = pltpu.CompilerParams(dimension_semantics=("arbitrary",),
                                  vmem_limit_bytes=VMEM_LIMIT_BYTES)

    def grouped_rows(width, rows, tile):
        per_seq = seq // rows
        return pl.BlockSpec((1, N_KV_HEADS, rows, width),
                            lambda i: (tile(i) // per_seq, 0, tile(i) % per_seq, 0))

    def lead_tile(n):
        return lambda i: jnp.minimum(i, n - 1)

    mix_tile = lead_tile(n_row_tiles)
    in_tile = lead_tile(n_in_tiles)

    def lag_tile(i):
        return jnp.maximum(i - 1, 0)

    def grouped_seq(width):
        return pl.BlockSpec((1, N_KV_HEADS, seq, width), lambda b: (b, 0, 0, 0))

    q, krep, vext, u = pl.pallas_call(
        functools.partial(_inproj_kernel, q_scale=q_scale, n_tiles=n_in_tiles, seq=seq),
        grid=(n_in_tiles + 1,),
        in_specs=[
            pl.BlockSpec((INPROJ_ROWS, D_MODEL), lambda i: (in_tile(i), 0)),
            _const_spec((1, D_MODEL)),
            _const_spec((D_MODEL, IN_WIDTH)),
            _const_spec((1, LANES)),
            _const_spec((1, LANES)),
            _const_spec((seq, LANES)),
            _const_spec((seq, LANES)),
            _const_spec((256, 256)),
        ],
        out_specs=[
            grouped_rows(GROUP_WIDTH, INPROJ_ROWS, lag_tile),
            grouped_rows(GROUP_WIDTH, INPROJ_ROWS, lag_tile),
            grouped_rows(LANES, INPROJ_ROWS, lag_tile),
            pl.BlockSpec((INPROJ_ROWS, POOL_WIDTH), lambda i: (lag_tile(i), 0)),
        ],
        out_shape=[
            jax.ShapeDtypeStruct((batch, N_KV_HEADS, seq, GROUP_WIDTH), BF16),
            jax.ShapeDtypeStruct((batch, N_KV_HEADS, seq, GROUP_WIDTH), BF16),
            jax.ShapeDtypeStruct((batch, N_KV_HEADS, seq, LANES), BF16),
            jax.ShapeDtypeStruct((n_tok, POOL_WIDTH), F32),
        ],
        scratch_shapes=[pltpu.VMEM((INPROJ_ROWS, IN_WIDTH), F32),
                        pltpu.VMEM((INPROJ_ROWS, IN_WIDTH), F32),
                        pltpu.VMEM((D_MODEL, IN_WIDTH), BF16)],
        compiler_params=params,
        name="inproj",
    )(x2, norm1_g.astype(F32)[None, :], w_in.astype(F32), qg, kg,
      jnp.asarray(cos), jnp.asarray(sin), seg)

    late_weights = [w_out, w_gate, w_up, w_down]
    for w in late_weights:
        assert w.shape[0] % (batch * BF16_SUBLANES) == 0

    def row_slab(w):
        return pl.BlockSpec((w.shape[0] // batch, w.shape[1]), lambda b: (b, 0))

    attn, w_out_b, w_gate_b, w_up_b, w_down_b = pl.pallas_call(
        functools.partial(_attention_kernel, seq=seq),
        grid=(batch,),
        in_specs=([grouped_seq(GROUP_WIDTH), grouped_seq(GROUP_WIDTH), grouped_seq(LANES)]
                  + [row_slab(w) for w in late_weights]),
        out_specs=[grouped_seq(GROUP_WIDTH)] + [row_slab(w) for w in late_weights],
        out_shape=([jax.ShapeDtypeStruct((batch, N_KV_HEADS, seq, GROUP_WIDTH), BF16)]
                   + [jax.ShapeDtypeStruct(w.shape, BF16) for w in late_weights]),
        scratch_shapes=([pltpu.VMEM((Q_PER_KV * Q_TILE, seq), F32)] * 3
                        + [pltpu.VMEM((Q_PER_KV * Q_TILE, LANES), F32)] * 3
                        + [pltpu.VMEM((Q_PER_KV * Q_TILE, 1), F32)] * 3
                        + [pltpu.VMEM((Q_PER_KV * Q_TILE, LANES), F32)] * 3),
        compiler_params=params,
        name="attention",
    )(q, krep, vext, *[w.astype(F32) for w in late_weights])

    halo_blocks = ROW_TILE // POOL_HALO
    last_halo = n_tok // POOL_HALO - 1
    out = pl.pallas_call(
        functools.partial(_mix_ffn_kernel, seq=seq, d_ff=d_ff, n_tiles=n_row_tiles),
        grid=(n_row_tiles + 1,),
        in_specs=[
            pl.BlockSpec((ROW_TILE, D_MODEL), lambda i: (mix_tile(i), 0)),
            grouped_rows(GROUP_WIDTH, ROW_TILE, mix_tile),
            pl.BlockSpec((ROW_TILE, POOL_WIDTH), lambda i: (mix_tile(i), 0)),
            pl.BlockSpec((POOL_HALO, POOL_WIDTH),
                         lambda i: (jnp.maximum(mix_tile(i) * halo_blocks - 1, 0), 0)),
            pl.BlockSpec((POOL_HALO, POOL_WIDTH),
                         lambda i: (jnp.minimum((mix_tile(i) + 1) * halo_blocks, last_halo), 0)),
            _const_spec((N_POOL_GROUPS, POOL_GROUP_DIM, POOL_GROUP_DIM)),
            _const_spec((N_POOL_GROUPS, 1, POOL_GROUP_DIM)),
            _const_spec((1, POOL_WIDTH)),
            _const_spec((D_MODEL, D_MODEL)),
            _const_spec((1, D_MODEL)),
            _const_spec((D_MODEL, d_ff)),
            _const_spec((D_MODEL, d_ff)),
            _const_spec((d_ff, D_MODEL)),
        ],
        out_specs=pl.BlockSpec((ROW_TILE, D_MODEL), lambda i: (jnp.maximum(i - 1, 0), 0)),
        out_shape=jax.ShapeDtypeStruct((n_tok, D_MODEL), F32),
        scratch_shapes=[pltpu.VMEM((ROW_TILE + 2 * POOL_HALO, POOL_WIDTH), F32),
                        pltpu.VMEM((ROW_TILE, POOL_WIDTH), BF16),
                        pltpu.VMEM((ROW_TILE, D_MODEL), F32), pltpu.VMEM((ROW_TILE, D_MODEL), BF16)],
        compiler_params=params,
        name="mix_ffn",
    )(x2, attn, u, u, u, pool_w.astype(BF16), pool_b.astype(F32)[:, None, :],
      pool_scale.astype(F32)[None, :], w_out_b, norm2_g.astype(F32)[None, :],
      w_gate_b, w_up_b, w_down_b)
    return out


def kernel(x, norm1_g, w_in, q_norm_g, k_norm_g, pool_w, pool_b, pool_scale,
           w_out, norm2_g, w_gate, w_up, w_down):
    batch, seq, d_model = x.shape
    assert d_model == D_MODEL and w_in.shape[-1] == IN_WIDTH
    x2 = x.reshape(batch * seq, d_model)
    for l in range(norm1_g.shape[0]):
        x2 = _layer(x2, seq, norm1_g[l], w_in[l], q_norm_g[l], k_norm_g[l], pool_w[l],
                    pool_b[l], pool_scale[l], w_out[l], norm2_g[l], w_gate[l], w_up[l],
                    w_down[l])
    return x2.reshape(batch, seq, d_model)
```

```python
import functools
import math

import jax
import jax.numpy as jnp
import numpy as np
from jax import lax
from jax.experimental import pallas as pl
from jax.experimental.pallas import tpu as pltpu

D_MODEL = 1024
HEAD_DIM = 64
N_Q_HEADS = 8
N_KV_HEADS = 2
Q_PER_KV = N_Q_HEADS // N_KV_HEADS
ATTN_WIDTH = N_Q_HEADS * HEAD_DIM
KV_WIDTH = N_KV_HEADS * HEAD_DIM
GROUP_WIDTH = Q_PER_KV * HEAD_DIM
ROPE_PAIRS = HEAD_DIM // 4
ROPE_THETA = 10000.0
POOL_WINDOWS = (2, 4, 8, 16)
N_POOL_GROUPS = len(POOL_WINDOWS)
POOL_WIDTH = D_MODEL - ATTN_WIDTH
POOL_GROUP_DIM = POOL_WIDTH // N_POOL_GROUPS
IN_WIDTH = ATTN_WIDTH + 2 * KV_WIDTH + POOL_WIDTH
GRID_W = 64
EPS = 1e-6

LANES = 128
BF16_SUBLANES = 16
POOL_HALO = 8
VMEM_LIMIT_BYTES = 56 * 1024 * 1024

ROW_TILE = 512
INPROJ_ROWS = 1024
Q_TILE = 128
ATTN_ROTATIONS_PER_STEP = 3
FF_CHUNK = 512
INPROJ_CHUNK = 256

F32 = jnp.float32
BF16 = jnp.bfloat16


def _lane_iota(shape):
    return lax.broadcasted_iota(jnp.int32, shape, len(shape) - 1)


def _rope_tables(seq):
    t = np.arange(seq)
    row = (t // GRID_W).astype(np.float64)
    col = (t % GRID_W).astype(np.float64)
    inv_freq = ROPE_THETA ** (-np.arange(ROPE_PAIRS, dtype=np.float64) / ROPE_PAIRS)
    ang_row = row[:, None] * inv_freq[None, :]
    ang_col = col[:, None] * inv_freq[None, :]
    ang = np.concatenate([ang_row, ang_row, ang_col, ang_col], axis=1)
    sign = np.concatenate([-np.ones(ROPE_PAIRS), np.ones(ROPE_PAIRS)] * 2)
    cos = np.tile(np.cos(ang), (1, 2)).astype(np.float32)
    sin = np.tile(np.sin(ang) * sign[None, :], (1, 2)).astype(np.float32)
    return cos, sin


def _segment_ones(width):
    idx = np.arange(width) // HEAD_DIM
    return (idx[:, None] == idx[None, :]).astype(np.float32)


def _head_sumsq(x, seg):
    sq = x * x
    hi = sq.astype(BF16)
    lo = (sq - hi.astype(F32)).astype(BF16)
    return (jnp.dot(hi, seg, preferred_element_type=F32)
            + jnp.dot(lo, seg, preferred_element_type=F32))


def _rope_partner(x):
    lane = _lane_iota(x.shape)
    fwd = pltpu.roll(x, LANES - ROPE_PAIRS, axis=1)
    bwd = pltpu.roll(x, ROPE_PAIRS, axis=1)
    return jnp.where((lane & ROPE_PAIRS) == 0, fwd, bwd)


def _norm_rope(x, sumsq, gain, cos, sin):
    y = x * lax.rsqrt(sumsq * (1.0 / HEAD_DIM) + EPS) * gain
    return y * cos + _rope_partner(y) * sin


def _inproj_kernel(x_ref, g1_ref, win_ref, qg_ref, kg_ref, cos_ref, sin_ref, seg_ref,
                   q_ref, krep_ref, vext_ref, u_ref, proja_ref, projb_ref, wbf_ref,
                   *, q_scale, n_tiles, seq):
    step = pl.program_id(0)
    rows = x_ref.shape[0]
    kv0 = ATTN_WIDTH
    u0 = ATTN_WIDTH + 2 * KV_WIDTH
    pos0 = pl.multiple_of((jnp.maximum(step - 1, 0) % (seq // rows)) * rows, rows)

    def rope_tables():
        return cos_ref[pl.ds(pos0, rows), :], sin_ref[pl.ds(pos0, rows), :]

    def head_pair_gain(g_ref):
        g = g_ref[...]
        return jnp.concatenate([g] * (LANES // HEAD_DIM), axis=1)

    def normed_input():
        x = x_ref[...]
        ms = jnp.mean(x * x, axis=-1, keepdims=True)
        return (x * lax.rsqrt(ms + EPS) * g1_ref[...]).astype(BF16)

    def finish_q(src_ref, g):
        qh = src_ref[:, g * GROUP_WIDTH:(g + 1) * GROUP_WIDTH]
        ss = _head_sumsq(qh, seg_ref[...])
        for j in range(GROUP_WIDTH // LANES):
            sl = slice(j * LANES, (j + 1) * LANES)
            qr = _norm_rope(qh[:, sl], ss[:, sl], head_pair_gain(qg_ref), *rope_tables()) * q_scale
            q_ref[0, g, :, sl] = qr.astype(BF16)

    def finish_kv(src_ref):
        k = src_ref[:, kv0:kv0 + KV_WIDTH]
        kss = _head_sumsq(k, seg_ref[:KV_WIDTH, :KV_WIDTH])
        kr = _norm_rope(k, kss, head_pair_gain(kg_ref), *rope_tables())
        first = _lane_iota(kr.shape) < HEAD_DIM
        kswap = pltpu.roll(kr, HEAD_DIM, axis=1)
        k00 = jnp.where(first, kr, kswap).astype(BF16)
        k11 = jnp.where(first, kswap, kr).astype(BF16)
        for j in range(GROUP_WIDTH // LANES):
            krep_ref[0, 0, :, j * LANES:(j + 1) * LANES] = k00
            krep_ref[0, 1, :, j * LANES:(j + 1) * LANES] = k11
        v = src_ref[:, kv0 + KV_WIDTH:kv0 + 2 * KV_WIDTH]
        vswap = pltpu.roll(v, HEAD_DIM, axis=1)
        vext_ref[0, 0] = jnp.where(first, v, 1.0).astype(BF16)
        vext_ref[0, 1] = jnp.where(first, vswap, 1.0).astype(BF16)

    def finish_u(src_ref, half):
        cs = slice(half * (POOL_WIDTH // 2), (half + 1) * (POOL_WIDTH // 2))
        u_ref[:, cs] = src_ref[:, u0 + cs.start:u0 + cs.stop]

    def finish_pieces(src_ref):
        return [functools.partial(finish_q, src_ref, 0), functools.partial(finish_q, src_ref, 1),
                functools.partial(finish_kv, src_ref),
                functools.partial(finish_u, src_ref, 0), functools.partial(finish_u, src_ref, 1)]

    def project(dst_ref, between=()):
        h = normed_input()
        for c in range(IN_WIDTH // INPROJ_CHUNK):
            cs = slice(c * INPROJ_CHUNK, (c + 1) * INPROJ_CHUNK)
            dst_ref[:, cs] = jnp.dot(h, wbf_ref[:, cs], preferred_element_type=F32)
            if c < len(between):
                between[c]()

    even = step % 2 == 0
    assert n_tiles % 2 == 0

    @pl.when(step == 0)
    def _():
        wbf_ref[...] = win_ref[...].astype(BF16)
        project(proja_ref)

    @pl.when(jnp.logical_and(even, jnp.logical_and(step > 0, step < n_tiles)))
    def _():
        project(proja_ref, finish_pieces(projb_ref))

    @pl.when(jnp.logical_not(even))
    def _():
        project(projb_ref, finish_pieces(proja_ref))

    @pl.when(step == n_tiles)
    def _():
        for piece in finish_pieces(projb_ref):
            piece()


def _attention_kernel(q_ref, krep_ref, vext_ref, wf0_ref, wf1_ref, wf2_ref, wf3_ref,
                      o_ref, wb0_ref, wb1_ref, wb2_ref, wb3_ref,
                      s0_ref, s1_ref, s2_ref, mv0_ref, mv1_ref, mv2_ref,
                      m0_ref, m1_ref, m2_ref, oe0_ref, oe1_ref, oe2_ref, *, seq):
    for wf_ref, wb_ref in ((wf0_ref, wb0_ref), (wf1_ref, wb1_ref),
                           (wf2_ref, wb2_ref), (wf3_ref, wb3_ref)):
        wb_ref[...] = wf_ref[...].astype(BF16)

    tiles_per_group = seq // Q_TILE
    n_tiles = N_KV_HEADS * tiles_per_group
    lane_q = _lane_iota((Q_TILE, GROUP_WIDTH))
    lane_o = _lane_iota((Q_TILE, LANES))

    def tile_pos(t):
        g = t // tiles_per_group
        r0 = pl.multiple_of((t % tiles_per_group) * Q_TILE, Q_TILE)
        return g, r0

    def scores(t, s_ref, mv_ref):
        g, r0 = tile_pos(t)
        qt = q_ref[0, g, pl.ds(r0, Q_TILE), :]
        zero = jnp.zeros_like(qt)
        qm = jnp.concatenate(
            [jnp.where((lane_q // HEAD_DIM) == hh, qt, zero) for hh in range(Q_PER_KV)], axis=0)
        s = lax.dot_general(qm, krep_ref[0, g], (((1,), (1,)), ((), ())),
                            preferred_element_type=F32)
        s_ref[...] = s
        mv_ref[...] = functools.reduce(
            jnp.maximum, [s[:, j * LANES:(j + 1) * LANES] for j in range(seq // LANES)])

    def row_max(mv_ref, m_ref):
        m_ref[...] = jnp.max(mv_ref[...], axis=-1, keepdims=True)

    def output(t, s_ref, m_ref, oe_ref):
        g, _ = tile_pos(t)
        p = jnp.exp2(s_ref[...] - m_ref[...]).astype(BF16)
        oe_ref[...] = jnp.dot(p, vext_ref[0, g], preferred_element_type=F32)

    def finalize(t, oe_ref):
        g, r0 = tile_pos(t)
        low = lane_o < HEAD_DIM
        for j in range(Q_PER_KV // 2):
            even = oe_ref[2 * j * Q_TILE:(2 * j + 1) * Q_TILE, :]
            odd = oe_ref[(2 * j + 1) * Q_TILE:(2 * j + 2) * Q_TILE, :]
            even_sw = pltpu.roll(even, HEAD_DIM, axis=1)
            odd_sw = pltpu.roll(odd, HEAD_DIM, axis=1)
            pair = jnp.where(low, even, odd_sw) / jnp.where(low, even_sw, odd)
            o_ref[0, g, pl.ds(r0, Q_TILE), j * LANES:(j + 1) * LANES] = pair.astype(BF16)

    n_buf = 3
    s_refs, mv_refs = (s0_ref, s1_ref, s2_ref), (mv0_ref, mv1_ref, mv2_ref)
    m_refs, oe_refs = (m0_ref, m1_ref, m2_ref), (oe0_ref, oe1_ref, oe2_ref)

    def slot(t, k, first=False):
        if t_plus(t, 1) is not None:
            row_max(mv_refs[(k + 1) % n_buf], m_refs[(k + 1) % n_buf])
        if not first:
            finalize(t - 1, oe_refs[(k - 1) % n_buf])
        if t_plus(t, 2) is not None:
            scores(t + 2, s_refs[(k + 2) % n_buf], mv_refs[(k + 2) % n_buf])
        output(t, s_refs[k % n_buf], m_refs[k % n_buf], oe_refs[k % n_buf])

    def t_plus(t, d):
        return t + d if not isinstance(t, int) or t + d < n_tiles else None

    per_body = n_buf * ATTN_ROTATIONS_PER_STEP
    n_loop = (n_tiles - 3) // per_body * per_body
    assert n_loop > 0

    scores(0, s_refs[0], mv_refs[0])
    scores(1, s_refs[1], mv_refs[1])
    row_max(mv_refs[0], m_refs[0])
    slot(0, 0, first=True)

    def body(jj, carry):
        for i in range(per_body):
            slot(1 + jj * per_body + i, (1 + i) % n_buf)
        return carry

    lax.fori_loop(0, n_loop // per_body, body, 0)

    for t in range(1 + n_loop, n_tiles):
        slot(t, t % n_buf)
    finalize(n_tiles - 1, oe_refs[(n_tiles - 1) % n_buf])


def _mix_ffn_kernel(x_ref, a_ref, u_ref, uprev_ref, unext_ref, pw_ref, pb_ref, ps_ref,
                    wout_ref, g2_ref, wg_ref, wu_ref, wd_ref, o_ref,
                    ext_ref, pooled_ref, mixp_ref, x1_ref, h_ref, *, seq, d_ff, n_tiles):
    rows = x_ref.shape[0]
    tiles_per_seq = seq // rows
    step = pl.program_id(0)

    st = jnp.minimum(step, n_tiles - 1) % tiles_per_seq

    def stage_windows():
        ext_ref[0:POOL_HALO, :] = jnp.where(st > 0, uprev_ref[...], 0.0)
        ext_ref[POOL_HALO:POOL_HALO + rows, :] = u_ref[...]
        ext_ref[POOL_HALO + rows:, :] = jnp.where(st < tiles_per_seq - 1, unext_ref[...], 0.0)

    def group_cols(g):
        return slice(g * POOL_GROUP_DIM, (g + 1) * POOL_GROUP_DIM)

    def pool_windows(g):
        n_ext = rows + 2 * POOL_HALO
        t = st * rows + lax.broadcasted_iota(jnp.int32, (rows, LANES), 0)
        w = POOL_WINDOWS[g]
        cs = group_cols(g)
        ext = ext_ref[:, cs]
        b = ext
        k = 1
        while k < w:
            b = b + pltpu.roll(b, n_ext - k, axis=0)
            k *= 2
        win = pltpu.roll(b, w // 2, axis=0)[POOL_HALO:POOL_HALO + rows]
        lo = jnp.maximum(t - w // 2, 0)
        hi = jnp.minimum(t - w // 2 + w, seq)
        cnt = (hi - lo).astype(F32)
        pooled_ref[:, cs] = (win / cnt - ext[POOL_HALO:POOL_HALO + rows]).astype(BF16)

    def pool_dot(g):
        cs = group_cols(g)
        y = jnp.dot(pooled_ref[:, cs], pw_ref[g].astype(BF16),
                    preferred_element_type=F32) + pb_ref[g]
        mixp_ref[:, cs] = (y * ps_ref[:, cs]).astype(BF16)

    def project_out():
        mix = jnp.concatenate([a_ref[0, g] for g in range(N_KV_HEADS)] + [mixp_ref[...]],
                              axis=1)
        x1 = x_ref[...] + jnp.dot(mix, wout_ref[...], preferred_element_type=F32)
        ms = jnp.mean(x1 * x1, axis=-1, keepdims=True)
        x1_ref[...] = x1
        return (x1 * lax.rsqrt(ms + EPS) * g2_ref[...]).astype(BF16)

    bounds = [(c0, min(c0 + FF_CHUNK, d_ff)) for c0 in range(0, d_ff, FF_CHUNK)]

    @pl.when(step == 0)
    def _():
        stage_windows()
        for g in range(N_POOL_GROUPS):
            pool_windows(g)
            pool_dot(g)
        h_ref[...] = project_out()

    @pl.when(step > 0)
    def _():
        stage_windows()
        h = h_ref[...]
        acc = x1_ref[...]
        h_next = []
        between = [[functools.partial(pool_windows, g), functools.partial(pool_dot, g)]
                   for g in range(N_POOL_GROUPS)]
        between += [[lambda: h_next.append(project_out())], []]
        assert len(between) == len(bounds)
        for c, (c0, c1) in enumerate(bounds):
            gate = jnp.dot(h, wg_ref[:, c0:c1], preferred_element_type=F32)
            up = jnp.dot(h, wu_ref[:, c0:c1], preferred_element_type=F32)
            if c == len(bounds) - 1:
                h_ref[...] = h_next[0]
            act = (gate * jax.nn.sigmoid(gate) * up).astype(BF16)
            acc = acc + jnp.dot(act, wd_ref[c0:c1, :], preferred_element_type=F32)
            for piece in between[c]:
                piece()
        o_ref[...] = acc


def _const_spec(shape):
    nd = len(shape)
    return pl.BlockSpec(shape, lambda *_: (0,) * nd)


def _layer(x2, seq, norm1_g, w_in, q_norm_g, k_norm_g, pool_w, pool_b, pool_scale,
           w_out, norm2_g, w_gate, w_up, w_down):
    n_tok = x2.shape[0]
    batch = n_tok // seq
    d_ff = w_gate.shape[1]
    assert seq % ROW_TILE == 0 and seq % Q_TILE == 0 and ROW_TILE % POOL_HALO == 0
    assert seq % INPROJ_ROWS == 0
    n_row_tiles = n_tok // ROW_TILE
    n_in_tiles = n_tok // INPROJ_ROWS

    cos, sin = _rope_tables(seq)
    seg = jnp.asarray(_segment_ones(256), BF16)
    q_scale = HEAD_DIM ** -0.5 * math.log2(math.e)

    params = pltpu.CompilerParams(dimension_semantics=("arbitrary",),
                                  vmem_limit_bytes=VMEM_LIMIT_BYTES)

    def grouped_rows(width, rows, tile):
        per_seq = seq // rows
        return pl.BlockSpec((1, N_KV_HEADS, rows, width),
                            lambda i: (tile(i) // per_seq, 0, tile(i) % per_seq, 0))

    def lead_tile(n):
        return lambda i: jnp.minimum(i, n - 1)

    mix_tile = lead_tile(n_row_tiles)
    in_tile = lead_tile(n_in_tiles)

    def lag_tile(i):
        return jnp.maximum(i - 1, 0)

    def grouped_seq(width):
        return pl.BlockSpec((1, N_KV_HEADS, seq, width), lambda b: (b, 0, 0, 0))

    q, krep, vext, u = pl.pallas_call(
        functools.partial(_inproj_kernel, q_scale=q_scale, n_tiles=n_in_tiles, seq=seq),
        grid=(n_in_tiles + 1,),
        in_specs=[
            pl.BlockSpec((INPROJ_ROWS, D_MODEL), lambda i: (in_tile(i), 0)),
            _const_spec((1, D_MODEL)),
            _const_spec((D_MODEL, IN_WIDTH)),
            _const_spec((1, HEAD_DIM)),
            _const_spec((1, HEAD_DIM)),
            _const_spec((seq, LANES)),
            _const_spec((seq, LANES)),
            _const_spec((256, 256)),
        ],
        out_specs=[
            grouped_rows(GROUP_WIDTH, INPROJ_ROWS, lag_tile),
            grouped_rows(GROUP_WIDTH, INPROJ_ROWS, lag_tile),
            grouped_rows(LANES, INPROJ_ROWS, lag_tile),
            pl.BlockSpec((INPROJ_ROWS, POOL_WIDTH), lambda i: (lag_tile(i), 0)),
        ],
        out_shape=[
            jax.ShapeDtypeStruct((batch, N_KV_HEADS, seq, GROUP_WIDTH), BF16),
            jax.ShapeDtypeStruct((batch, N_KV_HEADS, seq, GROUP_WIDTH), BF16),
            jax.ShapeDtypeStruct((batch, N_KV_HEADS, seq, LANES), BF16),
            jax.ShapeDtypeStruct((n_tok, POOL_WIDTH), F32),
        ],
        scratch_shapes=[pltpu.VMEM((INPROJ_ROWS, IN_WIDTH), F32),
                        pltpu.VMEM((INPROJ_ROWS, IN_WIDTH), F32),
                        pltpu.VMEM((D_MODEL, IN_WIDTH), BF16)],
        compiler_params=params,
        name="inproj",
    )(x2, norm1_g.astype(F32)[None, :], w_in.astype(F32),
      q_norm_g.astype(F32)[None, :], k_norm_g.astype(F32)[None, :],
      jnp.asarray(cos), jnp.asarray(sin), seg)

    late_weights = [w_out, w_gate, w_up, w_down]
    for w in late_weights:
        assert w.shape[0] % (batch * BF16_SUBLANES) == 0

    def row_slab(w):
        return pl.BlockSpec((w.shape[0] // batch, w.shape[1]), lambda b: (b, 0))

    attn, w_out_b, w_gate_b, w_up_b, w_down_b = pl.pallas_call(
        functools.partial(_attention_kernel, seq=seq),
        grid=(batch,),
        in_specs=([grouped_seq(GROUP_WIDTH), grouped_seq(GROUP_WIDTH), grouped_seq(LANES)]
                  + [row_slab(w) for w in late_weights]),
        out_specs=[grouped_seq(GROUP_WIDTH)] + [row_slab(w) for w in late_weights],
        out_shape=([jax.ShapeDtypeStruct((batch, N_KV_HEADS, seq, GROUP_WIDTH), BF16)]
                   + [jax.ShapeDtypeStruct(w.shape, BF16) for w in late_weights]),
        scratch_shapes=([pltpu.VMEM((Q_PER_KV * Q_TILE, seq), F32)] * 3
                        + [pltpu.VMEM((Q_PER_KV * Q_TILE, LANES), F32)] * 3
                        + [pltpu.VMEM((Q_PER_KV * Q_TILE, 1), F32)] * 3
                        + [pltpu.VMEM((Q_PER_KV * Q_TILE, LANES), F32)] * 3),
        compiler_params=params,
        name="attention",
    )(q, krep, vext, *[w.astype(F32) for w in late_weights])

    halo_blocks = ROW_TILE // POOL_HALO
    last_halo = n_tok // POOL_HALO - 1
    out = pl.pallas_call(
        functools.partial(_mix_ffn_kernel, seq=seq, d_ff=d_ff, n_tiles=n_row_tiles),
        grid=(n_row_tiles + 1,),
        in_specs=[
            pl.BlockSpec((ROW_TILE, D_MODEL), lambda i: (mix_tile(i), 0)),
            grouped_rows(GROUP_WIDTH, ROW_TILE, mix_tile),
            pl.BlockSpec((ROW_TILE, POOL_WIDTH), lambda i: (mix_tile(i), 0)),
            pl.BlockSpec((POOL_HALO, POOL_WIDTH),
                         lambda i: (jnp.maximum(mix_tile(i) * halo_blocks - 1, 0), 0)),
            pl.BlockSpec((POOL_HALO, POOL_WIDTH),
                         lambda i: (jnp.minimum((mix_tile(i) + 1) * halo_blocks, last_halo), 0)),
            _const_spec((N_POOL_GROUPS, POOL_GROUP_DIM, POOL_GROUP_DIM)),
            _const_spec((N_POOL_GROUPS, 1, POOL_GROUP_DIM)),
            _const_spec((1, POOL_WIDTH)),
            _const_spec((D_MODEL, D_MODEL)),
            _const_spec((1, D_MODEL)),
            _const_spec((D_MODEL, d_ff)),
            _const_spec((D_MODEL, d_ff)),
            _const_spec((d_ff, D_MODEL)),
        ],
        out_specs=pl.BlockSpec((ROW_TILE, D_MODEL), lambda i: (jnp.maximum(i - 1, 0), 0)),
        out_shape=jax.ShapeDtypeStruct((n_tok, D_MODEL), F32),
        scratch_shapes=[pltpu.VMEM((ROW_TILE + 2 * POOL_HALO, POOL_WIDTH), F32),
                        pltpu.VMEM((ROW_TILE, POOL_WIDTH), BF16),
                        pltpu.VMEM((ROW_TILE, POOL_WIDTH), BF16),
                        pltpu.VMEM((ROW_TILE, D_MODEL), F32), pltpu.VMEM((ROW_TILE, D_MODEL), BF16)],
        compiler_params=params,
        name="mix_ffn",
    )(x2, attn, u, u, u, pool_w.astype(F32), pool_b.astype(F32)[:, None, :],
      pool_scale.astype(F32)[None, :], w_out_b, norm2_g.astype(F32)[None, :],
      w_gate_b, w_up_b, w_down_b)
    return out


def kernel(x, norm1_g, w_in, q_norm_g, k_norm_g, pool_w, pool_b, pool_scale,
           w_out, norm2_g, w_gate, w_up, w_down):
    batch, seq, d_model = x.shape
    assert d_model == D_MODEL and w_in.shape[-1] == IN_WIDTH
    x2 = x.reshape(batch * seq, d_model)
    for l in range(norm1_g.shape[0]):
        x2 = _layer(x2, seq, norm1_g[l], w_in[l], q_norm_g[l], k_norm_g[l], pool_w[l],
                    pool_b[l], pool_scale[l], w_out[l], norm2_g[l], w_gate[l], w_up[l],
                    w_down[l])
    return x2.reshape(batch, seq, d_model)
```

```python
import functools
import math

import jax
import jax.numpy as jnp
import numpy as np
from jax import lax
from jax.experimental import pallas as pl
from jax.experimental.pallas import tpu as pltpu

D_MODEL = 1024
HEAD_DIM = 64
N_Q_HEADS = 8
N_KV_HEADS = 2
Q_PER_KV = N_Q_HEADS // N_KV_HEADS
ATTN_WIDTH = N_Q_HEADS * HEAD_DIM
KV_WIDTH = N_KV_HEADS * HEAD_DIM
GROUP_WIDTH = Q_PER_KV * HEAD_DIM
ROPE_PAIRS = HEAD_DIM // 4
ROPE_THETA = 10000.0
POOL_WINDOWS = (2, 4, 8, 16)
N_POOL_GROUPS = len(POOL_WINDOWS)
POOL_WIDTH = D_MODEL - ATTN_WIDTH
POOL_GROUP_DIM = POOL_WIDTH // N_POOL_GROUPS
IN_WIDTH = ATTN_WIDTH + 2 * KV_WIDTH + POOL_WIDTH
GRID_W = 64
EPS = 1e-6

LANES = 128
BF16_SUBLANES = 16
POOL_HALO = 8
VMEM_LIMIT_BYTES = 56 * 1024 * 1024

ROW_TILE = 512
INPROJ_ROWS = 1024
Q_TILE = 128
ATTN_ROTATIONS_PER_STEP = 3
FF_CHUNK = 512
INPROJ_CHUNK = 256

F32 = jnp.float32
BF16 = jnp.bfloat16


def _lane_iota(shape):
    return lax.broadcasted_iota(jnp.int32, shape, len(shape) - 1)


def _rope_tables(seq):
    t = np.arange(seq)
    row = (t // GRID_W).astype(np.float64)
    col = (t % GRID_W).astype(np.float64)
    inv_freq = ROPE_THETA ** (-np.arange(ROPE_PAIRS, dtype=np.float64) / ROPE_PAIRS)
    ang_row = row[:, None] * inv_freq[None, :]
    ang_col = col[:, None] * inv_freq[None, :]
    ang = np.concatenate([ang_row, ang_row, ang_col, ang_col], axis=1)
    sign = np.concatenate([-np.ones(ROPE_PAIRS), np.ones(ROPE_PAIRS)] * 2)
    cos = np.tile(np.cos(ang), (1, 2)).astype(np.float32)
    sin = np.tile(np.sin(ang) * sign[None, :], (1, 2)).astype(np.float32)
    return cos, sin


def _segment_ones(width):
    idx = np.arange(width) // HEAD_DIM
    return (idx[:, None] == idx[None, :]).astype(np.float32)


def _head_sumsq(x, seg):
    sq = x * x
    hi = sq.astype(BF16)
    lo = (sq - hi.astype(F32)).astype(BF16)
    return (jnp.dot(hi, seg, preferred_element_type=F32)
            + jnp.dot(lo, seg, preferred_element_type=F32))


def _rope_partner(x):
    lane = _lane_iota(x.shape)
    fwd = pltpu.roll(x, LANES - ROPE_PAIRS, axis=1)
    bwd = pltpu.roll(x, ROPE_PAIRS, axis=1)
    return jnp.where((lane & ROPE_PAIRS) == 0, fwd, bwd)


def _norm_rope(x, sumsq, gain, cos, sin):
    y = x * lax.rsqrt(sumsq * (1.0 / HEAD_DIM) + EPS) * gain
    return y * cos + _rope_partner(y) * sin


def _inproj_kernel(x_ref, g1_ref, win_ref, qg_ref, kg_ref, cos_ref, sin_ref, seg_ref,
                   q_ref, krep_ref, vext_ref, u_ref, proja_ref, projb_ref, wbf_ref,
                   *, q_scale, n_tiles, seq):
    step = pl.program_id(0)
    rows = x_ref.shape[0]
    kv0 = ATTN_WIDTH
    u0 = ATTN_WIDTH + 2 * KV_WIDTH
    pos0 = pl.multiple_of((jnp.maximum(step - 1, 0) % (seq // rows)) * rows, rows)

    def rope_tables():
        return cos_ref[pl.ds(pos0, rows), :], sin_ref[pl.ds(pos0, rows), :]

    def head_pair_gain(g_ref):
        g = g_ref[...]
        return jnp.concatenate([g] * (LANES // HEAD_DIM), axis=1)

    def normed_input():
        x = x_ref[...]
        ms = jnp.mean(x * x, axis=-1, keepdims=True)
        return (x * lax.rsqrt(ms + EPS) * g1_ref[...]).astype(BF16)

    def finish_q(src_ref, g):
        qh = src_ref[:, g * GROUP_WIDTH:(g + 1) * GROUP_WIDTH]
        ss = _head_sumsq(qh, seg_ref[...])
        for j in range(GROUP_WIDTH // LANES):
            sl = slice(j * LANES, (j + 1) * LANES)
            qr = _norm_rope(qh[:, sl], ss[:, sl], head_pair_gain(qg_ref), *rope_tables()) * q_scale
            q_ref[0, g, :, sl] = qr.astype(BF16)

    def finish_kv(src_ref):
        k = src_ref[:, kv0:kv0 + KV_WIDTH]
        kss = _head_sumsq(k, seg_ref[:KV_WIDTH, :KV_WIDTH])
        kr = _norm_rope(k, kss, head_pair_gain(kg_ref), *rope_tables())
        first = _lane_iota(kr.shape) < HEAD_DIM
        kswap = pltpu.roll(kr, HEAD_DIM, axis=1)
        k00 = jnp.where(first, kr, kswap).astype(BF16)
        k11 = jnp.where(first, kswap, kr).astype(BF16)
        for j in range(GROUP_WIDTH // LANES):
            krep_ref[0, 0, :, j * LANES:(j + 1) * LANES] = k00
            krep_ref[0, 1, :, j * LANES:(j + 1) * LANES] = k11
        v = src_ref[:, kv0 + KV_WIDTH:kv0 + 2 * KV_WIDTH]
        vswap = pltpu.roll(v, HEAD_DIM, axis=1)
        vext_ref[0, 0] = jnp.where(first, v, 1.0).astype(BF16)
        vext_ref[0, 1] = jnp.where(first, vswap, 1.0).astype(BF16)

    def finish_u(src_ref, half):
        cs = slice(half * (POOL_WIDTH // 2), (half + 1) * (POOL_WIDTH // 2))
        u_ref[:, cs] = src_ref[:, u0 + cs.start:u0 + cs.stop]

    def finish_pieces(src_ref):
        return [functools.partial(finish_q, src_ref, 0), functools.partial(finish_q, src_ref, 1),
                functools.partial(finish_kv, src_ref),
                functools.partial(finish_u, src_ref, 0), functools.partial(finish_u, src_ref, 1)]

    def project(dst_ref, between=()):
        h = normed_input()
        for c in range(IN_WIDTH // INPROJ_CHUNK):
            cs = slice(c * INPROJ_CHUNK, (c + 1) * INPROJ_CHUNK)
            dst_ref[:, cs] = jnp.dot(h, wbf_ref[:, cs], preferred_element_type=F32)
            if c < len(between):
                between[c]()

    even = step % 2 == 0
    assert n_tiles % 2 == 0

    @pl.when(step == 0)
    def _():
        wbf_ref[...] = win_ref[...].astype(BF16)
        project(proja_ref)

    @pl.when(jnp.logical_and(even, jnp.logical_and(step > 0, step < n_tiles)))
    def _():
        project(proja_ref, finish_pieces(projb_ref))

    @pl.when(jnp.logical_not(even))
    def _():
        project(projb_ref, finish_pieces(proja_ref))

    @pl.when(step == n_tiles)
    def _():
        for piece in finish_pieces(projb_ref):
            piece()


def _attention_kernel(q_ref, krep_ref, vext_ref, wf0_ref, wf1_ref, wf2_ref, wf3_ref,
                      o_ref, wb0_ref, wb1_ref, wb2_ref, wb3_ref,
                      s0_ref, s1_ref, s2_ref, mv0_ref, mv1_ref, mv2_ref,
                      m0_ref, m1_ref, m2_ref, oe0_ref, oe1_ref, oe2_ref, *, seq):
    for wf_ref, wb_ref in ((wf0_ref, wb0_ref), (wf1_ref, wb1_ref),
                           (wf2_ref, wb2_ref), (wf3_ref, wb3_ref)):
        wb_ref[...] = wf_ref[...].astype(BF16)

    tiles_per_group = seq // Q_TILE
    n_tiles = N_KV_HEADS * tiles_per_group
    lane_q = _lane_iota((Q_TILE, GROUP_WIDTH))
    lane_o = _lane_iota((Q_TILE, LANES))

    def tile_pos(t):
        g = t // tiles_per_group
        r0 = pl.multiple_of((t % tiles_per_group) * Q_TILE, Q_TILE)
        return g, r0

    def scores(t, s_ref, mv_ref):
        g, r0 = tile_pos(t)
        qt = q_ref[0, g, pl.ds(r0, Q_TILE), :]
        zero = jnp.zeros_like(qt)
        qm = jnp.concatenate(
            [jnp.where((lane_q // HEAD_DIM) == hh, qt, zero) for hh in range(Q_PER_KV)], axis=0)
        s = lax.dot_general(qm, krep_ref[0, g], (((1,), (1,)), ((), ())),
                            preferred_element_type=F32)
        s_ref[...] = s
        mv_ref[...] = functools.reduce(
            jnp.maximum, [s[:, j * LANES:(j + 1) * LANES] for j in range(seq // LANES)])

    def row_max(mv_ref, m_ref):
        m_ref[...] = jnp.max(mv_ref[...], axis=-1, keepdims=True)

    def output(t, s_ref, m_ref, oe_ref):
        g, _ = tile_pos(t)
        p = jnp.exp2(s_ref[...] - m_ref[...]).astype(BF16)
        oe_ref[...] = jnp.dot(p, vext_ref[0, g], preferred_element_type=F32)

    def finalize(t, oe_ref):
        g, r0 = tile_pos(t)
        low = lane_o < HEAD_DIM
        for j in range(Q_PER_KV // 2):
            even = oe_ref[2 * j * Q_TILE:(2 * j + 1) * Q_TILE, :]
            odd = oe_ref[(2 * j + 1) * Q_TILE:(2 * j + 2) * Q_TILE, :]
            even_sw = pltpu.roll(even, HEAD_DIM, axis=1)
            odd_sw = pltpu.roll(odd, HEAD_DIM, axis=1)
            pair = jnp.where(low, even, odd_sw) / jnp.where(low, even_sw, odd)
            o_ref[0, g, pl.ds(r0, Q_TILE), j * LANES:(j + 1) * LANES] = pair.astype(BF16)

    n_buf = 3
    s_refs, mv_refs = (s0_ref, s1_ref, s2_ref), (mv0_ref, mv1_ref, mv2_ref)
    m_refs, oe_refs = (m0_ref, m1_ref, m2_ref), (oe0_ref, oe1_ref, oe2_ref)

    def slot(t, k, first=False):
        if t_plus(t, 1) is not None:
            row_max(mv_refs[(k + 1) % n_buf], m_refs[(k + 1) % n_buf])
        if not first:
            finalize(t - 1, oe_refs[(k - 1) % n_buf])
        if t_plus(t, 2) is not None:
            scores(t + 2, s_refs[(k + 2) % n_buf], mv_refs[(k + 2) % n_buf])
        output(t, s_refs[k % n_buf], m_refs[k % n_buf], oe_refs[k % n_buf])

    def t_plus(t, d):
        return t + d if not isinstance(t, int) or t + d < n_tiles else None

    per_body = n_buf * ATTN_ROTATIONS_PER_STEP
    n_loop = (n_tiles - 3) // per_body * per_body
    assert n_loop > 0

    scores(0, s_refs[0], mv_refs[0])
    scores(1, s_refs[1], mv_refs[1])
    row_max(mv_refs[0], m_refs[0])
    slot(0, 0, first=True)

    def body(jj, carry):
        for i in range(per_body):
            slot(1 + jj * per_body + i, (1 + i) % n_buf)
        return carry

    lax.fori_loop(0, n_loop // per_body, body, 0)

    for t in range(1 + n_loop, n_tiles):
        slot(t, t % n_buf)
    finalize(n_tiles - 1, oe_refs[(n_tiles - 1) % n_buf])


def _mix_ffn_kernel(x_ref, a_ref, u_ref, uprev_ref, unext_ref, pw_ref, pb_ref, ps_ref,
                    wout_ref, g2_ref, wg_ref, wu_ref, wd_ref, o_ref,
                    ext_ref, mixp_ref, x1_ref, h_ref, *, seq, d_ff, n_tiles):
    rows = x_ref.shape[0]
    tiles_per_seq = seq // rows
    step = pl.program_id(0)

    st = jnp.minimum(step, n_tiles - 1) % tiles_per_seq

    def stage_windows():
        ext_ref[0:POOL_HALO, :] = jnp.where(st > 0, uprev_ref[...], 0.0)
        ext_ref[POOL_HALO:POOL_HALO + rows, :] = u_ref[...]
        ext_ref[POOL_HALO + rows:, :] = jnp.where(st < tiles_per_seq - 1, unext_ref[...], 0.0)

    def pool_group(g):
        n_ext = rows + 2 * POOL_HALO
        w = POOL_WINDOWS[g]
        assert w // 2 <= POOL_HALO
        cs = slice(g * POOL_GROUP_DIM, (g + 1) * POOL_GROUP_DIM)
        ext = ext_ref[:, cs]
        f = ext
        k = 1
        while k < w // 2:
            f = f + pltpu.roll(f, n_ext - k, axis=0)
            k *= 2
        both = f + pltpu.roll(f, w // 2, axis=0)
        win = both[POOL_HALO:POOL_HALO + rows]

        def edge_mean(r0):
            t = st * rows + r0 + lax.broadcasted_iota(jnp.int32, (POOL_HALO, LANES), 0)
            lo = jnp.maximum(t - w // 2, 0)
            hi = jnp.minimum(t - w // 2 + w, seq)
            return win[r0:r0 + POOL_HALO] / (hi - lo).astype(F32)

        mean = jnp.concatenate([edge_mean(0), win[POOL_HALO:rows - POOL_HALO] * (1.0 / w),
                                edge_mean(rows - POOL_HALO)], axis=0)
        pooled = (mean - ext[POOL_HALO:POOL_HALO + rows]).astype(BF16)
        y = jnp.dot(pooled, pw_ref[g].astype(BF16), preferred_element_type=F32) + pb_ref[g]
        mixp_ref[:, cs] = (y * ps_ref[:, cs]).astype(BF16)

    def project_out():
        mix = jnp.concatenate([a_ref[0, g] for g in range(N_KV_HEADS)] + [mixp_ref[...]],
                              axis=1)
        x1 = x_ref[...] + jnp.dot(mix, wout_ref[...], preferred_element_type=F32)
        ms = jnp.mean(x1 * x1, axis=-1, keepdims=True)
        x1_ref[...] = x1
        return (x1 * lax.rsqrt(ms + EPS) * g2_ref[...]).astype(BF16)

    bounds = [(c0, min(c0 + FF_CHUNK, d_ff)) for c0 in range(0, d_ff, FF_CHUNK)]

    @pl.when(step == 0)
    def _():
        stage_windows()
        for g in range(N_POOL_GROUPS):
            pool_group(g)
        h_ref[...] = project_out()

    @pl.when(step > 0)
    def _():
        stage_windows()
        h = h_ref[...]
        acc = x1_ref[...]
        h_next = []
        between = [[functools.partial(pool_group, g)] for g in range(N_POOL_GROUPS)]
        between += [[lambda: h_next.append(project_out())], []]
        assert len(between) == len(bounds)
        for c, (c0, c1) in enumerate(bounds):
            gate = jnp.dot(h, wg_ref[:, c0:c1], preferred_element_type=F32)
            up = jnp.dot(h, wu_ref[:, c0:c1], preferred_element_type=F32)
            if c == len(bounds) - 1:
                h_ref[...] = h_next[0]
            act = (gate * jax.nn.sigmoid(gate) * up).astype(BF16)
            acc = acc + jnp.dot(act, wd_ref[c0:c1, :], preferred_element_type=F32)
            for piece in between[c]:
                piece()
        o_ref[...] = acc


def _const_spec(shape):
    nd = len(shape)
    return pl.BlockSpec(shape, lambda *_: (0,) * nd)


def _layer(x2, seq, norm1_g, w_in, q_norm_g, k_norm_g, pool_w, pool_b, pool_scale,
           w_out, norm2_g, w_gate, w_up, w_down):
    n_tok = x2.shape[0]
    batch = n_tok // seq
    d_ff = w_gate.shape[1]
    assert seq % ROW_TILE == 0 and seq % Q_TILE == 0 and ROW_TILE % POOL_HALO == 0
    assert seq % INPROJ_ROWS == 0
    n_row_tiles = n_tok // ROW_TILE
    n_in_tiles = n_tok // INPROJ_ROWS

    cos, sin = _rope_tables(seq)
    seg = jnp.asarray(_segment_ones(256), BF16)
    q_scale = HEAD_DIM ** -0.5 * math.log2(math.e)

    params = pltpu.CompilerParams(dimension_semantics=("arbitrary",),
                                  vmem_limit_bytes=VMEM_LIMIT_BYTES)

    def grouped_rows(width, rows, tile):
        per_seq = seq // rows
        return pl.BlockSpec((1, N_KV_HEADS, rows, width),
                            lambda i: (tile(i) // per_seq, 0, tile(i) % per_seq, 0))

    def lead_tile(n):
        return lambda i: jnp.minimum(i, n - 1)

    mix_tile = lead_tile(n_row_tiles)
    in_tile = lead_tile(n_in_tiles)

    def lag_tile(i):
        return jnp.maximum(i - 1, 0)

    def grouped_seq(width):
        return pl.BlockSpec((1, N_KV_HEADS, seq, width), lambda b: (b, 0, 0, 0))

    q, krep, vext, u = pl.pallas_call(
        functools.partial(_inproj_kernel, q_scale=q_scale, n_tiles=n_in_tiles, seq=seq),
        grid=(n_in_tiles + 1,),
        in_specs=[
            pl.BlockSpec((INPROJ_ROWS, D_MODEL), lambda i: (in_tile(i), 0)),
            _const_spec((1, D_MODEL)),
            _const_spec((D_MODEL, IN_WIDTH)),
            _const_spec((1, HEAD_DIM)),
            _const_spec((1, HEAD_DIM)),
            _const_spec((seq, LANES)),
            _const_spec((seq, LANES)),
            _const_spec((256, 256)),
        ],
        out_specs=[
            grouped_rows(GROUP_WIDTH, INPROJ_ROWS, lag_tile),
            grouped_rows(GROUP_WIDTH, INPROJ_ROWS, lag_tile),
            grouped_rows(LANES, INPROJ_ROWS, lag_tile),
            pl.BlockSpec((INPROJ_ROWS, POOL_WIDTH), lambda i: (lag_tile(i), 0)),
        ],
        out_shape=[
            jax.ShapeDtypeStruct((batch, N_KV_HEADS, seq, GROUP_WIDTH), BF16),
            jax.ShapeDtypeStruct((batch, N_KV_HEADS, seq, GROUP_WIDTH), BF16),
            jax.ShapeDtypeStruct((batch, N_KV_HEADS, seq, LANES), BF16),
            jax.ShapeDtypeStruct((n_tok, POOL_WIDTH), F32),
        ],
        scratch_shapes=[pltpu.VMEM((INPROJ_ROWS, IN_WIDTH), F32),
                        pltpu.VMEM((INPROJ_ROWS, IN_WIDTH), F32),
                        pltpu.VMEM((D_MODEL, IN_WIDTH), BF16)],
        compiler_params=params,
        name="inproj",
    )(x2, norm1_g.astype(F32)[None, :], w_in.astype(F32),
      q_norm_g.astype(F32)[None, :], k_norm_g.astype(F32)[None, :],
      jnp.asarray(cos), jnp.asarray(sin), seg)

    late_weights = [w_out, w_gate, w_up, w_down]
    for w in late_weights:
        assert w.shape[0] % (batch * BF16_SUBLANES) == 0

    def row_slab(w):
        return pl.BlockSpec((w.shape[0] // batch, w.shape[1]), lambda b: (b, 0))

    attn, w_out_b, w_gate_b, w_up_b, w_down_b = pl.pallas_call(
        functools.partial(_attention_kernel, seq=seq),
        grid=(batch,),
        in_specs=([grouped_seq(GROUP_WIDTH), grouped_seq(GROUP_WIDTH), grouped_seq(LANES)]
                  + [row_slab(w) for w in late_weights]),
        out_specs=[grouped_seq(GROUP_WIDTH)] + [row_slab(w) for w in late_weights],
        out_shape=([jax.ShapeDtypeStruct((batch, N_KV_HEADS, seq, GROUP_WIDTH), BF16)]
                   + [jax.ShapeDtypeStruct(w.shape, BF16) for w in late_weights]),
        scratch_shapes=([pltpu.VMEM((Q_PER_KV * Q_TILE, seq), F32)] * 3
                        + [pltpu.VMEM((Q_PER_KV * Q_TILE, LANES), F32)] * 3
                        + [pltpu.VMEM((Q_PER_KV * Q_TILE, 1), F32)] * 3
                        + [pltpu.VMEM((Q_PER_KV * Q_TILE, LANES), F32)] * 3),
        compiler_params=params,
        name="attention",
    )(q, krep, vext, *[w.astype(F32) for w in late_weights])

    halo_blocks = ROW_TILE // POOL_HALO
    last_halo = n_tok // POOL_HALO - 1
    out = pl.pallas_call(
        functools.partial(_mix_ffn_kernel, seq=seq, d_ff=d_ff, n_tiles=n_row_tiles),
        grid=(n_row_tiles + 1,),
        in_specs=[
            pl.BlockSpec((ROW_TILE, D_MODEL), lambda i: (mix_tile(i), 0)),
            grouped_rows(GROUP_WIDTH, ROW_TILE, mix_tile),
            pl.BlockSpec((ROW_TILE, POOL_WIDTH), lambda i: (mix_tile(i), 0)),
            pl.BlockSpec((POOL_HALO, POOL_WIDTH),
                         lambda i: (jnp.maximum(mix_tile(i) * halo_blocks - 1, 0), 0)),
            pl.BlockSpec((POOL_HALO, POOL_WIDTH),
                         lambda i: (jnp.minimum((mix_tile(i) + 1) * halo_blocks, last_halo), 0)),
            _const_spec((N_POOL_GROUPS, POOL_GROUP_DIM, POOL_GROUP_DIM)),
            _const_spec((N_POOL_GROUPS, 1, POOL_GROUP_DIM)),
            _const_spec((1, POOL_WIDTH)),
            _const_spec((D_MODEL, D_MODEL)),
            _const_spec((1, D_MODEL)),
            _const_spec((D_MODEL, d_ff)),
            _const_spec((D_MODEL, d_ff)),
            _const_spec((d_ff, D_MODEL)),
        ],
        out_specs=pl.BlockSpec((ROW_TILE, D_MODEL), lambda i: (jnp.maximum(i - 1, 0), 0)),
        out_shape=jax.ShapeDtypeStruct((n_tok, D_MODEL), F32),
        scratch_shapes=[pltpu.VMEM((ROW_TILE + 2 * POOL_HALO, POOL_WIDTH), F32),
                        pltpu.VMEM((ROW_TILE, POOL_WIDTH), BF16),
                        pltpu.VMEM((ROW_TILE, D_MODEL), F32), pltpu.VMEM((ROW_TILE, D_MODEL), BF16)],
        compiler_params=params,
        name="mix_ffn",
    )(x2, attn, u, u, u, pool_w.astype(F32), pool_b.astype(F32)[:, None, :],
      pool_scale.astype(F32)[None, :], w_out_b, norm2_g.astype(F32)[None, :],
      w_gate_b, w_up_b, w_down_b)
    return out


def kernel(x, norm1_g, w_in, q_norm_g, k_norm_g, pool_w, pool_b, pool_scale,
           w_out, norm2_g, w_gate, w_up, w_down):
    batch, seq, d_model = x.shape
    assert d_model == D_MODEL and w_in.shape[-1] == IN_WIDTH
    x2 = x.reshape(batch * seq, d_model)
    for l in range(norm1_g.shape[0]):
        x2 = _layer(x2, seq, norm1_g[l], w_in[l], q_norm_g[l], k_norm_g[l], pool_w[l],
                    pool_b[l], pool_scale[l], w_out[l], norm2_g[l], w_gate[l], w_up[l],
                    w_down[l])
    return x2.reshape(batch, seq, d_model)
```

```python
import functools
import math

import jax
import jax.numpy as jnp
import numpy as np
from jax import lax
from jax.experimental import pallas as pl
from jax.experimental.pallas import tpu as pltpu

D_MODEL = 1024
HEAD_DIM = 64
N_Q_HEADS = 8
N_KV_HEADS = 2
Q_PER_KV = N_Q_HEADS // N_KV_HEADS
ATTN_WIDTH = N_Q_HEADS * HEAD_DIM
KV_WIDTH = N_KV_HEADS * HEAD_DIM
GROUP_WIDTH = Q_PER_KV * HEAD_DIM
ROPE_PAIRS = HEAD_DIM // 4
ROPE_THETA = 10000.0
POOL_WINDOWS = (2, 4, 8, 16)
N_POOL_GROUPS = len(POOL_WINDOWS)
POOL_WIDTH = D_MODEL - ATTN_WIDTH
POOL_GROUP_DIM = POOL_WIDTH // N_POOL_GROUPS
IN_WIDTH = ATTN_WIDTH + 2 * KV_WIDTH + POOL_WIDTH
GRID_W = 64
EPS = 1e-6

LANES = 128
BF16_SUBLANES = 16
POOL_HALO = 8
VMEM_LIMIT_BYTES = 56 * 1024 * 1024

ROW_TILE = 512
INPROJ_ROWS = 1024
Q_TILE = 128
ATTN_SLOTS_PER_STEP = 8
N_SCORE_BUFS = 4
NEXT_TILES = 2
FF_CHUNK = 512
INPROJ_CHUNK = 256

F32 = jnp.float32
BF16 = jnp.bfloat16


def _lane_iota(shape):
    return lax.broadcasted_iota(jnp.int32, shape, len(shape) - 1)


def _rope_tables(seq):
    t = np.arange(seq)
    row = (t // GRID_W).astype(np.float64)
    col = (t % GRID_W).astype(np.float64)
    inv_freq = ROPE_THETA ** (-np.arange(ROPE_PAIRS, dtype=np.float64) / ROPE_PAIRS)
    ang_row = row[:, None] * inv_freq[None, :]
    ang_col = col[:, None] * inv_freq[None, :]
    ang = np.concatenate([ang_row, ang_row, ang_col, ang_col], axis=1)
    sign = np.concatenate([-np.ones(ROPE_PAIRS), np.ones(ROPE_PAIRS)] * 2)
    cos = np.tile(np.cos(ang), (1, 2)).astype(np.float32)
    sin = np.tile(np.sin(ang) * sign[None, :], (1, 2)).astype(np.float32)
    return cos, sin


def _segment_ones(width):
    idx = np.arange(width) // HEAD_DIM
    return (idx[:, None] == idx[None, :]).astype(np.float32)


def _head_sumsq(x, seg):
    sq = x * x
    hi = sq.astype(BF16)
    lo = (sq - hi.astype(F32)).astype(BF16)
    return (jnp.dot(hi, seg, preferred_element_type=F32)
            + jnp.dot(lo, seg, preferred_element_type=F32))


def _rope_partner(x):
    lane = _lane_iota(x.shape)
    fwd = pltpu.roll(x, LANES - ROPE_PAIRS, axis=1)
    bwd = pltpu.roll(x, ROPE_PAIRS, axis=1)
    return jnp.where((lane & ROPE_PAIRS) == 0, fwd, bwd)


def _norm_rope(x, sumsq, gain, cos, sin):
    y = x * lax.rsqrt(sumsq * (1.0 / HEAD_DIM) + EPS) * gain
    return y * cos + _rope_partner(y) * sin


def _inproj_kernel(x_ref, g1_ref, win_ref, qg_ref, kg_ref, cos_ref, sin_ref, seg_ref,
                   q_ref, krep_ref, vext_ref, u_ref, proja_ref, projb_ref, wbf_ref,
                   *, q_scale, n_tiles, seq):
    step = pl.program_id(0)
    rows = x_ref.shape[0]
    kv0 = ATTN_WIDTH
    u0 = ATTN_WIDTH + 2 * KV_WIDTH
    pos0 = pl.multiple_of((jnp.maximum(step - 1, 0) % (seq // rows)) * rows, rows)

    def rope_tables():
        return cos_ref[pl.ds(pos0, rows), :], sin_ref[pl.ds(pos0, rows), :]

    def head_pair_gain(g_ref):
        g = g_ref[...]
        return jnp.concatenate([g] * (LANES // HEAD_DIM), axis=1)

    def normed_input():
        x = x_ref[...]
        ms = jnp.mean(x * x, axis=-1, keepdims=True)
        return (x * lax.rsqrt(ms + EPS) * g1_ref[...]).astype(BF16)

    def finish_q(src_ref, g):
        qh = src_ref[:, g * GROUP_WIDTH:(g + 1) * GROUP_WIDTH]
        ss = _head_sumsq(qh, seg_ref[...])
        for j in range(GROUP_WIDTH // LANES):
            sl = slice(j * LANES, (j + 1) * LANES)
            qr = _norm_rope(qh[:, sl], ss[:, sl], head_pair_gain(qg_ref), *rope_tables()) * q_scale
            q_ref[0, g, :, sl] = qr.astype(BF16)

    def finish_kv(src_ref):
        k = src_ref[:, kv0:kv0 + KV_WIDTH]
        kss = _head_sumsq(k, seg_ref[:KV_WIDTH, :KV_WIDTH])
        kr = _norm_rope(k, kss, head_pair_gain(kg_ref), *rope_tables())
        first = _lane_iota(kr.shape) < HEAD_DIM
        kswap = pltpu.roll(kr, HEAD_DIM, axis=1)
        k00 = jnp.where(first, kr, kswap).astype(BF16)
        k11 = jnp.where(first, kswap, kr).astype(BF16)
        for j in range(GROUP_WIDTH // LANES):
            krep_ref[0, 0, :, j * LANES:(j + 1) * LANES] = k00
            krep_ref[0, 1, :, j * LANES:(j + 1) * LANES] = k11
        v = src_ref[:, kv0 + KV_WIDTH:kv0 + 2 * KV_WIDTH]
        vswap = pltpu.roll(v, HEAD_DIM, axis=1)
        vext_ref[0, 0] = jnp.where(first, v, 1.0).astype(BF16)
        vext_ref[0, 1] = jnp.where(first, vswap, 1.0).astype(BF16)

    def finish_u(src_ref, half):
        cs = slice(half * (POOL_WIDTH // 2), (half + 1) * (POOL_WIDTH // 2))
        u_ref[:, cs] = src_ref[:, u0 + cs.start:u0 + cs.stop]

    def finish_pieces(src_ref):
        return [functools.partial(finish_q, src_ref, 0), functools.partial(finish_q, src_ref, 1),
                functools.partial(finish_kv, src_ref),
                functools.partial(finish_u, src_ref, 0), functools.partial(finish_u, src_ref, 1)]

    def project(dst_ref, between=()):
        h = normed_input()
        for c in range(IN_WIDTH // INPROJ_CHUNK):
            cs = slice(c * INPROJ_CHUNK, (c + 1) * INPROJ_CHUNK)
            dst_ref[:, cs] = jnp.dot(h, wbf_ref[:, cs], preferred_element_type=F32)
            if c < len(between):
                between[c]()

    even = step % 2 == 0
    assert n_tiles % 2 == 0

    @pl.when(step == 0)
    def _():
        wbf_ref[...] = win_ref[...].astype(BF16)
        project(proja_ref)

    @pl.when(jnp.logical_and(even, jnp.logical_and(step > 0, step < n_tiles)))
    def _():
        project(proja_ref, finish_pieces(projb_ref))

    @pl.when(jnp.logical_not(even))
    def _():
        project(projb_ref, finish_pieces(proja_ref))

    @pl.when(step == n_tiles)
    def _():
        for piece in finish_pieces(projb_ref):
            piece()


def _attention_kernel(q_ref, krep_ref, vext_ref, qn_ref, kn_ref,
                      wf0_ref, wf1_ref, wf2_ref, wf3_ref,
                      o_ref, wb0_ref, wb1_ref, wb2_ref, wb3_ref, *scratch, seq):
    for wf_ref, wb_ref in ((wf0_ref, wb0_ref), (wf1_ref, wb1_ref),
                           (wf2_ref, wb2_ref), (wf3_ref, wb3_ref)):
        wb_ref[...] = wf_ref[...].astype(BF16)

    tiles_per_group = seq // Q_TILE
    n_tiles = N_KV_HEADS * tiles_per_group
    lane_q = _lane_iota((Q_TILE, GROUP_WIDTH))
    lane_o = _lane_iota((Q_TILE, LANES))

    def tile_pos(t):
        g = t // tiles_per_group
        r0 = pl.multiple_of((t % tiles_per_group) * Q_TILE, Q_TILE)
        return g, r0

    def scores(t, s_ref, mv_ref):
        if isinstance(t, int) and t >= n_tiles:
            assert t - n_tiles < NEXT_TILES
            qt = qn_ref[0, 0, (t - n_tiles) * Q_TILE:(t - n_tiles + 1) * Q_TILE, :]
            keys = kn_ref[0, 0]
        else:
            g, r0 = tile_pos(t)
            qt = q_ref[0, g, pl.ds(r0, Q_TILE), :]
            keys = krep_ref[0, g]
        zero = jnp.zeros_like(qt)
        qm = jnp.concatenate(
            [jnp.where((lane_q // HEAD_DIM) == hh, qt, zero) for hh in range(Q_PER_KV)], axis=0)
        s = lax.dot_general(qm, keys, (((1,), (1,)), ((), ())),
                            preferred_element_type=F32)
        s_ref[...] = s
        mv_ref[...] = functools.reduce(
            jnp.maximum, [s[:, j * LANES:(j + 1) * LANES] for j in range(seq // LANES)])

    def row_max(mv_ref, m_ref):
        m_ref[...] = jnp.max(mv_ref[...], axis=-1, keepdims=True)

    def output(t, s_ref, m_ref, oe_ref):
        g, _ = tile_pos(t)
        p = jnp.exp2(s_ref[...] - m_ref[...]).astype(BF16)
        oe_ref[...] = jnp.dot(p, vext_ref[0, g], preferred_element_type=F32)

    def finalize(t, oe_ref):
        g, r0 = tile_pos(t)
        low = lane_o < HEAD_DIM
        for j in range(Q_PER_KV // 2):
            even = oe_ref[2 * j * Q_TILE:(2 * j + 1) * Q_TILE, :]
            odd = oe_ref[(2 * j + 1) * Q_TILE:(2 * j + 2) * Q_TILE, :]
            even_sw = pltpu.roll(even, HEAD_DIM, axis=1)
            odd_sw = pltpu.roll(odd, HEAD_DIM, axis=1)
            pair = jnp.where(low, even, odd_sw) / jnp.where(low, even_sw, odd)
            o_ref[0, g, pl.ds(r0, Q_TILE), j * LANES:(j + 1) * LANES] = pair.astype(BF16)

    n_buf = N_SCORE_BUFS
    assert n_tiles % n_buf == 0 and tiles_per_group >= NEXT_TILES and len(scratch) == 4 * n_buf
    s_refs, mv_refs = scratch[:n_buf], scratch[n_buf:2 * n_buf]
    m_refs, oe_refs = scratch[2 * n_buf:3 * n_buf], scratch[3 * n_buf:]

    def slot(t, k, first=False):
        row_max(mv_refs[(k + 1) % n_buf], m_refs[(k + 1) % n_buf])
        if not first:
            finalize(t - 1, oe_refs[(k - 1) % n_buf])
        scores(t + NEXT_TILES, s_refs[(k + NEXT_TILES) % n_buf], mv_refs[(k + NEXT_TILES) % n_buf])
        output(t, s_refs[k % n_buf], m_refs[k % n_buf], oe_refs[k % n_buf])

    @pl.when(pl.program_id(0) == 0)
    def _():
        for t in range(NEXT_TILES):
            scores(t, s_refs[t], mv_refs[t])
        row_max(mv_refs[0], m_refs[0])

    n_loop = (n_tiles - NEXT_TILES - 1) // ATTN_SLOTS_PER_STEP * ATTN_SLOTS_PER_STEP
    assert n_loop > 0 and ATTN_SLOTS_PER_STEP % n_buf == 0
    slot(0, 0, first=True)

    def body(jj, carry):
        for i in range(ATTN_SLOTS_PER_STEP):
            slot(1 + jj * ATTN_SLOTS_PER_STEP + i, (1 + i) % n_buf)
        return carry

    lax.fori_loop(0, n_loop // ATTN_SLOTS_PER_STEP, body, 0)

    for t in range(1 + n_loop, n_tiles):
        slot(t, t % n_buf)
    finalize(n_tiles - 1, oe_refs[(n_tiles - 1) % n_buf])


def _mix_ffn_kernel(x_ref, a_ref, u_ref, uprev_ref, unext_ref, pw_ref, pb_ref, ps_ref,
                    wout_ref, g2_ref, wg_ref, wu_ref, wd_ref, o_ref,
                    ext_ref, mixp_ref, x1_ref, h_ref, *, seq, d_ff, n_tiles):
    rows = x_ref.shape[0]
    tiles_per_seq = seq // rows
    step = pl.program_id(0)

    st = jnp.minimum(step, n_tiles - 1) % tiles_per_seq

    def stage_windows():
        ext_ref[0:POOL_HALO, :] = jnp.where(st > 0, uprev_ref[...], 0.0)
        ext_ref[POOL_HALO:POOL_HALO + rows, :] = u_ref[...]
        ext_ref[POOL_HALO + rows:, :] = jnp.where(st < tiles_per_seq - 1, unext_ref[...], 0.0)

    def pool_group(g):
        n_ext = rows + 2 * POOL_HALO
        w = POOL_WINDOWS[g]
        assert w // 2 <= POOL_HALO
        cs = slice(g * POOL_GROUP_DIM, (g + 1) * POOL_GROUP_DIM)
        ext = ext_ref[:, cs]
        f = ext
        k = 1
        while k < w // 2:
            f = f + pltpu.roll(f, n_ext - k, axis=0)
            k *= 2
        both = f + pltpu.roll(f, w // 2, axis=0)
        win = both[POOL_HALO:POOL_HALO + rows]

        def edge_mean(r0):
            t = st * rows + r0 + lax.broadcasted_iota(jnp.int32, (POOL_HALO, LANES), 0)
            lo = jnp.maximum(t - w // 2, 0)
            hi = jnp.minimum(t - w // 2 + w, seq)
            return win[r0:r0 + POOL_HALO] / (hi - lo).astype(F32)

        mean = jnp.concatenate([edge_mean(0), win[POOL_HALO:rows - POOL_HALO] * (1.0 / w),
                                edge_mean(rows - POOL_HALO)], axis=0)
        pooled = (mean - ext[POOL_HALO:POOL_HALO + rows]).astype(BF16)
        y = jnp.dot(pooled, pw_ref[g].astype(BF16), preferred_element_type=F32) + pb_ref[g]
        mixp_ref[:, cs] = (y * ps_ref[:, cs]).astype(BF16)

    def project_out():
        mix = jnp.concatenate([a_ref[0, g] for g in range(N_KV_HEADS)] + [mixp_ref[...]],
                              axis=1)
        x1 = x_ref[...] + jnp.dot(mix, wout_ref[...], preferred_element_type=F32)
        ms = jnp.mean(x1 * x1, axis=-1, keepdims=True)
        x1_ref[...] = x1
        return (x1 * lax.rsqrt(ms + EPS) * g2_ref[...]).astype(BF16)

    bounds = [(c0, min(c0 + FF_CHUNK, d_ff)) for c0 in range(0, d_ff, FF_CHUNK)]

    @pl.when(step == 0)
    def _():
        stage_windows()
        for g in range(N_POOL_GROUPS):
            pool_group(g)
        h_ref[...] = project_out()

    @pl.when(step > 0)
    def _():
        stage_windows()
        h = h_ref[...]
        acc = x1_ref[...]
        h_next = []
        between = [[functools.partial(pool_group, g)] for g in range(N_POOL_GROUPS)]
        between += [[lambda: h_next.append(project_out())], []]
        assert len(between) == len(bounds)
        for c, (c0, c1) in enumerate(bounds):
            gate = jnp.dot(h, wg_ref[:, c0:c1], preferred_element_type=F32)
            up = jnp.dot(h, wu_ref[:, c0:c1], preferred_element_type=F32)
            if c == len(bounds) - 1:
                h_ref[...] = h_next[0]
            act = (gate * jax.nn.sigmoid(gate) * up).astype(BF16)
            acc = acc + jnp.dot(act, wd_ref[c0:c1, :], preferred_element_type=F32)
            for piece in between[c]:
                piece()
        o_ref[...] = acc


def _const_spec(shape):
    nd = len(shape)
    return pl.BlockSpec(shape, lambda *_: (0,) * nd)


def _layer(x2, seq, norm1_g, w_in, q_norm_g, k_norm_g, pool_w, pool_b, pool_scale,
           w_out, norm2_g, w_gate, w_up, w_down):
    n_tok = x2.shape[0]
    batch = n_tok // seq
    d_ff = w_gate.shape[1]
    assert seq % ROW_TILE == 0 and seq % Q_TILE == 0 and ROW_TILE % POOL_HALO == 0
    assert seq % INPROJ_ROWS == 0
    n_row_tiles = n_tok // ROW_TILE
    n_in_tiles = n_tok // INPROJ_ROWS

    cos, sin = _rope_tables(seq)
    seg = jnp.asarray(_segment_ones(256), BF16)
    q_scale = HEAD_DIM ** -0.5 * math.log2(math.e)

    params = pltpu.CompilerParams(dimension_semantics=("arbitrary",),
                                  vmem_limit_bytes=VMEM_LIMIT_BYTES)

    def grouped_rows(width, rows, tile):
        per_seq = seq // rows
        return pl.BlockSpec((1, N_KV_HEADS, rows, width),
                            lambda i: (tile(i) // per_seq, 0, tile(i) % per_seq, 0))

    def lead_tile(n):
        return lambda i: jnp.minimum(i, n - 1)

    mix_tile = lead_tile(n_row_tiles)
    in_tile = lead_tile(n_in_tiles)

    def lag_tile(i):
        return jnp.maximum(i - 1, 0)

    def grouped_seq(width):
        return pl.BlockSpec((1, N_KV_HEADS, seq, width), lambda b: (b, 0, 0, 0))

    q, krep, vext, u = pl.pallas_call(
        functools.partial(_inproj_kernel, q_scale=q_scale, n_tiles=n_in_tiles, seq=seq),
        grid=(n_in_tiles + 1,),
        in_specs=[
            pl.BlockSpec((INPROJ_ROWS, D_MODEL), lambda i: (in_tile(i), 0)),
            _const_spec((1, D_MODEL)),
            _const_spec((D_MODEL, IN_WIDTH)),
            _const_spec((1, HEAD_DIM)),
            _const_spec((1, HEAD_DIM)),
            _const_spec((seq, LANES)),
            _const_spec((seq, LANES)),
            _const_spec((256, 256)),
        ],
        out_specs=[
            grouped_rows(GROUP_WIDTH, INPROJ_ROWS, lag_tile),
            grouped_rows(GROUP_WIDTH, INPROJ_ROWS, lag_tile),
            grouped_rows(LANES, INPROJ_ROWS, lag_tile),
            pl.BlockSpec((INPROJ_ROWS, POOL_WIDTH), lambda i: (lag_tile(i), 0)),
        ],
        out_shape=[
            jax.ShapeDtypeStruct((batch, N_KV_HEADS, seq, GROUP_WIDTH), BF16),
            jax.ShapeDtypeStruct((batch, N_KV_HEADS, seq, GROUP_WIDTH), BF16),
            jax.ShapeDtypeStruct((batch, N_KV_HEADS, seq, LANES), BF16),
            jax.ShapeDtypeStruct((n_tok, POOL_WIDTH), F32),
        ],
        scratch_shapes=[pltpu.VMEM((INPROJ_ROWS, IN_WIDTH), F32),
                        pltpu.VMEM((INPROJ_ROWS, IN_WIDTH), F32),
                        pltpu.VMEM((D_MODEL, IN_WIDTH), BF16)],
        compiler_params=params,
        name="inproj",
    )(x2, norm1_g.astype(F32)[None, :], w_in.astype(F32),
      q_norm_g.astype(F32)[None, :], k_norm_g.astype(F32)[None, :],
      jnp.asarray(cos), jnp.asarray(sin), seg)

    late_weights = [w_out, w_gate, w_up, w_down]
    for w in late_weights:
        assert w.shape[0] % (batch * BF16_SUBLANES) == 0

    def next_lead(rows):
        return pl.BlockSpec((1, 1, rows, GROUP_WIDTH),
                            lambda b: (jnp.minimum(b + 1, batch - 1), 0, 0, 0))

    def row_slab(w):
        return pl.BlockSpec((w.shape[0] // batch, w.shape[1]), lambda b: (b, 0))

    attn, w_out_b, w_gate_b, w_up_b, w_down_b = pl.pallas_call(
        functools.partial(_attention_kernel, seq=seq),
        grid=(batch,),
        in_specs=([grouped_seq(GROUP_WIDTH), grouped_seq(GROUP_WIDTH), grouped_seq(LANES),
                   next_lead(NEXT_TILES * Q_TILE), next_lead(seq)]
                  + [row_slab(w) for w in late_weights]),
        out_specs=[grouped_seq(GROUP_WIDTH)] + [row_slab(w) for w in late_weights],
        out_shape=([jax.ShapeDtypeStruct((batch, N_KV_HEADS, seq, GROUP_WIDTH), BF16)]
                   + [jax.ShapeDtypeStruct(w.shape, BF16) for w in late_weights]),
        scratch_shapes=([pltpu.VMEM((Q_PER_KV * Q_TILE, seq), F32)] * N_SCORE_BUFS
                        + [pltpu.VMEM((Q_PER_KV * Q_TILE, LANES), F32)] * N_SCORE_BUFS
                        + [pltpu.VMEM((Q_PER_KV * Q_TILE, 1), F32)] * N_SCORE_BUFS
                        + [pltpu.VMEM((Q_PER_KV * Q_TILE, LANES), F32)] * N_SCORE_BUFS),
        compiler_params=params,
        name="attention",
    )(q, krep, vext, q, krep, *[w.astype(F32) for w in late_weights])

    halo_blocks = ROW_TILE // POOL_HALO
    last_halo = n_tok // POOL_HALO - 1
    out = pl.pallas_call(
        functools.partial(_mix_ffn_kernel, seq=seq, d_ff=d_ff, n_tiles=n_row_tiles),
        grid=(n_row_tiles + 1,),
        in_specs=[
            pl.BlockSpec((ROW_TILE, D_MODEL), lambda i: (mix_tile(i), 0)),
            grouped_rows(GROUP_WIDTH, ROW_TILE, mix_tile),
            pl.BlockSpec((ROW_TILE, POOL_WIDTH), lambda i: (mix_tile(i), 0)),
            pl.BlockSpec((POOL_HALO, POOL_WIDTH),
                         lambda i: (jnp.maximum(mix_tile(i) * halo_blocks - 1, 0), 0)),
            pl.BlockSpec((POOL_HALO, POOL_WIDTH),
                         lambda i: (jnp.minimum((mix_tile(i) + 1) * halo_blocks, last_halo), 0)),
            _const_spec((N_POOL_GROUPS, POOL_GROUP_DIM, POOL_GROUP_DIM)),
            _const_spec((N_POOL_GROUPS, 1, POOL_GROUP_DIM)),
            _const_spec((1, POOL_WIDTH)),
            _const_spec((D_MODEL, D_MODEL)),
            _const_spec((1, D_MODEL)),
            _const_spec((D_MODEL, d_ff)),
            _const_spec((D_MODEL, d_ff)),
            _const_spec((d_ff, D_MODEL)),
        ],
        out_specs=pl.BlockSpec((ROW_TILE, D_MODEL), lambda i: (jnp.maximum(i - 1, 0), 0)),
        out_shape=jax.ShapeDtypeStruct((n_tok, D_MODEL), F32),
        scratch_shapes=[pltpu.VMEM((ROW_TILE + 2 * POOL_HALO, POOL_WIDTH), F32),
                        pltpu.VMEM((ROW_TILE, POOL_WIDTH), BF16),
                        pltpu.VMEM((ROW_TILE, D_MODEL), F32), pltpu.VMEM((ROW_TILE, D_MODEL), BF16)],
        compiler_params=params,
        name="mix_ffn",
    )(x2, attn, u, u, u, pool_w.astype(F32), pool_b.astype(F32)[:, None, :],
      pool_scale.astype(F32)[None, :], w_out_b, norm2_g.astype(F32)[None, :],
      w_gate_b, w_up_b, w_down_b)
    return out


def kernel(x, norm1_g, w_in, q_norm_g, k_norm_g, pool_w, pool_b, pool_scale,
           w_out, norm2_g, w_gate, w_up, w_down):
    batch, seq, d_model = x.shape
    assert d_model == D_MODEL and w_in.shape[-1] == IN_WIDTH
    x2 = x.reshape(batch * seq, d_model)
    for l in range(norm1_g.shape[0]):
        x2 = _layer(x2, seq, norm1_g[l], w_in[l], q_norm_g[l], k_norm_g[l], pool_w[l],
                    pool_b[l], pool_scale[l], w_out[l], norm2_g[l], w_gate[l], w_up[l],
                    w_down[l])
    return x2.reshape(batch, seq, d_model)
```

```python
import functools
import math

import jax
import jax.numpy as jnp
import numpy as np
from jax import lax
from jax.experimental import pallas as pl
from jax.experimental.pallas import tpu as pltpu

D_MODEL = 1024
HEAD_DIM = 64
N_Q_HEADS = 8
N_KV_HEADS = 2
Q_PER_KV = N_Q_HEADS // N_KV_HEADS
ATTN_WIDTH = N_Q_HEADS * HEAD_DIM
KV_WIDTH = N_KV_HEADS * HEAD_DIM
GROUP_WIDTH = Q_PER_KV * HEAD_DIM
ROPE_PAIRS = HEAD_DIM // 4
ROPE_THETA = 10000.0
POOL_WINDOWS = (2, 4, 8, 16)
N_POOL_GROUPS = len(POOL_WINDOWS)
POOL_WIDTH = D_MODEL - ATTN_WIDTH
POOL_GROUP_DIM = POOL_WIDTH // N_POOL_GROUPS
IN_WIDTH = ATTN_WIDTH + 2 * KV_WIDTH + POOL_WIDTH
GRID_W = 64
EPS = 1e-6

LANES = 128
BF16_SUBLANES = 16
POOL_HALO = 8
VMEM_LIMIT_BYTES = 56 * 1024 * 1024

ROW_TILE = 512
MIX_ROWS = 1024
INPROJ_ROWS = 1024
Q_TILE = 128
ATTN_SLOTS_PER_STEP = 8
N_SCORE_BUFS = 4
NEXT_TILES = 2
FF_CHUNK = 512
INPROJ_CHUNK = 256

F32 = jnp.float32
BF16 = jnp.bfloat16


def _lane_iota(shape):
    return lax.broadcasted_iota(jnp.int32, shape, len(shape) - 1)


def _rope_tables(seq):
    t = np.arange(seq)
    row = (t // GRID_W).astype(np.float64)
    col = (t % GRID_W).astype(np.float64)
    inv_freq = ROPE_THETA ** (-np.arange(ROPE_PAIRS, dtype=np.float64) / ROPE_PAIRS)
    ang_row = row[:, None] * inv_freq[None, :]
    ang_col = col[:, None] * inv_freq[None, :]
    ang = np.concatenate([ang_row, ang_row, ang_col, ang_col], axis=1)
    sign = np.concatenate([-np.ones(ROPE_PAIRS), np.ones(ROPE_PAIRS)] * 2)
    cos = np.tile(np.cos(ang), (1, 2)).astype(np.float32)
    sin = np.tile(np.sin(ang) * sign[None, :], (1, 2)).astype(np.float32)
    return cos, sin


def _segment_ones(width):
    idx = np.arange(width) // HEAD_DIM
    return (idx[:, None] == idx[None, :]).astype(np.float32)


def _head_sumsq(x, seg):
    sq = x * x
    hi = sq.astype(BF16)
    lo = (sq - hi.astype(F32)).astype(BF16)
    return (jnp.dot(hi, seg, preferred_element_type=F32)
            + jnp.dot(lo, seg, preferred_element_type=F32))


def _rope_partner(x):
    lane = _lane_iota(x.shape)
    fwd = pltpu.roll(x, LANES - ROPE_PAIRS, axis=1)
    bwd = pltpu.roll(x, ROPE_PAIRS, axis=1)
    return jnp.where((lane & ROPE_PAIRS) == 0, fwd, bwd)


def _norm_rope(x, sumsq, gain, cos, sin):
    y = x * lax.rsqrt(sumsq * (1.0 / HEAD_DIM) + EPS) * gain
    return y * cos + _rope_partner(y) * sin


def _inproj_kernel(x_ref, g1_ref, win_ref, qg_ref, kg_ref, cos_ref, sin_ref, seg_ref,
                   q_ref, krep_ref, vext_ref, u_ref, proja_ref, projb_ref, wbf_ref,
                   *, q_scale, n_tiles, seq):
    step = pl.program_id(0)
    rows = x_ref.shape[0]
    kv0 = ATTN_WIDTH
    u0 = ATTN_WIDTH + 2 * KV_WIDTH
    pos0 = pl.multiple_of((jnp.maximum(step - 1, 0) % (seq // rows)) * rows, rows)

    def rope_tables():
        return cos_ref[pl.ds(pos0, rows), :], sin_ref[pl.ds(pos0, rows), :]

    def head_pair_gain(g_ref):
        g = g_ref[...]
        return jnp.concatenate([g] * (LANES // HEAD_DIM), axis=1)

    def normed_input():
        x = x_ref[...]
        ms = jnp.mean(x * x, axis=-1, keepdims=True)
        return (x * lax.rsqrt(ms + EPS) * g1_ref[...]).astype(BF16)

    def finish_q(src_ref, g):
        qh = src_ref[:, g * GROUP_WIDTH:(g + 1) * GROUP_WIDTH]
        ss = _head_sumsq(qh, seg_ref[...])
        for j in range(GROUP_WIDTH // LANES):
            sl = slice(j * LANES, (j + 1) * LANES)
            qr = _norm_rope(qh[:, sl], ss[:, sl], head_pair_gain(qg_ref), *rope_tables()) * q_scale
            q_ref[0, g, :, sl] = qr.astype(BF16)

    def finish_kv(src_ref):
        k = src_ref[:, kv0:kv0 + KV_WIDTH]
        kss = _head_sumsq(k, seg_ref[:KV_WIDTH, :KV_WIDTH])
        kr = _norm_rope(k, kss, head_pair_gain(kg_ref), *rope_tables())
        first = _lane_iota(kr.shape) < HEAD_DIM
        kswap = pltpu.roll(kr, HEAD_DIM, axis=1)
        k00 = jnp.where(first, kr, kswap).astype(BF16)
        k11 = jnp.where(first, kswap, kr).astype(BF16)
        for j in range(GROUP_WIDTH // LANES):
            krep_ref[0, 0, :, j * LANES:(j + 1) * LANES] = k00
            krep_ref[0, 1, :, j * LANES:(j + 1) * LANES] = k11
        v = src_ref[:, kv0 + KV_WIDTH:kv0 + 2 * KV_WIDTH]
        vswap = pltpu.roll(v, HEAD_DIM, axis=1)
        vext_ref[0, 0] = jnp.where(first, v, 1.0).astype(BF16)
        vext_ref[0, 1] = jnp.where(first, vswap, 1.0).astype(BF16)

    def finish_u(src_ref, half):
        cs = slice(half * (POOL_WIDTH // 2), (half + 1) * (POOL_WIDTH // 2))
        u_ref[:, cs] = src_ref[:, u0 + cs.start:u0 + cs.stop]

    def finish_pieces(src_ref):
        return [functools.partial(finish_q, src_ref, 0), functools.partial(finish_q, src_ref, 1),
                functools.partial(finish_kv, src_ref),
                functools.partial(finish_u, src_ref, 0), functools.partial(finish_u, src_ref, 1)]

    def project(dst_ref, between=()):
        h = normed_input()
        for c in range(IN_WIDTH // INPROJ_CHUNK):
            cs = slice(c * INPROJ_CHUNK, (c + 1) * INPROJ_CHUNK)
            dst_ref[:, cs] = jnp.dot(h, wbf_ref[:, cs], preferred_element_type=F32)
            if c < len(between):
                between[c]()

    even = step % 2 == 0
    assert n_tiles % 2 == 0

    @pl.when(step == 0)
    def _():
        wbf_ref[...] = win_ref[...].astype(BF16)
        project(proja_ref)

    @pl.when(jnp.logical_and(even, jnp.logical_and(step > 0, step < n_tiles)))
    def _():
        project(proja_ref, finish_pieces(projb_ref))

    @pl.when(jnp.logical_not(even))
    def _():
        project(projb_ref, finish_pieces(proja_ref))

    @pl.when(step == n_tiles)
    def _():
        for piece in finish_pieces(projb_ref):
            piece()


def _attention_kernel(q_ref, krep_ref, vext_ref, qn_ref, kn_ref,
                      wf0_ref, wf1_ref, wf2_ref, wf3_ref,
                      o_ref, wb0_ref, wb1_ref, wb2_ref, wb3_ref, *scratch, seq):
    for wf_ref, wb_ref in ((wf0_ref, wb0_ref), (wf1_ref, wb1_ref),
                           (wf2_ref, wb2_ref), (wf3_ref, wb3_ref)):
        wb_ref[...] = wf_ref[...].astype(BF16)

    tiles_per_group = seq // Q_TILE
    n_tiles = N_KV_HEADS * tiles_per_group
    lane_q = _lane_iota((Q_TILE, GROUP_WIDTH))
    lane_o = _lane_iota((Q_TILE, LANES))

    def tile_pos(t):
        g = t // tiles_per_group
        r0 = pl.multiple_of((t % tiles_per_group) * Q_TILE, Q_TILE)
        return g, r0

    def scores(t, s_ref, mv_ref):
        if isinstance(t, int) and t >= n_tiles:
            assert t - n_tiles < NEXT_TILES
            qt = qn_ref[0, 0, (t - n_tiles) * Q_TILE:(t - n_tiles + 1) * Q_TILE, :]
            keys = kn_ref[0, 0]
        else:
            g, r0 = tile_pos(t)
            qt = q_ref[0, g, pl.ds(r0, Q_TILE), :]
            keys = krep_ref[0, g]
        zero = jnp.zeros_like(qt)
        qm = jnp.concatenate(
            [jnp.where((lane_q // HEAD_DIM) == hh, qt, zero) for hh in range(Q_PER_KV)], axis=0)
        s = lax.dot_general(qm, keys, (((1,), (1,)), ((), ())),
                            preferred_element_type=F32)
        s_ref[...] = s
        mv_ref[...] = functools.reduce(
            jnp.maximum, [s[:, j * LANES:(j + 1) * LANES] for j in range(seq // LANES)])

    def row_max(mv_ref, m_ref):
        m_ref[...] = jnp.max(mv_ref[...], axis=-1, keepdims=True)

    def output(t, s_ref, m_ref, oe_ref):
        g, _ = tile_pos(t)
        p = jnp.exp2(s_ref[...] - m_ref[...]).astype(BF16)
        oe_ref[...] = jnp.dot(p, vext_ref[0, g], preferred_element_type=F32)

    def finalize(t, oe_ref):
        g, r0 = tile_pos(t)
        low = lane_o < HEAD_DIM
        for j in range(Q_PER_KV // 2):
            even = oe_ref[2 * j * Q_TILE:(2 * j + 1) * Q_TILE, :]
            odd = oe_ref[(2 * j + 1) * Q_TILE:(2 * j + 2) * Q_TILE, :]
            even_sw = pltpu.roll(even, HEAD_DIM, axis=1)
            odd_sw = pltpu.roll(odd, HEAD_DIM, axis=1)
            pair = jnp.where(low, even, odd_sw) / jnp.where(low, even_sw, odd)
            o_ref[0, g, pl.ds(r0, Q_TILE), j * LANES:(j + 1) * LANES] = pair.astype(BF16)

    n_buf = N_SCORE_BUFS
    assert n_tiles % n_buf == 0 and tiles_per_group >= NEXT_TILES and len(scratch) == 4 * n_buf
    s_refs, mv_refs = scratch[:n_buf], scratch[n_buf:2 * n_buf]
    m_refs, oe_refs = scratch[2 * n_buf:3 * n_buf], scratch[3 * n_buf:]

    def slot(t, k, first=False):
        row_max(mv_refs[(k + 1) % n_buf], m_refs[(k + 1) % n_buf])
        if not first:
            finalize(t - 1, oe_refs[(k - 1) % n_buf])
        scores(t + NEXT_TILES, s_refs[(k + NEXT_TILES) % n_buf], mv_refs[(k + NEXT_TILES) % n_buf])
        output(t, s_refs[k % n_buf], m_refs[k % n_buf], oe_refs[k % n_buf])

    @pl.when(pl.program_id(0) == 0)
    def _():
        for t in range(NEXT_TILES):
            scores(t, s_refs[t], mv_refs[t])
        row_max(mv_refs[0], m_refs[0])

    n_loop = (n_tiles - NEXT_TILES - 1) // ATTN_SLOTS_PER_STEP * ATTN_SLOTS_PER_STEP
    assert n_loop > 0 and ATTN_SLOTS_PER_STEP % n_buf == 0
    slot(0, 0, first=True)

    def body(jj, carry):
        for i in range(ATTN_SLOTS_PER_STEP):
            slot(1 + jj * ATTN_SLOTS_PER_STEP + i, (1 + i) % n_buf)
        return carry

    lax.fori_loop(0, n_loop // ATTN_SLOTS_PER_STEP, body, 0)

    for t in range(1 + n_loop, n_tiles):
        slot(t, t % n_buf)
    finalize(n_tiles - 1, oe_refs[(n_tiles - 1) % n_buf])


def _mix_ffn_kernel(x_ref, a_ref, u_ref, uprev_ref, unext_ref, pw_ref, pb_ref, ps_ref,
                    wout_ref, g2_ref, wg_ref, wu_ref, wd_ref, o_ref,
                    ext_ref, mixp_ref, *state, seq, d_ff, n_blocks):
    blk_rows = x_ref.shape[0]
    rows = ROW_TILE
    n_sub = blk_rows // rows
    assert len(state) == 2 * n_sub
    x1_refs, h_refs = state[:n_sub], state[n_sub:]
    blocks_per_seq = seq // blk_rows
    step = pl.program_id(0)

    blk = jnp.minimum(step, n_blocks - 1) % blocks_per_seq

    def stage_windows(j):
        r0 = j * rows
        if j == 0:
            prev = jnp.where(blk > 0, uprev_ref[...], 0.0)
        else:
            prev = u_ref[r0 - POOL_HALO:r0, :]
        if j == n_sub - 1:
            nxt = jnp.where(blk < blocks_per_seq - 1, unext_ref[...], 0.0)
        else:
            nxt = u_ref[r0 + rows:r0 + rows + POOL_HALO, :]
        ext_ref[0:POOL_HALO, :] = prev
        ext_ref[POOL_HALO:POOL_HALO + rows, :] = u_ref[r0:r0 + rows, :]
        ext_ref[POOL_HALO + rows:, :] = nxt

    def pool_group(j, g):
        n_ext = rows + 2 * POOL_HALO
        w = POOL_WINDOWS[g]
        assert w // 2 <= POOL_HALO
        cs = slice(g * POOL_GROUP_DIM, (g + 1) * POOL_GROUP_DIM)
        ext = ext_ref[:, cs]
        f = ext
        k = 1
        while k < w // 2:
            f = f + pltpu.roll(f, n_ext - k, axis=0)
            k *= 2
        both = f + pltpu.roll(f, w // 2, axis=0)
        win = both[POOL_HALO:POOL_HALO + rows]

        def edge_mean(r0):
            t = (blk * blk_rows + j * rows + r0
                 + lax.broadcasted_iota(jnp.int32, (POOL_HALO, LANES), 0))
            lo = jnp.maximum(t - w // 2, 0)
            hi = jnp.minimum(t - w // 2 + w, seq)
            return win[r0:r0 + POOL_HALO] / (hi - lo).astype(F32)

        mean = jnp.concatenate([edge_mean(0), win[POOL_HALO:rows - POOL_HALO] * (1.0 / w),
                                edge_mean(rows - POOL_HALO)], axis=0)
        pooled = (mean - ext[POOL_HALO:POOL_HALO + rows]).astype(BF16)
        y = jnp.dot(pooled, pw_ref[g].astype(BF16), preferred_element_type=F32) + pb_ref[g]
        mixp_ref[:, cs] = (y * ps_ref[:, cs]).astype(BF16)

    def project_out(j):
        rs = slice(j * rows, (j + 1) * rows)
        mix = jnp.concatenate([a_ref[0, g, rs, :] for g in range(N_KV_HEADS)] + [mixp_ref[...]],
                              axis=1)
        x1 = x_ref[rs, :] + jnp.dot(mix, wout_ref[...], preferred_element_type=F32)
        ms = jnp.mean(x1 * x1, axis=-1, keepdims=True)
        x1_refs[j][...] = x1
        return (x1 * lax.rsqrt(ms + EPS) * g2_ref[...]).astype(BF16)

    bounds = [(c0, min(c0 + FF_CHUNK, d_ff)) for c0 in range(0, d_ff, FF_CHUNK)]

    @pl.when(step == 0)
    def _():
        for j in range(n_sub):
            stage_windows(j)
            for g in range(N_POOL_GROUPS):
                pool_group(j, g)
            h_refs[j][...] = project_out(j)

    def ffn_and_next_mixer(j):
        stage_windows(j)
        h = h_refs[j][...]
        acc = x1_refs[j][...]
        h_next = []
        between = [[functools.partial(pool_group, j, g)] for g in range(N_POOL_GROUPS)]
        between += [[lambda: h_next.append(project_out(j))], []]
        assert len(between) == len(bounds)
        for c, (c0, c1) in enumerate(bounds):
            gate = jnp.dot(h, wg_ref[:, c0:c1], preferred_element_type=F32)
            up = jnp.dot(h, wu_ref[:, c0:c1], preferred_element_type=F32)
            if c == len(bounds) - 1:
                h_refs[j][...] = h_next[0]
            act = (gate * jax.nn.sigmoid(gate) * up).astype(BF16)
            acc = acc + jnp.dot(act, wd_ref[c0:c1, :], preferred_element_type=F32)
            for piece in between[c]:
                piece()
        o_ref[j * rows:(j + 1) * rows, :] = acc

    @pl.when(step > 0)
    def _():
        for j in range(n_sub):
            ffn_and_next_mixer(j)


def _const_spec(shape):
    nd = len(shape)
    return pl.BlockSpec(shape, lambda *_: (0,) * nd)


def _layer(x2, seq, norm1_g, w_in, q_norm_g, k_norm_g, pool_w, pool_b, pool_scale,
           w_out, norm2_g, w_gate, w_up, w_down):
    n_tok = x2.shape[0]
    batch = n_tok // seq
    d_ff = w_gate.shape[1]
    assert seq % ROW_TILE == 0 and seq % Q_TILE == 0 and ROW_TILE % POOL_HALO == 0
    assert seq % INPROJ_ROWS == 0
    assert MIX_ROWS % ROW_TILE == 0 and seq % MIX_ROWS == 0
    n_mix_blocks = n_tok // MIX_ROWS
    n_in_tiles = n_tok // INPROJ_ROWS

    cos, sin = _rope_tables(seq)
    seg = jnp.asarray(_segment_ones(256), BF16)
    q_scale = HEAD_DIM ** -0.5 * math.log2(math.e)

    params = pltpu.CompilerParams(dimension_semantics=("arbitrary",),
                                  vmem_limit_bytes=VMEM_LIMIT_BYTES)

    def grouped_rows(width, rows, tile):
        per_seq = seq // rows
        return pl.BlockSpec((1, N_KV_HEADS, rows, width),
                            lambda i: (tile(i) // per_seq, 0, tile(i) % per_seq, 0))

    def lead_tile(n):
        return lambda i: jnp.minimum(i, n - 1)

    mix_tile = lead_tile(n_mix_blocks)
    in_tile = lead_tile(n_in_tiles)

    def lag_tile(i):
        return jnp.maximum(i - 1, 0)

    def grouped_seq(width):
        return pl.BlockSpec((1, N_KV_HEADS, seq, width), lambda b: (b, 0, 0, 0))

    q, krep, vext, u = pl.pallas_call(
        functools.partial(_inproj_kernel, q_scale=q_scale, n_tiles=n_in_tiles, seq=seq),
        grid=(n_in_tiles + 1,),
        in_specs=[
            pl.BlockSpec((INPROJ_ROWS, D_MODEL), lambda i: (in_tile(i), 0)),
            _const_spec((1, D_MODEL)),
            _const_spec((D_MODEL, IN_WIDTH)),
            _const_spec((1, HEAD_DIM)),
            _const_spec((1, HEAD_DIM)),
            _const_spec((seq, LANES)),
            _const_spec((seq, LANES)),
            _const_spec((256, 256)),
        ],
        out_specs=[
            grouped_rows(GROUP_WIDTH, INPROJ_ROWS, lag_tile),
            grouped_rows(GROUP_WIDTH, INPROJ_ROWS, lag_tile),
            grouped_rows(LANES, INPROJ_ROWS, lag_tile),
            pl.BlockSpec((INPROJ_ROWS, POOL_WIDTH), lambda i: (lag_tile(i), 0)),
        ],
        out_shape=[
            jax.ShapeDtypeStruct((batch, N_KV_HEADS, seq, GROUP_WIDTH), BF16),
            jax.ShapeDtypeStruct((batch, N_KV_HEADS, seq, GROUP_WIDTH), BF16),
            jax.ShapeDtypeStruct((batch, N_KV_HEADS, seq, LANES), BF16),
            jax.ShapeDtypeStruct((n_tok, POOL_WIDTH), F32),
        ],
        scratch_shapes=[pltpu.VMEM((INPROJ_ROWS, IN_WIDTH), F32),
                        pltpu.VMEM((INPROJ_ROWS, IN_WIDTH), F32),
                        pltpu.VMEM((D_MODEL, IN_WIDTH), BF16)],
        compiler_params=params,
        name="inproj",
    )(x2, norm1_g.astype(F32)[None, :], w_in.astype(F32),
      q_norm_g.astype(F32)[None, :], k_norm_g.astype(F32)[None, :],
      jnp.asarray(cos), jnp.asarray(sin), seg)

    late_weights = [w_out, w_gate, w_up, w_down]
    for w in late_weights:
        assert w.shape[0] % (batch * BF16_SUBLANES) == 0

    def next_lead(rows):
        return pl.BlockSpec((1, 1, rows, GROUP_WIDTH),
                            lambda b: (jnp.minimum(b + 1, batch - 1), 0, 0, 0))

    def row_slab(w):
        return pl.BlockSpec((w.shape[0] // batch, w.shape[1]), lambda b: (b, 0))

    attn, w_out_b, w_gate_b, w_up_b, w_down_b = pl.pallas_call(
        functools.partial(_attention_kernel, seq=seq),
        grid=(batch,),
        in_specs=([grouped_seq(GROUP_WIDTH), grouped_seq(GROUP_WIDTH), grouped_seq(LANES),
                   next_lead(NEXT_TILES * Q_TILE), next_lead(seq)]
                  + [row_slab(w) for w in late_weights]),
        out_specs=[grouped_seq(GROUP_WIDTH)] + [row_slab(w) for w in late_weights],
        out_shape=([jax.ShapeDtypeStruct((batch, N_KV_HEADS, seq, GROUP_WIDTH), BF16)]
                   + [jax.ShapeDtypeStruct(w.shape, BF16) for w in late_weights]),
        scratch_shapes=([pltpu.VMEM((Q_PER_KV * Q_TILE, seq), F32)] * N_SCORE_BUFS
                        + [pltpu.VMEM((Q_PER_KV * Q_TILE, LANES), F32)] * N_SCORE_BUFS
                        + [pltpu.VMEM((Q_PER_KV * Q_TILE, 1), F32)] * N_SCORE_BUFS
                        + [pltpu.VMEM((Q_PER_KV * Q_TILE, LANES), F32)] * N_SCORE_BUFS),
        compiler_params=params,
        name="attention",
    )(q, krep, vext, q, krep, *[w.astype(F32) for w in late_weights])

    halo_blocks = MIX_ROWS // POOL_HALO
    last_halo = n_tok // POOL_HALO - 1
    out = pl.pallas_call(
        functools.partial(_mix_ffn_kernel, seq=seq, d_ff=d_ff, n_blocks=n_mix_blocks),
        grid=(n_mix_blocks + 1,),
        in_specs=[
            pl.BlockSpec((MIX_ROWS, D_MODEL), lambda i: (mix_tile(i), 0)),
            grouped_rows(GROUP_WIDTH, MIX_ROWS, mix_tile),
            pl.BlockSpec((MIX_ROWS, POOL_WIDTH), lambda i: (mix_tile(i), 0)),
            pl.BlockSpec((POOL_HALO, POOL_WIDTH),
                         lambda i: (jnp.maximum(mix_tile(i) * halo_blocks - 1, 0), 0)),
            pl.BlockSpec((POOL_HALO, POOL_WIDTH),
                         lambda i: (jnp.minimum((mix_tile(i) + 1) * halo_blocks, last_halo), 0)),
            _const_spec((N_POOL_GROUPS, POOL_GROUP_DIM, POOL_GROUP_DIM)),
            _const_spec((N_POOL_GROUPS, 1, POOL_GROUP_DIM)),
            _const_spec((1, POOL_WIDTH)),
            _const_spec((D_MODEL, D_MODEL)),
            _const_spec((1, D_MODEL)),
            _const_spec((D_MODEL, d_ff)),
            _const_spec((D_MODEL, d_ff)),
            _const_spec((d_ff, D_MODEL)),
        ],
        out_specs=pl.BlockSpec((MIX_ROWS, D_MODEL), lambda i: (lag_tile(i), 0)),
        out_shape=jax.ShapeDtypeStruct((n_tok, D_MODEL), F32),
        scratch_shapes=([pltpu.VMEM((ROW_TILE + 2 * POOL_HALO, POOL_WIDTH), F32),
                         pltpu.VMEM((ROW_TILE, POOL_WIDTH), BF16)]
                        + [pltpu.VMEM((ROW_TILE, D_MODEL), F32)] * (MIX_ROWS // ROW_TILE)
                        + [pltpu.VMEM((ROW_TILE, D_MODEL), BF16)] * (MIX_ROWS // ROW_TILE)),
        compiler_params=params,
        name="mix_ffn",
    )(x2, attn, u, u, u, pool_w.astype(F32), pool_b.astype(F32)[:, None, :],
      pool_scale.astype(F32)[None, :], w_out_b, norm2_g.astype(F32)[None, :],
      w_gate_b, w_up_b, w_down_b)
    return out


def kernel(x, norm1_g, w_in, q_norm_g, k_norm_g, pool_w, pool_b, pool_scale,
           w_out, norm2_g, w_gate, w_up, w_down):
    batch, seq, d_model = x.shape
    assert d_model == D_MODEL and w_in.shape[-1] == IN_WIDTH
    x2 = x.reshape(batch * seq, d_model)
    for l in range(norm1_g.shape[0]):
        x2 = _layer(x2, seq, norm1_g[l], w_in[l], q_norm_g[l], k_norm_g[l], pool_w[l],
                    pool_b[l], pool_scale[l], w_out[l], norm2_g[l], w_gate[l], w_up[l],
                    w_down[l])
    return x2.reshape(batch, seq, d_model)
```

```python
import functools
import math

import jax
import jax.numpy as jnp
import numpy as np
from jax import lax
from jax.experimental import pallas as pl
from jax.experimental.pallas import tpu as pltpu

D_MODEL = 1024
HEAD_DIM = 64
N_Q_HEADS = 8
N_KV_HEADS = 2
Q_PER_KV = N_Q_HEADS // N_KV_HEADS
ATTN_WIDTH = N_Q_HEADS * HEAD_DIM
KV_WIDTH = N_KV_HEADS * HEAD_DIM
GROUP_WIDTH = Q_PER_KV * HEAD_DIM
ROPE_PAIRS = HEAD_DIM // 4
ROPE_THETA = 10000.0
POOL_WINDOWS = (2, 4, 8, 16)
N_POOL_GROUPS = len(POOL_WINDOWS)
POOL_WIDTH = D_MODEL - ATTN_WIDTH
POOL_GROUP_DIM = POOL_WIDTH // N_POOL_GROUPS
IN_WIDTH = ATTN_WIDTH + 2 * KV_WIDTH + POOL_WIDTH
GRID_W = 64
EPS = 1e-6

LANES = 128
BF16_SUBLANES = 16
POOL_HALO = 8
VMEM_LIMIT_BYTES = 56 * 1024 * 1024

ROW_TILE = 512
INPROJ_ROWS = 1024
Q_TILE = 128
ATTN_SLOTS_PER_STEP = 4
N_SCORE_BUFS = 4
NEXT_TILES = 2
FF_CHUNK = 512
INPROJ_CHUNK = 256

F32 = jnp.float32
BF16 = jnp.bfloat16


def _lane_iota(shape):
    return lax.broadcasted_iota(jnp.int32, shape, len(shape) - 1)


def _rope_tables(seq):
    t = np.arange(seq)
    row = (t // GRID_W).astype(np.float64)
    col = (t % GRID_W).astype(np.float64)
    inv_freq = ROPE_THETA ** (-np.arange(ROPE_PAIRS, dtype=np.float64) / ROPE_PAIRS)
    ang_row = row[:, None] * inv_freq[None, :]
    ang_col = col[:, None] * inv_freq[None, :]
    ang = np.concatenate([ang_row, ang_row, ang_col, ang_col], axis=1)
    sign = np.concatenate([-np.ones(ROPE_PAIRS), np.ones(ROPE_PAIRS)] * 2)
    cos = np.tile(np.cos(ang), (1, 2)).astype(np.float32)
    sin = np.tile(np.sin(ang) * sign[None, :], (1, 2)).astype(np.float32)
    return cos, sin


def _segment_ones(width):
    idx = np.arange(width) // HEAD_DIM
    return (idx[:, None] == idx[None, :]).astype(np.float32)


def _head_sumsq(x, seg):
    sq = x * x
    hi = sq.astype(BF16)
    lo = (sq - hi.astype(F32)).astype(BF16)
    return (jnp.dot(hi, seg, preferred_element_type=F32)
            + jnp.dot(lo, seg, preferred_element_type=F32))


def _rope_partner(x):
    lane = _lane_iota(x.shape)
    fwd = pltpu.roll(x, LANES - ROPE_PAIRS, axis=1)
    bwd = pltpu.roll(x, ROPE_PAIRS, axis=1)
    return jnp.where((lane & ROPE_PAIRS) == 0, fwd, bwd)


def _norm_rope(x, sumsq, gain, cos, sin):
    y = x * lax.rsqrt(sumsq * (1.0 / HEAD_DIM) + EPS) * gain
    return y * cos + _rope_partner(y) * sin


def _inproj_kernel(x_ref, g1_ref, win_ref, qg_ref, kg_ref, cos_ref, sin_ref, seg_ref,
                   q_ref, krep_ref, vext_ref, u_ref, proja_ref, projb_ref, wbf_ref,
                   *, q_scale, n_tiles, seq):
    step = pl.program_id(0)
    rows = x_ref.shape[0]
    kv0 = ATTN_WIDTH
    u0 = ATTN_WIDTH + 2 * KV_WIDTH
    pos0 = pl.multiple_of((jnp.maximum(step - 1, 0) % (seq // rows)) * rows, rows)

    def rope_tables():
        return cos_ref[pl.ds(pos0, rows), :], sin_ref[pl.ds(pos0, rows), :]

    def head_pair_gain(g_ref):
        g = g_ref[...]
        return jnp.concatenate([g] * (LANES // HEAD_DIM), axis=1)

    def normed_input():
        x = x_ref[...]
        ms = jnp.mean(x * x, axis=-1, keepdims=True)
        return (x * lax.rsqrt(ms + EPS) * g1_ref[...]).astype(BF16)

    def finish_q(src_ref, g):
        qh = src_ref[:, g * GROUP_WIDTH:(g + 1) * GROUP_WIDTH]
        ss = _head_sumsq(qh, seg_ref[...])
        for j in range(GROUP_WIDTH // LANES):
            sl = slice(j * LANES, (j + 1) * LANES)
            qr = _norm_rope(qh[:, sl], ss[:, sl], head_pair_gain(qg_ref), *rope_tables()) * q_scale
            q_ref[0, g, :, sl] = qr.astype(BF16)

    def finish_kv(src_ref):
        k = src_ref[:, kv0:kv0 + KV_WIDTH]
        kss = _head_sumsq(k, seg_ref[:KV_WIDTH, :KV_WIDTH])
        kr = _norm_rope(k, kss, head_pair_gain(kg_ref), *rope_tables())
        first = _lane_iota(kr.shape) < HEAD_DIM
        kswap = pltpu.roll(kr, HEAD_DIM, axis=1)
        k00 = jnp.where(first, kr, kswap).astype(BF16)
        k11 = jnp.where(first, kswap, kr).astype(BF16)
        for j in range(GROUP_WIDTH // LANES):
            krep_ref[0, 0, :, j * LANES:(j + 1) * LANES] = k00
            krep_ref[0, 1, :, j * LANES:(j + 1) * LANES] = k11
        v = src_ref[:, kv0 + KV_WIDTH:kv0 + 2 * KV_WIDTH]
        vswap = pltpu.roll(v, HEAD_DIM, axis=1)
        vext_ref[0, 0] = jnp.where(first, v, 1.0).astype(BF16)
        vext_ref[0, 1] = jnp.where(first, vswap, 1.0).astype(BF16)

    def finish_u(src_ref, half):
        cs = slice(half * (POOL_WIDTH // 2), (half + 1) * (POOL_WIDTH // 2))
        u_ref[:, cs] = src_ref[:, u0 + cs.start:u0 + cs.stop]

    def finish_pieces(src_ref):
        return [functools.partial(finish_q, src_ref, 0), functools.partial(finish_q, src_ref, 1),
                functools.partial(finish_kv, src_ref),
                functools.partial(finish_u, src_ref, 0), functools.partial(finish_u, src_ref, 1)]

    def project(dst_ref, between=()):
        h = normed_input()
        for c in range(IN_WIDTH // INPROJ_CHUNK):
            cs = slice(c * INPROJ_CHUNK, (c + 1) * INPROJ_CHUNK)
            dst_ref[:, cs] = jnp.dot(h, wbf_ref[:, cs], preferred_element_type=F32)
            if c < len(between):
                between[c]()

    even = step % 2 == 0
    assert n_tiles % 2 == 0

    @pl.when(step == 0)
    def _():
        wbf_ref[...] = win_ref[...].astype(BF16)
        project(proja_ref)

    @pl.when(jnp.logical_and(even, jnp.logical_and(step > 0, step < n_tiles)))
    def _():
        project(proja_ref, finish_pieces(projb_ref))

    @pl.when(jnp.logical_not(even))
    def _():
        project(projb_ref, finish_pieces(proja_ref))

    @pl.when(step == n_tiles)
    def _():
        for piece in finish_pieces(projb_ref):
            piece()


def _attention_kernel(q_ref, krep_ref, vext_ref, qn_ref, kn_ref,
                      wf0_ref, wf1_ref, wf2_ref, wf3_ref,
                      o_ref, wb0_ref, wb1_ref, wb2_ref, wb3_ref, *scratch, seq):
    for wf_ref, wb_ref in ((wf0_ref, wb0_ref), (wf1_ref, wb1_ref),
                           (wf2_ref, wb2_ref), (wf3_ref, wb3_ref)):
        wb_ref[...] = wf_ref[...].astype(BF16)

    tiles_per_group = seq // Q_TILE
    n_tiles = N_KV_HEADS * tiles_per_group
    lane_q = _lane_iota((Q_TILE, GROUP_WIDTH))
    lane_o = _lane_iota((Q_TILE, LANES))

    def tile_pos(t):
        g = t // tiles_per_group
        r0 = pl.multiple_of((t % tiles_per_group) * Q_TILE, Q_TILE)
        return g, r0

    def scores(t, s_ref, mv_ref):
        if isinstance(t, int) and t >= n_tiles:
            assert t - n_tiles < NEXT_TILES
            qt = qn_ref[0, 0, (t - n_tiles) * Q_TILE:(t - n_tiles + 1) * Q_TILE, :]
            keys = kn_ref[0, 0]
        else:
            g, r0 = tile_pos(t)
            qt = q_ref[0, g, pl.ds(r0, Q_TILE), :]
            keys = krep_ref[0, g]
        zero = jnp.zeros_like(qt)
        qm = jnp.concatenate(
            [jnp.where((lane_q // HEAD_DIM) == hh, qt, zero) for hh in range(Q_PER_KV)], axis=0)
        s = lax.dot_general(qm, keys, (((1,), (1,)), ((), ())),
                            preferred_element_type=F32)
        s_ref[...] = s
        mv_ref[...] = functools.reduce(
            jnp.maximum, [s[:, j * LANES:(j + 1) * LANES] for j in range(seq // LANES)])

    def row_max(mv_ref, m_ref):
        m_ref[...] = jnp.max(mv_ref[...], axis=-1, keepdims=True)

    def output(t, s_ref, m_ref, oe_ref):
        g, _ = tile_pos(t)
        p = jnp.exp2(s_ref[...] - m_ref[...]).astype(BF16)
        oe_ref[...] = jnp.dot(p, vext_ref[0, g], preferred_element_type=F32)

    def finalize(t, oe_ref):
        g, r0 = tile_pos(t)
        low = lane_o < HEAD_DIM
        for j in range(Q_PER_KV // 2):
            even = oe_ref[2 * j * Q_TILE:(2 * j + 1) * Q_TILE, :]
            odd = oe_ref[(2 * j + 1) * Q_TILE:(2 * j + 2) * Q_TILE, :]
            even_sw = pltpu.roll(even, HEAD_DIM, axis=1)
            odd_sw = pltpu.roll(odd, HEAD_DIM, axis=1)
            pair = jnp.where(low, even, odd_sw) / jnp.where(low, even_sw, odd)
            o_ref[0, g, pl.ds(r0, Q_TILE), j * LANES:(j + 1) * LANES] = pair.astype(BF16)

    n_buf = N_SCORE_BUFS
    assert n_tiles % n_buf == 0 and tiles_per_group >= NEXT_TILES and len(scratch) == 4 * n_buf
    s_refs, mv_refs = scratch[:n_buf], scratch[n_buf:2 * n_buf]
    m_refs, oe_refs = scratch[2 * n_buf:3 * n_buf], scratch[3 * n_buf:]

    def slot(t, k, first=False):
        row_max(mv_refs[(k + 1) % n_buf], m_refs[(k + 1) % n_buf])
        if not first:
            finalize(t - 1, oe_refs[(k - 1) % n_buf])
        scores(t + NEXT_TILES, s_refs[(k + NEXT_TILES) % n_buf], mv_refs[(k + NEXT_TILES) % n_buf])
        output(t, s_refs[k % n_buf], m_refs[k % n_buf], oe_refs[k % n_buf])

    @pl.when(pl.program_id(0) == 0)
    def _():
        for t in range(NEXT_TILES):
            scores(t, s_refs[t], mv_refs[t])
        row_max(mv_refs[0], m_refs[0])

    n_loop = (n_tiles - NEXT_TILES - 1) // ATTN_SLOTS_PER_STEP * ATTN_SLOTS_PER_STEP
    assert n_loop > 0 and ATTN_SLOTS_PER_STEP % n_buf == 0
    slot(0, 0, first=True)

    def body(jj, carry):
        for i in range(ATTN_SLOTS_PER_STEP):
            slot(1 + jj * ATTN_SLOTS_PER_STEP + i, (1 + i) % n_buf)
        return carry

    lax.fori_loop(0, n_loop // ATTN_SLOTS_PER_STEP, body, 0)

    for t in range(1 + n_loop, n_tiles):
        slot(t, t % n_buf)
    finalize(n_tiles - 1, oe_refs[(n_tiles - 1) % n_buf])


def _mix_ffn_kernel(x_ref, a_ref, u_ref, uprev_ref, unext_ref, pw_ref, pb_ref, ps_ref,
                    wout_ref, g2_ref, wg_ref, wu_ref, wd_ref, o_ref,
                    ext_ref, mixp_ref, x1_ref, h_ref, *, seq, d_ff, n_tiles):
    rows = x_ref.shape[0]
    tiles_per_seq = seq // rows
    step = pl.program_id(0)

    st = jnp.minimum(step, n_tiles - 1) % tiles_per_seq

    def stage_windows():
        ext_ref[0:POOL_HALO, :] = jnp.where(st > 0, uprev_ref[...], 0.0)
        ext_ref[POOL_HALO:POOL_HALO + rows, :] = u_ref[...]
        ext_ref[POOL_HALO + rows:, :] = jnp.where(st < tiles_per_seq - 1, unext_ref[...], 0.0)

    def pool_group(g):
        n_ext = rows + 2 * POOL_HALO
        w = POOL_WINDOWS[g]
        assert w // 2 <= POOL_HALO
        cs = slice(g * POOL_GROUP_DIM, (g + 1) * POOL_GROUP_DIM)
        ext = ext_ref[:, cs]
        f = ext
        k = 1
        while k < w // 2:
            f = f + pltpu.roll(f, n_ext - k, axis=0)
            k *= 2
        both = f + pltpu.roll(f, w // 2, axis=0)
        win = both[POOL_HALO:POOL_HALO + rows]

        def edge_mean(r0):
            t = st * rows + r0 + lax.broadcasted_iota(jnp.int32, (POOL_HALO, LANES), 0)
            lo = jnp.maximum(t - w // 2, 0)
            hi = jnp.minimum(t - w // 2 + w, seq)
            return win[r0:r0 + POOL_HALO] / (hi - lo).astype(F32)

        mean = jnp.concatenate([edge_mean(0), win[POOL_HALO:rows - POOL_HALO] * (1.0 / w),
                                edge_mean(rows - POOL_HALO)], axis=0)
        pooled = (mean - ext[POOL_HALO:POOL_HALO + rows]).astype(BF16)
        y = jnp.dot(pooled, pw_ref[g].astype(BF16), preferred_element_type=F32) + pb_ref[g]
        mixp_ref[:, cs] = (y * ps_ref[:, cs]).astype(BF16)

    def project_out():
        mix = jnp.concatenate([a_ref[0, g] for g in range(N_KV_HEADS)] + [mixp_ref[...]],
                              axis=1)
        x1 = x_ref[...] + jnp.dot(mix, wout_ref[...], preferred_element_type=F32)
        ms = jnp.mean(x1 * x1, axis=-1, keepdims=True)
        x1_ref[...] = x1
        return (x1 * lax.rsqrt(ms + EPS) * g2_ref[...]).astype(BF16)

    bounds = [(c0, min(c0 + FF_CHUNK, d_ff)) for c0 in range(0, d_ff, FF_CHUNK)]

    @pl.when(step == 0)
    def _():
        stage_windows()
        for g in range(N_POOL_GROUPS):
            pool_group(g)
        h_ref[...] = project_out()

    @pl.when(step > 0)
    def _():
        stage_windows()
        h = h_ref[...]
        acc = x1_ref[...]
        h_next = []
        between = [[functools.partial(pool_group, g)] for g in range(N_POOL_GROUPS)]
        between += [[lambda: h_next.append(project_out())], []]
        assert len(between) == len(bounds)
        for c, (c0, c1) in enumerate(bounds):
            gate = jnp.dot(h, wg_ref[:, c0:c1], preferred_element_type=F32)
            up = jnp.dot(h, wu_ref[:, c0:c1], preferred_element_type=F32)
            if c == len(bounds) - 1:
                h_ref[...] = h_next[0]
            act = (gate * jax.nn.sigmoid(gate) * up).astype(BF16)
            acc = acc + jnp.dot(act, wd_ref[c0:c1, :], preferred_element_type=F32)
            for piece in between[c]:
                piece()
        o_ref[...] = acc


def _const_spec(shape):
    nd = len(shape)
    return pl.BlockSpec(shape, lambda *_: (0,) * nd)


def _layer(x2, seq, norm1_g, w_in, q_norm_g, k_norm_g, pool_w, pool_b, pool_scale,
           w_out, norm2_g, w_gate, w_up, w_down):
    n_tok = x2.shape[0]
    batch = n_tok // seq
    d_ff = w_gate.shape[1]
    assert seq % ROW_TILE == 0 and seq % Q_TILE == 0 and ROW_TILE % POOL_HALO == 0
    assert seq % INPROJ_ROWS == 0
    n_row_tiles = n_tok // ROW_TILE
    n_in_tiles = n_tok // INPROJ_ROWS

    cos, sin = _rope_tables(seq)
    seg = jnp.asarray(_segment_ones(256), BF16)
    q_scale = HEAD_DIM ** -0.5 * math.log2(math.e)

    params = pltpu.CompilerParams(dimension_semantics=("arbitrary",),
                                  vmem_limit_bytes=VMEM_LIMIT_BYTES)

    def grouped_rows(width, rows, tile):
        per_seq = seq // rows
        return pl.BlockSpec((1, N_KV_HEADS, rows, width),
                            lambda i: (tile(i) // per_seq, 0, tile(i) % per_seq, 0))

    def lead_tile(n):
        return lambda i: jnp.minimum(i, n - 1)

    mix_tile = lead_tile(n_row_tiles)
    in_tile = lead_tile(n_in_tiles)

    def lag_tile(i):
        return jnp.maximum(i - 1, 0)

    def grouped_seq(width):
        return pl.BlockSpec((1, N_KV_HEADS, seq, width), lambda b: (b, 0, 0, 0))

    q, krep, vext, u = pl.pallas_call(
        functools.partial(_inproj_kernel, q_scale=q_scale, n_tiles=n_in_tiles, seq=seq),
        grid=(n_in_tiles + 1,),
        in_specs=[
            pl.BlockSpec((INPROJ_ROWS, D_MODEL), lambda i: (in_tile(i), 0)),
            _const_spec((1, D_MODEL)),
            _const_spec((D_MODEL, IN_WIDTH)),
            _const_spec((1, HEAD_DIM)),
            _const_spec((1, HEAD_DIM)),
            _const_spec((seq, LANES)),
            _const_spec((seq, LANES)),
            _const_spec((256, 256)),
        ],
        out_specs=[
            grouped_rows(GROUP_WIDTH, INPROJ_ROWS, lag_tile),
            grouped_rows(GROUP_WIDTH, INPROJ_ROWS, lag_tile),
            grouped_rows(LANES, INPROJ_ROWS, lag_tile),
            pl.BlockSpec((INPROJ_ROWS, POOL_WIDTH), lambda i: (lag_tile(i), 0)),
        ],
        out_shape=[
            jax.ShapeDtypeStruct((batch, N_KV_HEADS, seq, GROUP_WIDTH), BF16),
            jax.ShapeDtypeStruct((batch, N_KV_HEADS, seq, GROUP_WIDTH), BF16),
            jax.ShapeDtypeStruct((batch, N_KV_HEADS, seq, LANES), BF16),
            jax.ShapeDtypeStruct((n_tok, POOL_WIDTH), F32),
        ],
        scratch_shapes=[pltpu.VMEM((INPROJ_ROWS, IN_WIDTH), F32),
                        pltpu.VMEM((INPROJ_ROWS, IN_WIDTH), F32),
                        pltpu.VMEM((D_MODEL, IN_WIDTH), BF16)],
        compiler_params=params,
        name="inproj",
    )(x2, norm1_g.astype(F32)[None, :], w_in.astype(F32),
      q_norm_g.astype(F32)[None, :], k_norm_g.astype(F32)[None, :],
      jnp.asarray(cos), jnp.asarray(sin), seg)

    late_weights = [w_out, w_gate, w_up, w_down]
    for w in late_weights:
        assert w.shape[0] % (batch * BF16_SUBLANES) == 0

    def next_lead(rows):
        return pl.BlockSpec((1, 1, rows, GROUP_WIDTH),
                            lambda b: (jnp.minimum(b + 1, batch - 1), 0, 0, 0))

    def row_slab(w):
        return pl.BlockSpec((w.shape[0] // batch, w.shape[1]), lambda b: (b, 0))

    attn, w_out_b, w_gate_b, w_up_b, w_down_b = pl.pallas_call(
        functools.partial(_attention_kernel, seq=seq),
        grid=(batch,),
        in_specs=([grouped_seq(GROUP_WIDTH), grouped_seq(GROUP_WIDTH), grouped_seq(LANES),
                   next_lead(NEXT_TILES * Q_TILE), next_lead(seq)]
                  + [row_slab(w) for w in late_weights]),
        out_specs=[grouped_seq(GROUP_WIDTH)] + [row_slab(w) for w in late_weights],
        out_shape=([jax.ShapeDtypeStruct((batch, N_KV_HEADS, seq, GROUP_WIDTH), BF16)]
                   + [jax.ShapeDtypeStruct(w.shape, BF16) for w in late_weights]),
        scratch_shapes=([pltpu.VMEM((Q_PER_KV * Q_TILE, seq), F32)] * N_SCORE_BUFS
                        + [pltpu.VMEM((Q_PER_KV * Q_TILE, LANES), F32)] * N_SCORE_BUFS
                        + [pltpu.VMEM((Q_PER_KV * Q_TILE, 1), F32)] * N_SCORE_BUFS
                        + [pltpu.VMEM((Q_PER_KV * Q_TILE, LANES), F32)] * N_SCORE_BUFS),
        compiler_params=params,
        name="attention",
    )(q, krep, vext, q, krep, *[w.astype(F32) for w in late_weights])

    halo_blocks = ROW_TILE // POOL_HALO
    last_halo = n_tok // POOL_HALO - 1
    out = pl.pallas_call(
        functools.partial(_mix_ffn_kernel, seq=seq, d_ff=d_ff, n_tiles=n_row_tiles),
        grid=(n_row_tiles + 1,),
        in_specs=[
            pl.BlockSpec((ROW_TILE, D_MODEL), lambda i: (mix_tile(i), 0)),
            grouped_rows(GROUP_WIDTH, ROW_TILE, mix_tile),
            pl.BlockSpec((ROW_TILE, POOL_WIDTH), lambda i: (mix_tile(i), 0)),
            pl.BlockSpec((POOL_HALO, POOL_WIDTH),
                         lambda i: (jnp.maximum(mix_tile(i) * halo_blocks - 1, 0), 0)),
            pl.BlockSpec((POOL_HALO, POOL_WIDTH),
                         lambda i: (jnp.minimum((mix_tile(i) + 1) * halo_blocks, last_halo), 0)),
            _const_spec((N_POOL_GROUPS, POOL_GROUP_DIM, POOL_GROUP_DIM)),
            _const_spec((N_POOL_GROUPS, 1, POOL_GROUP_DIM)),
            _const_spec((1, POOL_WIDTH)),
            _const_spec((D_MODEL, D_MODEL)),
            _const_spec((1, D_MODEL)),
            _const_spec((D_MODEL, d_ff)),
            _const_spec((D_MODEL, d_ff)),
            _const_spec((d_ff, D_MODEL)),
        ],
        out_specs=pl.BlockSpec((ROW_TILE, D_MODEL), lambda i: (jnp.maximum(i - 1, 0), 0)),
        out_shape=jax.ShapeDtypeStruct((n_tok, D_MODEL), F32),
        scratch_shapes=[pltpu.VMEM((ROW_TILE + 2 * POOL_HALO, POOL_WIDTH), F32),
                        pltpu.VMEM((ROW_TILE, POOL_WIDTH), BF16),
                        pltpu.VMEM((ROW_TILE, D_MODEL), F32), pltpu.VMEM((ROW_TILE, D_MODEL), BF16)],
        compiler_params=params,
        name="mix_ffn",
    )(x2, attn, u, u, u, pool_w.astype(F32), pool_b.astype(F32)[:, None, :],
      pool_scale.astype(F32)[None, :], w_out_b, norm2_g.astype(F32)[None, :],
      w_gate_b, w_up_b, w_down_b)
    return out


def kernel(x, norm1_g, w_in, q_norm_g, k_norm_g, pool_w, pool_b, pool_scale,
           w_out, norm2_g, w_gate, w_up, w_down):
    batch, seq, d_model = x.shape
    assert d_model == D_MODEL and w_in.shape[-1] == IN_WIDTH
    x2 = x.reshape(batch * seq, d_model)
    for l in range(norm1_g.shape[0]):
        x2 = _layer(x2, seq, norm1_g[l], w_in[l], q_norm_g[l], k_norm_g[l], pool_w[l],
                    pool_b[l], pool_scale[l], w_out[l], norm2_g[l], w_gate[l], w_up[l],
                    w_down[l])
    return x2.reshape(batch, seq, d_model)
```

```python
import functools
import math

import jax
import jax.numpy as jnp
import numpy as np
from jax import lax
from jax.experimental import pallas as pl
from jax.experimental.pallas import tpu as pltpu

D_MODEL = 1024
HEAD_DIM = 64
N_Q_HEADS = 8
N_KV_HEADS = 2
Q_PER_KV = N_Q_HEADS // N_KV_HEADS
ATTN_WIDTH = N_Q_HEADS * HEAD_DIM
KV_WIDTH = N_KV_HEADS * HEAD_DIM
GROUP_WIDTH = Q_PER_KV * HEAD_DIM
ROPE_PAIRS = HEAD_DIM // 4
ROPE_THETA = 10000.0
POOL_WINDOWS = (2, 4, 8, 16)
N_POOL_GROUPS = len(POOL_WINDOWS)
POOL_WIDTH = D_MODEL - ATTN_WIDTH
POOL_GROUP_DIM = POOL_WIDTH // N_POOL_GROUPS
IN_WIDTH = ATTN_WIDTH + 2 * KV_WIDTH + POOL_WIDTH
GRID_W = 64
EPS = 1e-6

LANES = 128
BF16_SUBLANES = 16
POOL_HALO = 8
VMEM_LIMIT_BYTES = 56 * 1024 * 1024

ROW_TILE = 512
INPROJ_ROWS = 1024
Q_TILE = 128
ATTN_SLOTS_PER_STEP = 4
N_SCORE_BUFS = 4
NEXT_TILES = 2
FF_CHUNK = 512
INPROJ_CHUNK = 256

F32 = jnp.float32
BF16 = jnp.bfloat16


def _lane_iota(shape):
    return lax.broadcasted_iota(jnp.int32, shape, len(shape) - 1)


def _rope_tables(seq):
    t = np.arange(seq)
    row = (t // GRID_W).astype(np.float64)
    col = (t % GRID_W).astype(np.float64)
    inv_freq = ROPE_THETA ** (-np.arange(ROPE_PAIRS, dtype=np.float64) / ROPE_PAIRS)
    ang_row = row[:, None] * inv_freq[None, :]
    ang_col = col[:, None] * inv_freq[None, :]
    ang = np.concatenate([ang_row, ang_row, ang_col, ang_col], axis=1)
    sign = np.concatenate([-np.ones(ROPE_PAIRS), np.ones(ROPE_PAIRS)] * 2)
    cos = np.tile(np.cos(ang), (1, 2)).astype(np.float32)
    sin = np.tile(np.sin(ang) * sign[None, :], (1, 2)).astype(np.float32)
    return cos, sin


def _segment_ones(width):
    idx = np.arange(width) // HEAD_DIM
    return (idx[:, None] == idx[None, :]).astype(np.float32)


def _head_sumsq(x, seg):
    sq = x * x
    hi = sq.astype(BF16)
    lo = (sq - hi.astype(F32)).astype(BF16)
    return (jnp.dot(hi, seg, preferred_element_type=F32)
            + jnp.dot(lo, seg, preferred_element_type=F32))


def _rope_partner(x):
    lane = _lane_iota(x.shape)
    fwd = pltpu.roll(x, LANES - ROPE_PAIRS, axis=1)
    bwd = pltpu.roll(x, ROPE_PAIRS, axis=1)
    return jnp.where((lane & ROPE_PAIRS) == 0, fwd, bwd)


def _norm_rope(x, sumsq, gain, cos, sin):
    y = x * lax.rsqrt(sumsq * (1.0 / HEAD_DIM) + EPS) * gain
    return y * cos + _rope_partner(y) * sin


def _inproj_kernel(x_ref, g1_ref, win_ref, qg_ref, kg_ref, cos_ref, sin_ref, seg_ref,
                   q_ref, krep_ref, vext_ref, u_ref, proja_ref, projb_ref, wbf_ref,
                   *, q_scale, n_tiles, seq):
    step = pl.program_id(0)
    rows = x_ref.shape[0]
    kv0 = ATTN_WIDTH
    u0 = ATTN_WIDTH + 2 * KV_WIDTH
    pos0 = pl.multiple_of((jnp.maximum(step - 1, 0) % (seq // rows)) * rows, rows)

    def rope_tables():
        return cos_ref[pl.ds(pos0, rows), :], sin_ref[pl.ds(pos0, rows), :]

    def head_pair_gain(g_ref):
        g = g_ref[...]
        return jnp.concatenate([g] * (LANES // HEAD_DIM), axis=1)

    def normed_input():
        x = x_ref[...]
        ms = jnp.mean(x * x, axis=-1, keepdims=True)
        return (x * lax.rsqrt(ms + EPS) * g1_ref[...]).astype(BF16)

    def finish_q(src_ref, g):
        qh = src_ref[:, g * GROUP_WIDTH:(g + 1) * GROUP_WIDTH]
        ss = _head_sumsq(qh, seg_ref[...])
        for j in range(GROUP_WIDTH // LANES):
            sl = slice(j * LANES, (j + 1) * LANES)
            qr = _norm_rope(qh[:, sl], ss[:, sl], head_pair_gain(qg_ref), *rope_tables()) * q_scale
            q_ref[0, g, :, sl] = qr.astype(BF16)

    def finish_kv(src_ref):
        k = src_ref[:, kv0:kv0 + KV_WIDTH]
        kss = _head_sumsq(k, seg_ref[:KV_WIDTH, :KV_WIDTH])
        kr = _norm_rope(k, kss, head_pair_gain(kg_ref), *rope_tables())
        first = _lane_iota(kr.shape) < HEAD_DIM
        kswap = pltpu.roll(kr, HEAD_DIM, axis=1)
        krep_ref[0, 0] = jnp.where(first, kr, kswap).astype(BF16)
        krep_ref[0, 1] = jnp.where(first, kswap, kr).astype(BF16)
        v = src_ref[:, kv0 + KV_WIDTH:kv0 + 2 * KV_WIDTH]
        vswap = pltpu.roll(v, HEAD_DIM, axis=1)
        vext_ref[0, 0] = jnp.where(first, v, 1.0).astype(BF16)
        vext_ref[0, 1] = jnp.where(first, vswap, 1.0).astype(BF16)

    def finish_u(src_ref, half):
        cs = slice(half * (POOL_WIDTH // 2), (half + 1) * (POOL_WIDTH // 2))
        u_ref[:, cs] = src_ref[:, u0 + cs.start:u0 + cs.stop]

    def finish_pieces(src_ref):
        return [functools.partial(finish_q, src_ref, 0), functools.partial(finish_q, src_ref, 1),
                functools.partial(finish_kv, src_ref),
                functools.partial(finish_u, src_ref, 0), functools.partial(finish_u, src_ref, 1)]

    def project(dst_ref, between=()):
        h = normed_input()
        for c in range(IN_WIDTH // INPROJ_CHUNK):
            cs = slice(c * INPROJ_CHUNK, (c + 1) * INPROJ_CHUNK)
            dst_ref[:, cs] = jnp.dot(h, wbf_ref[:, cs], preferred_element_type=F32)
            if c < len(between):
                between[c]()

    even = step % 2 == 0
    assert n_tiles % 2 == 0

    @pl.when(step == 0)
    def _():
        wbf_ref[...] = win_ref[...].astype(BF16)
        project(proja_ref)

    @pl.when(jnp.logical_and(even, jnp.logical_and(step > 0, step < n_tiles)))
    def _():
        project(proja_ref, finish_pieces(projb_ref))

    @pl.when(jnp.logical_not(even))
    def _():
        project(projb_ref, finish_pieces(proja_ref))

    @pl.when(step == n_tiles)
    def _():
        for piece in finish_pieces(projb_ref):
            piece()


def _attention_kernel(q_ref, krep_ref, vext_ref, qn_ref, kn_ref,
                      wf0_ref, wf1_ref, wf2_ref, wf3_ref,
                      o_ref, wb0_ref, wb1_ref, wb2_ref, wb3_ref, *scratch, seq):
    for wf_ref, wb_ref in ((wf0_ref, wb0_ref), (wf1_ref, wb1_ref),
                           (wf2_ref, wb2_ref), (wf3_ref, wb3_ref)):
        wb_ref[...] = wf_ref[...].astype(BF16)

    tiles_per_group = seq // Q_TILE
    n_tiles = N_KV_HEADS * tiles_per_group
    lane_o = _lane_iota((Q_TILE, LANES))

    def tile_pos(t):
        g = t // tiles_per_group
        r0 = pl.multiple_of((t % tiles_per_group) * Q_TILE, Q_TILE)
        return g, r0

    def scores(t, s_ref, mv_ref):
        if isinstance(t, int) and t >= n_tiles:
            assert t - n_tiles < NEXT_TILES
            qt = qn_ref[0, 0, (t - n_tiles) * Q_TILE:(t - n_tiles + 1) * Q_TILE, :]
            keys = kn_ref[0, 0]
        else:
            g, r0 = tile_pos(t)
            qt = q_ref[0, g, pl.ds(r0, Q_TILE), :]
            keys = krep_ref[0, g]
        halves = [qt[:, j * LANES:(j + 1) * LANES] for j in range(GROUP_WIDTH // LANES)]
        zero = jnp.zeros_like(halves[0])
        per_half = LANES // HEAD_DIM
        qm = jnp.concatenate(
            [jnp.where((lane_o // HEAD_DIM) == hh % per_half, halves[hh // per_half], zero)
             for hh in range(Q_PER_KV)], axis=0)
        s = lax.dot_general(qm, keys, (((1,), (1,)), ((), ())),
                            preferred_element_type=F32)
        s_ref[...] = s
        mv_ref[...] = functools.reduce(
            jnp.maximum, [s[:, j * LANES:(j + 1) * LANES] for j in range(seq // LANES)])

    def row_max(mv_ref, m_ref):
        m_ref[...] = jnp.max(mv_ref[...], axis=-1, keepdims=True)

    def output(t, s_ref, m_ref, oe_ref):
        g, _ = tile_pos(t)
        p = jnp.exp2(s_ref[...] - m_ref[...]).astype(BF16)
        oe_ref[...] = jnp.dot(p, vext_ref[0, g], preferred_element_type=F32)

    def finalize(t, oe_ref):
        g, r0 = tile_pos(t)
        low = lane_o < HEAD_DIM
        for j in range(Q_PER_KV // 2):
            even = oe_ref[2 * j * Q_TILE:(2 * j + 1) * Q_TILE, :]
            odd = oe_ref[(2 * j + 1) * Q_TILE:(2 * j + 2) * Q_TILE, :]
            even_sw = pltpu.roll(even, HEAD_DIM, axis=1)
            odd_sw = pltpu.roll(odd, HEAD_DIM, axis=1)
            pair = jnp.where(low, even, odd_sw) / jnp.where(low, even_sw, odd)
            o_ref[0, g, pl.ds(r0, Q_TILE), j * LANES:(j + 1) * LANES] = pair.astype(BF16)

    n_buf = N_SCORE_BUFS
    assert n_tiles % n_buf == 0 and tiles_per_group >= NEXT_TILES and len(scratch) == 4 * n_buf
    s_refs, mv_refs = scratch[:n_buf], scratch[n_buf:2 * n_buf]
    m_refs, oe_refs = scratch[2 * n_buf:3 * n_buf], scratch[3 * n_buf:]

    def slot(t, k, first=False):
        row_max(mv_refs[(k + 1) % n_buf], m_refs[(k + 1) % n_buf])
        if not first:
            finalize(t - 1, oe_refs[(k - 1) % n_buf])
        scores(t + NEXT_TILES, s_refs[(k + NEXT_TILES) % n_buf], mv_refs[(k + NEXT_TILES) % n_buf])
        output(t, s_refs[k % n_buf], m_refs[k % n_buf], oe_refs[k % n_buf])

    @pl.when(pl.program_id(0) == 0)
    def _():
        for t in range(NEXT_TILES):
            scores(t, s_refs[t], mv_refs[t])
        row_max(mv_refs[0], m_refs[0])

    n_loop = (n_tiles - NEXT_TILES - 1) // ATTN_SLOTS_PER_STEP * ATTN_SLOTS_PER_STEP
    assert n_loop > 0 and ATTN_SLOTS_PER_STEP % n_buf == 0
    slot(0, 0, first=True)

    def body(jj, carry):
        for i in range(ATTN_SLOTS_PER_STEP):
            slot(1 + jj * ATTN_SLOTS_PER_STEP + i, (1 + i) % n_buf)
        return carry

    lax.fori_loop(0, n_loop // ATTN_SLOTS_PER_STEP, body, 0)

    for t in range(1 + n_loop, n_tiles):
        slot(t, t % n_buf)
    finalize(n_tiles - 1, oe_refs[(n_tiles - 1) % n_buf])


def _mix_ffn_kernel(x_ref, a_ref, u_ref, uprev_ref, unext_ref, pw_ref, pb_ref, ps_ref,
                    wout_ref, g2_ref, wg_ref, wu_ref, wd_ref, o_ref,
                    ext_ref, mixp_ref, x1_ref, h_ref, *, seq, d_ff, n_tiles):
    rows = x_ref.shape[0]
    tiles_per_seq = seq // rows
    step = pl.program_id(0)

    st = jnp.minimum(step, n_tiles - 1) % tiles_per_seq

    def stage_windows():
        ext_ref[0:POOL_HALO, :] = jnp.where(st > 0, uprev_ref[...], 0.0)
        ext_ref[POOL_HALO:POOL_HALO + rows, :] = u_ref[...]
        ext_ref[POOL_HALO + rows:, :] = jnp.where(st < tiles_per_seq - 1, unext_ref[...], 0.0)

    def pool_group(g):
        n_ext = rows + 2 * POOL_HALO
        w = POOL_WINDOWS[g]
        assert w // 2 <= POOL_HALO
        cs = slice(g * POOL_GROUP_DIM, (g + 1) * POOL_GROUP_DIM)
        ext = ext_ref[:, cs]
        f = ext
        k = 1
        while k < w // 2:
            f = f + pltpu.roll(f, n_ext - k, axis=0)
            k *= 2
        both = f + pltpu.roll(f, w // 2, axis=0)
        win = both[POOL_HALO:POOL_HALO + rows]

        def edge_mean(r0):
            t = st * rows + r0 + lax.broadcasted_iota(jnp.int32, (POOL_HALO, LANES), 0)
            lo = jnp.maximum(t - w // 2, 0)
            hi = jnp.minimum(t - w // 2 + w, seq)
            return win[r0:r0 + POOL_HALO] / (hi - lo).astype(F32)

        mean = jnp.concatenate([edge_mean(0), win[POOL_HALO:rows - POOL_HALO] * (1.0 / w),
                                edge_mean(rows - POOL_HALO)], axis=0)
        pooled = (mean - ext[POOL_HALO:POOL_HALO + rows]).astype(BF16)
        y = jnp.dot(pooled, pw_ref[g].astype(BF16), preferred_element_type=F32) + pb_ref[g]
        mixp_ref[:, cs] = (y * ps_ref[:, cs]).astype(BF16)

    def project_out():
        mix = jnp.concatenate([a_ref[0, g] for g in range(N_KV_HEADS)] + [mixp_ref[...]],
                              axis=1)
        x1 = x_ref[...] + jnp.dot(mix, wout_ref[...], preferred_element_type=F32)
        ms = jnp.mean(x1 * x1, axis=-1, keepdims=True)
        x1_ref[...] = x1
        return (x1 * lax.rsqrt(ms + EPS) * g2_ref[...]).astype(BF16)

    bounds = [(c0, min(c0 + FF_CHUNK, d_ff)) for c0 in range(0, d_ff, FF_CHUNK)]

    @pl.when(step == 0)
    def _():
        stage_windows()
        for g in range(N_POOL_GROUPS):
            pool_group(g)
        h_ref[...] = project_out()

    @pl.when(step > 0)
    def _():
        stage_windows()
        h = h_ref[...]
        acc = x1_ref[...]
        h_next = []
        between = [[functools.partial(pool_group, g)] for g in range(N_POOL_GROUPS)]
        between += [[lambda: h_next.append(project_out())], []]
        assert len(between) == len(bounds)
        for c, (c0, c1) in enumerate(bounds):
            gate = jnp.dot(h, wg_ref[:, c0:c1], preferred_element_type=F32)
            up = jnp.dot(h, wu_ref[:, c0:c1], preferred_element_type=F32)
            if c == len(bounds) - 1:
                h_ref[...] = h_next[0]
            act = (gate * jax.nn.sigmoid(gate) * up).astype(BF16)
            acc = acc + jnp.dot(act, wd_ref[c0:c1, :], preferred_element_type=F32)
            for piece in between[c]:
                piece()
        o_ref[...] = acc


def _const_spec(shape):
    nd = len(shape)
    return pl.BlockSpec(shape, lambda *_: (0,) * nd)


def _layer(x2, seq, norm1_g, w_in, q_norm_g, k_norm_g, pool_w, pool_b, pool_scale,
           w_out, norm2_g, w_gate, w_up, w_down):
    n_tok = x2.shape[0]
    batch = n_tok // seq
    d_ff = w_gate.shape[1]
    assert seq % ROW_TILE == 0 and seq % Q_TILE == 0 and ROW_TILE % POOL_HALO == 0
    assert seq % INPROJ_ROWS == 0
    n_row_tiles = n_tok // ROW_TILE
    n_in_tiles = n_tok // INPROJ_ROWS

    cos, sin = _rope_tables(seq)
    seg = jnp.asarray(_segment_ones(256), BF16)
    q_scale = HEAD_DIM ** -0.5 * math.log2(math.e)

    params = pltpu.CompilerParams(dimension_semantics=("arbitrary",),
                                  vmem_limit_bytes=VMEM_LIMIT_BYTES)

    def grouped_rows(width, rows, tile):
        per_seq = seq // rows
        return pl.BlockSpec((1, N_KV_HEADS, rows, width),
                            lambda i: (tile(i) // per_seq, 0, tile(i) % per_seq, 0))

    def lead_tile(n):
        return lambda i: jnp.minimum(i, n - 1)

    mix_tile = lead_tile(n_row_tiles)
    in_tile = lead_tile(n_in_tiles)

    def lag_tile(i):
        return jnp.maximum(i - 1, 0)

    def grouped_seq(width):
        return pl.BlockSpec((1, N_KV_HEADS, seq, width), lambda b: (b, 0, 0, 0))

    q, krep, vext, u = pl.pallas_call(
        functools.partial(_inproj_kernel, q_scale=q_scale, n_tiles=n_in_tiles, seq=seq),
        grid=(n_in_tiles + 1,),
        in_specs=[
            pl.BlockSpec((INPROJ_ROWS, D_MODEL), lambda i: (in_tile(i), 0)),
            _const_spec((1, D_MODEL)),
            _const_spec((D_MODEL, IN_WIDTH)),
            _const_spec((1, HEAD_DIM)),
            _const_spec((1, HEAD_DIM)),
            _const_spec((seq, LANES)),
            _const_spec((seq, LANES)),
            _const_spec((256, 256)),
        ],
        out_specs=[
            grouped_rows(GROUP_WIDTH, INPROJ_ROWS, lag_tile),
            grouped_rows(LANES, INPROJ_ROWS, lag_tile),
            grouped_rows(LANES, INPROJ_ROWS, lag_tile),
            pl.BlockSpec((INPROJ_ROWS, POOL_WIDTH), lambda i: (lag_tile(i), 0)),
        ],
        out_shape=[
            jax.ShapeDtypeStruct((batch, N_KV_HEADS, seq, GROUP_WIDTH), BF16),
            jax.ShapeDtypeStruct((batch, N_KV_HEADS, seq, LANES), BF16),
            jax.ShapeDtypeStruct((batch, N_KV_HEADS, seq, LANES), BF16),
            jax.ShapeDtypeStruct((n_tok, POOL_WIDTH), F32),
        ],
        scratch_shapes=[pltpu.VMEM((INPROJ_ROWS, IN_WIDTH), F32),
                        pltpu.VMEM((INPROJ_ROWS, IN_WIDTH), F32),
                        pltpu.VMEM((D_MODEL, IN_WIDTH), BF16)],
        compiler_params=params,
        name="inproj",
    )(x2, norm1_g.astype(F32)[None, :], w_in.astype(F32),
      q_norm_g.astype(F32)[None, :], k_norm_g.astype(F32)[None, :],
      jnp.asarray(cos), jnp.asarray(sin), seg)

    late_weights = [w_out, w_gate, w_up, w_down]
    for w in late_weights:
        assert w.shape[0] % (batch * BF16_SUBLANES) == 0

    def next_lead(rows, width):
        return pl.BlockSpec((1, 1, rows, width),
                            lambda b: (jnp.minimum(b + 1, batch - 1), 0, 0, 0))

    def row_slab(w):
        return pl.BlockSpec((w.shape[0] // batch, w.shape[1]), lambda b: (b, 0))

    attn, w_out_b, w_gate_b, w_up_b, w_down_b = pl.pallas_call(
        functools.partial(_attention_kernel, seq=seq),
        grid=(batch,),
        in_specs=([grouped_seq(GROUP_WIDTH), grouped_seq(LANES), grouped_seq(LANES),
                   next_lead(NEXT_TILES * Q_TILE, GROUP_WIDTH), next_lead(seq, LANES)]
                  + [row_slab(w) for w in late_weights]),
        out_specs=[grouped_seq(GROUP_WIDTH)] + [row_slab(w) for w in late_weights],
        out_shape=([jax.ShapeDtypeStruct((batch, N_KV_HEADS, seq, GROUP_WIDTH), BF16)]
                   + [jax.ShapeDtypeStruct(w.shape, BF16) for w in late_weights]),
        scratch_shapes=([pltpu.VMEM((Q_PER_KV * Q_TILE, seq), F32)] * N_SCORE_BUFS
                        + [pltpu.VMEM((Q_PER_KV * Q_TILE, LANES), F32)] * N_SCORE_BUFS
                        + [pltpu.VMEM((Q_PER_KV * Q_TILE, 1), F32)] * N_SCORE_BUFS
                        + [pltpu.VMEM((Q_PER_KV * Q_TILE, LANES), F32)] * N_SCORE_BUFS),
        compiler_params=params,
        name="attention",
    )(q, krep, vext, q, krep, *[w.astype(F32) for w in late_weights])

    halo_blocks = ROW_TILE // POOL_HALO
    last_halo = n_tok // POOL_HALO - 1
    out = pl.pallas_call(
        functools.partial(_mix_ffn_kernel, seq=seq, d_ff=d_ff, n_tiles=n_row_tiles),
        grid=(n_row_tiles + 1,),
        in_specs=[
            pl.BlockSpec((ROW_TILE, D_MODEL), lambda i: (mix_tile(i), 0)),
            grouped_rows(GROUP_WIDTH, ROW_TILE, mix_tile),
            pl.BlockSpec((ROW_TILE, POOL_WIDTH), lambda i: (mix_tile(i), 0)),
            pl.BlockSpec((POOL_HALO, POOL_WIDTH),
                         lambda i: (jnp.maximum(mix_tile(i) * halo_blocks - 1, 0), 0)),
            pl.BlockSpec((POOL_HALO, POOL_WIDTH),
                         lambda i: (jnp.minimum((mix_tile(i) + 1) * halo_blocks, last_halo), 0)),
            _const_spec((N_POOL_GROUPS, POOL_GROUP_DIM, POOL_GROUP_DIM)),
            _const_spec((N_POOL_GROUPS, 1, POOL_GROUP_DIM)),
            _const_spec((1, POOL_WIDTH)),
            _const_spec((D_MODEL, D_MODEL)),
            _const_spec((1, D_MODEL)),
            _const_spec((D_MODEL, d_ff)),
            _const_spec((D_MODEL, d_ff)),
            _const_spec((d_ff, D_MODEL)),
        ],
        out_specs=pl.BlockSpec((ROW_TILE, D_MODEL), lambda i: (jnp.maximum(i - 1, 0), 0)),
        out_shape=jax.ShapeDtypeStruct((n_tok, D_MODEL), F32),
        scratch_shapes=[pltpu.VMEM((ROW_TILE + 2 * POOL_HALO, POOL_WIDTH), F32),
                        pltpu.VMEM((ROW_TILE, POOL_WIDTH), BF16),
                        pltpu.VMEM((ROW_TILE, D_MODEL), F32), pltpu.VMEM((ROW_TILE, D_MODEL), BF16)],
        compiler_params=params,
        name="mix_ffn",
    )(x2, attn, u, u, u, pool_w.astype(F32), pool_b.astype(F32)[:, None, :],
      pool_scale.astype(F32)[None, :], w_out_b, norm2_g.astype(F32)[None, :],
      w_gate_b, w_up_b, w_down_b)
    return out


def kernel(x, norm1_g, w_in, q_norm_g, k_norm_g, pool_w, pool_b, pool_scale,
           w_out, norm2_g, w_gate, w_up, w_down):
    batch, seq, d_model = x.shape
    assert d_model == D_MODEL and w_in.shape[-1] == IN_WIDTH
    x2 = x.reshape(batch * seq, d_model)
    for l in range(norm1_g.shape[0]):
        x2 = _layer(x2, seq, norm1_g[l], w_in[l], q_norm_g[l], k_norm_g[l], pool_w[l],
                    pool_b[l], pool_scale[l], w_out[l], norm2_g[l], w_gate[l], w_up[l],
                    w_down[l])
    return x2.reshape(batch, seq, d_model)
```

```python
import functools
import math

import jax
import jax.numpy as jnp
import numpy as np
from jax import lax
from jax.experimental import pallas as pl
from jax.experimental.pallas import tpu as pltpu

D_MODEL = 1024
HEAD_DIM = 64
N_Q_HEADS = 8
N_KV_HEADS = 2
Q_PER_KV = N_Q_HEADS // N_KV_HEADS
ATTN_WIDTH = N_Q_HEADS * HEAD_DIM
KV_WIDTH = N_KV_HEADS * HEAD_DIM
GROUP_WIDTH = Q_PER_KV * HEAD_DIM
ROPE_PAIRS = HEAD_DIM // 4
ROPE_THETA = 10000.0
POOL_WINDOWS = (2, 4, 8, 16)
N_POOL_GROUPS = len(POOL_WINDOWS)
POOL_WIDTH = D_MODEL - ATTN_WIDTH
POOL_GROUP_DIM = POOL_WIDTH // N_POOL_GROUPS
IN_WIDTH = ATTN_WIDTH + 2 * KV_WIDTH + POOL_WIDTH
GRID_W = 64
EPS = 1e-6

LANES = 128
BF16_SUBLANES = 16
POOL_HALO = 8
VMEM_LIMIT_BYTES = 56 * 1024 * 1024

ROW_TILE = 512
INPROJ_ROWS = 1024
Q_TILE = 128
ATTN_SLOTS_PER_STEP = 4
N_SCORE_BUFS = 4
NEXT_TILES = 2
FF_CHUNK = 512
INPROJ_CHUNK = 256

F32 = jnp.float32
BF16 = jnp.bfloat16


def _lane_iota(shape):
    return lax.broadcasted_iota(jnp.int32, shape, len(shape) - 1)


def _rope_tables(seq):
    t = np.arange(seq)
    row = (t // GRID_W).astype(np.float64)
    col = (t % GRID_W).astype(np.float64)
    inv_freq = ROPE_THETA ** (-np.arange(ROPE_PAIRS, dtype=np.float64) / ROPE_PAIRS)
    ang_row = row[:, None] * inv_freq[None, :]
    ang_col = col[:, None] * inv_freq[None, :]
    ang = np.concatenate([ang_row, ang_row, ang_col, ang_col], axis=1)
    sign = np.concatenate([-np.ones(ROPE_PAIRS), np.ones(ROPE_PAIRS)] * 2)
    cos = np.tile(np.cos(ang), (1, 2)).astype(np.float32)
    sin = np.tile(np.sin(ang) * sign[None, :], (1, 2)).astype(np.float32)
    return cos, sin


def _segment_ones(width):
    idx = np.arange(width) // HEAD_DIM
    return (idx[:, None] == idx[None, :]).astype(np.float32)


def _head_sumsq(x, seg):
    sq = x * x
    hi = sq.astype(BF16)
    lo = (sq - hi.astype(F32)).astype(BF16)
    return (jnp.dot(hi, seg, preferred_element_type=F32)
            + jnp.dot(lo, seg, preferred_element_type=F32))


def _rope_partner(x):
    lane = _lane_iota(x.shape)
    fwd = pltpu.roll(x, LANES - ROPE_PAIRS, axis=1)
    bwd = pltpu.roll(x, ROPE_PAIRS, axis=1)
    return jnp.where((lane & ROPE_PAIRS) == 0, fwd, bwd)


def _norm_rope(x, sumsq, gain, cos, sin):
    y = x * lax.rsqrt(sumsq * (1.0 / HEAD_DIM) + EPS) * gain
    return y * cos + _rope_partner(y) * sin


def _inproj_kernel(x_ref, g1_ref, win_ref, qg_ref, kg_ref, cos_ref, sin_ref, seg_ref,
                   q_ref, krep_ref, vext_ref, u_ref, proja_ref, projb_ref, wbf_ref,
                   *, q_scale, n_tiles, seq):
    step = pl.program_id(0)
    rows = x_ref.shape[0]
    kv0 = ATTN_WIDTH
    u0 = ATTN_WIDTH + 2 * KV_WIDTH
    pos0 = pl.multiple_of((jnp.maximum(step - 1, 0) % (seq // rows)) * rows, rows)

    def rope_tables():
        return cos_ref[pl.ds(pos0, rows), :], sin_ref[pl.ds(pos0, rows), :]

    def head_pair_gain(g_ref):
        g = g_ref[...]
        return jnp.concatenate([g] * (LANES // HEAD_DIM), axis=1)

    def normed_input():
        x = x_ref[...]
        ms = jnp.mean(x * x, axis=-1, keepdims=True)
        return (x * lax.rsqrt(ms + EPS) * g1_ref[...]).astype(BF16)

    def finish_q(src_ref, g):
        qh = src_ref[:, g * GROUP_WIDTH:(g + 1) * GROUP_WIDTH]
        ss = _head_sumsq(qh, seg_ref[...])
        for j in range(GROUP_WIDTH // LANES):
            sl = slice(j * LANES, (j + 1) * LANES)
            qr = _norm_rope(qh[:, sl], ss[:, sl], head_pair_gain(qg_ref), *rope_tables()) * q_scale
            q_ref[0, g, :, sl] = qr.astype(BF16)

    def finish_kv(src_ref):
        k = src_ref[:, kv0:kv0 + KV_WIDTH]
        kss = _head_sumsq(k, seg_ref[:KV_WIDTH, :KV_WIDTH])
        kr = _norm_rope(k, kss, head_pair_gain(kg_ref), *rope_tables())
        first = _lane_iota(kr.shape) < HEAD_DIM
        kswap = pltpu.roll(kr, HEAD_DIM, axis=1)
        krep_ref[0, 0] = jnp.where(first, kr, kswap).astype(BF16)
        krep_ref[0, 1] = jnp.where(first, kswap, kr).astype(BF16)
        v = src_ref[:, kv0 + KV_WIDTH:kv0 + 2 * KV_WIDTH]
        vswap = pltpu.roll(v, HEAD_DIM, axis=1)
        vext_ref[0, 0] = jnp.where(first, v, 1.0).astype(BF16)
        vext_ref[0, 1] = jnp.where(first, vswap, 1.0).astype(BF16)

    def finish_pieces(src_ref):
        return [functools.partial(finish_q, src_ref, 0), functools.partial(finish_q, src_ref, 1),
                functools.partial(finish_kv, src_ref)]

    def project(dst_ref, between=()):
        assert u0 % INPROJ_CHUNK == 0
        h = normed_input()
        for c in range(IN_WIDTH // INPROJ_CHUNK):
            c0 = c * INPROJ_CHUNK
            res = jnp.dot(h, wbf_ref[:, c0:c0 + INPROJ_CHUNK], preferred_element_type=F32)
            if c0 < u0:
                dst_ref[:, c0:c0 + INPROJ_CHUNK] = res
            else:
                u_ref[:, c0 - u0:c0 - u0 + INPROJ_CHUNK] = res
            if c < len(between):
                between[c]()

    even = step % 2 == 0
    assert n_tiles % 2 == 0

    @pl.when(step == 0)
    def _():
        wbf_ref[...] = win_ref[...].astype(BF16)
        project(proja_ref)

    @pl.when(jnp.logical_and(even, jnp.logical_and(step > 0, step < n_tiles)))
    def _():
        project(proja_ref, finish_pieces(projb_ref))

    @pl.when(jnp.logical_not(even))
    def _():
        project(projb_ref, finish_pieces(proja_ref))

    @pl.when(step == n_tiles)
    def _():
        for piece in finish_pieces(projb_ref):
            piece()


def _attention_kernel(q_ref, krep_ref, vext_ref, qn_ref, kn_ref,
                      wf0_ref, wf1_ref, wf2_ref, wf3_ref,
                      o_ref, wb0_ref, wb1_ref, wb2_ref, wb3_ref, *scratch, seq):
    for wf_ref, wb_ref in ((wf0_ref, wb0_ref), (wf1_ref, wb1_ref),
                           (wf2_ref, wb2_ref), (wf3_ref, wb3_ref)):
        wb_ref[...] = wf_ref[...].astype(BF16)

    tiles_per_group = seq // Q_TILE
    n_tiles = N_KV_HEADS * tiles_per_group
    lane_o = _lane_iota((Q_TILE, LANES))

    def tile_pos(t):
        g = t // tiles_per_group
        r0 = pl.multiple_of((t % tiles_per_group) * Q_TILE, Q_TILE)
        return g, r0

    def scores(t, s_ref, mv_ref):
        if isinstance(t, int) and t >= n_tiles:
            assert t - n_tiles < NEXT_TILES
            qt = qn_ref[0, 0, (t - n_tiles) * Q_TILE:(t - n_tiles + 1) * Q_TILE, :]
            keys = kn_ref[0, 0]
        else:
            g, r0 = tile_pos(t)
            qt = q_ref[0, g, pl.ds(r0, Q_TILE), :]
            keys = krep_ref[0, g]
        halves = [qt[:, j * LANES:(j + 1) * LANES] for j in range(GROUP_WIDTH // LANES)]
        zero = jnp.zeros_like(halves[0])
        per_half = LANES // HEAD_DIM
        qm = jnp.concatenate(
            [jnp.where((lane_o // HEAD_DIM) == hh % per_half, halves[hh // per_half], zero)
             for hh in range(Q_PER_KV)], axis=0)
        s = lax.dot_general(qm, keys, (((1,), (1,)), ((), ())),
                            preferred_element_type=F32)
        s_ref[...] = s
        mv_ref[...] = functools.reduce(
            jnp.maximum, [s[:, j * LANES:(j + 1) * LANES] for j in range(seq // LANES)])

    def row_max(mv_ref, m_ref):
        m_ref[...] = jnp.max(mv_ref[...], axis=-1, keepdims=True)

    def output(t, s_ref, m_ref, oe_ref):
        g, _ = tile_pos(t)
        p = jnp.exp2(s_ref[...] - m_ref[...]).astype(BF16)
        oe_ref[...] = jnp.dot(p, vext_ref[0, g], preferred_element_type=F32)

    def finalize(t, oe_ref):
        g, r0 = tile_pos(t)
        low = lane_o < HEAD_DIM
        for j in range(Q_PER_KV // 2):
            even = oe_ref[2 * j * Q_TILE:(2 * j + 1) * Q_TILE, :]
            odd = oe_ref[(2 * j + 1) * Q_TILE:(2 * j + 2) * Q_TILE, :]
            even_sw = pltpu.roll(even, HEAD_DIM, axis=1)
            odd_sw = pltpu.roll(odd, HEAD_DIM, axis=1)
            pair = jnp.where(low, even, odd_sw) / jnp.where(low, even_sw, odd)
            o_ref[0, g, pl.ds(r0, Q_TILE), j * LANES:(j + 1) * LANES] = pair.astype(BF16)

    n_buf = N_SCORE_BUFS
    assert n_tiles % n_buf == 0 and tiles_per_group >= NEXT_TILES and len(scratch) == 4 * n_buf
    s_refs, mv_refs = scratch[:n_buf], scratch[n_buf:2 * n_buf]
    m_refs, oe_refs = scratch[2 * n_buf:3 * n_buf], scratch[3 * n_buf:]

    def slot(t, k, first=False):
        row_max(mv_refs[(k + 1) % n_buf], m_refs[(k + 1) % n_buf])
        if not first:
            finalize(t - 1, oe_refs[(k - 1) % n_buf])
        scores(t + NEXT_TILES, s_refs[(k + NEXT_TILES) % n_buf], mv_refs[(k + NEXT_TILES) % n_buf])
        output(t, s_refs[k % n_buf], m_refs[k % n_buf], oe_refs[k % n_buf])

    @pl.when(pl.program_id(0) == 0)
    def _():
        for t in range(NEXT_TILES):
            scores(t, s_refs[t], mv_refs[t])
        row_max(mv_refs[0], m_refs[0])

    n_loop = (n_tiles - NEXT_TILES - 1) // ATTN_SLOTS_PER_STEP * ATTN_SLOTS_PER_STEP
    assert n_loop > 0 and ATTN_SLOTS_PER_STEP % n_buf == 0
    slot(0, 0, first=True)

    def body(jj, carry):
        for i in range(ATTN_SLOTS_PER_STEP):
            slot(1 + jj * ATTN_SLOTS_PER_STEP + i, (1 + i) % n_buf)
        return carry

    lax.fori_loop(0, n_loop // ATTN_SLOTS_PER_STEP, body, 0)

    for t in range(1 + n_loop, n_tiles):
        slot(t, t % n_buf)
    finalize(n_tiles - 1, oe_refs[(n_tiles - 1) % n_buf])


def _mix_ffn_kernel(x_ref, a_ref, u_ref, uprev_ref, unext_ref, pw_ref, pb_ref, ps_ref,
                    wout_ref, g2_ref, wg_ref, wu_ref, wd_ref, o_ref,
                    ext_ref, mixp_ref, x1_ref, h_ref, *, seq, d_ff, n_tiles):
    rows = x_ref.shape[0]
    tiles_per_seq = seq // rows
    step = pl.program_id(0)

    st = jnp.minimum(step, n_tiles - 1) % tiles_per_seq

    def stage_windows():
        ext_ref[0:POOL_HALO, :] = jnp.where(st > 0, uprev_ref[...], 0.0)
        ext_ref[POOL_HALO:POOL_HALO + rows, :] = u_ref[...]
        ext_ref[POOL_HALO + rows:, :] = jnp.where(st < tiles_per_seq - 1, unext_ref[...], 0.0)

    def pool_group(g):
        n_ext = rows + 2 * POOL_HALO
        w = POOL_WINDOWS[g]
        assert w // 2 <= POOL_HALO
        cs = slice(g * POOL_GROUP_DIM, (g + 1) * POOL_GROUP_DIM)
        ext = ext_ref[:, cs]
        f = ext
        k = 1
        while k < w // 2:
            f = f + pltpu.roll(f, n_ext - k, axis=0)
            k *= 2
        both = f + pltpu.roll(f, w // 2, axis=0)
        win = both[POOL_HALO:POOL_HALO + rows]

        def edge_mean(r0):
            t = st * rows + r0 + lax.broadcasted_iota(jnp.int32, (POOL_HALO, LANES), 0)
            lo = jnp.maximum(t - w // 2, 0)
            hi = jnp.minimum(t - w // 2 + w, seq)
            return win[r0:r0 + POOL_HALO] / (hi - lo).astype(F32)

        mean = jnp.concatenate([edge_mean(0), win[POOL_HALO:rows - POOL_HALO] * (1.0 / w),
                                edge_mean(rows - POOL_HALO)], axis=0)
        pooled = (mean - ext[POOL_HALO:POOL_HALO + rows]).astype(BF16)
        y = jnp.dot(pooled, pw_ref[g].astype(BF16), preferred_element_type=F32) + pb_ref[g]
        mixp_ref[:, cs] = (y * ps_ref[:, cs]).astype(BF16)

    def project_out():
        mix = jnp.concatenate([a_ref[0, g] for g in range(N_KV_HEADS)] + [mixp_ref[...]],
                              axis=1)
        x1 = x_ref[...] + jnp.dot(mix, wout_ref[...], preferred_element_type=F32)
        ms = jnp.mean(x1 * x1, axis=-1, keepdims=True)
        x1_ref[...] = x1
        return (x1 * lax.rsqrt(ms + EPS) * g2_ref[...]).astype(BF16)

    bounds = [(c0, min(c0 + FF_CHUNK, d_ff)) for c0 in range(0, d_ff, FF_CHUNK)]

    @pl.when(step == 0)
    def _():
        stage_windows()
        for g in range(N_POOL_GROUPS):
            pool_group(g)
        h_ref[...] = project_out()

    @pl.when(step > 0)
    def _():
        stage_windows()
        h = h_ref[...]
        acc = x1_ref[...]
        h_next = []
        between = [[functools.partial(pool_group, g)] for g in range(N_POOL_GROUPS)]
        between += [[lambda: h_next.append(project_out())], []]
        assert len(between) == len(bounds)
        for c, (c0, c1) in enumerate(bounds):
            gate = jnp.dot(h, wg_ref[:, c0:c1], preferred_element_type=F32)
            up = jnp.dot(h, wu_ref[:, c0:c1], preferred_element_type=F32)
            if c == len(bounds) - 1:
                h_ref[...] = h_next[0]
            act = (gate * jax.nn.sigmoid(gate) * up).astype(BF16)
            acc = acc + jnp.dot(act, wd_ref[c0:c1, :], preferred_element_type=F32)
            for piece in between[c]:
                piece()
        o_ref[...] = acc


def _const_spec(shape):
    nd = len(shape)
    return pl.BlockSpec(shape, lambda *_: (0,) * nd)


def _layer(x2, seq, norm1_g, w_in, q_norm_g, k_norm_g, pool_w, pool_b, pool_scale,
           w_out, norm2_g, w_gate, w_up, w_down):
    n_tok = x2.shape[0]
    batch = n_tok // seq
    d_ff = w_gate.shape[1]
    assert seq % ROW_TILE == 0 and seq % Q_TILE == 0 and ROW_TILE % POOL_HALO == 0
    assert seq % INPROJ_ROWS == 0
    n_row_tiles = n_tok // ROW_TILE
    n_in_tiles = n_tok // INPROJ_ROWS

    cos, sin = _rope_tables(seq)
    seg = jnp.asarray(_segment_ones(256), BF16)
    q_scale = HEAD_DIM ** -0.5 * math.log2(math.e)

    params = pltpu.CompilerParams(dimension_semantics=("arbitrary",),
                                  vmem_limit_bytes=VMEM_LIMIT_BYTES)

    def grouped_rows(width, rows, tile):
        per_seq = seq // rows
        return pl.BlockSpec((1, N_KV_HEADS, rows, width),
                            lambda i: (tile(i) // per_seq, 0, tile(i) % per_seq, 0))

    def lead_tile(n):
        return lambda i: jnp.minimum(i, n - 1)

    mix_tile = lead_tile(n_row_tiles)
    in_tile = lead_tile(n_in_tiles)

    def lag_tile(i):
        return jnp.maximum(i - 1, 0)

    def grouped_seq(width):
        return pl.BlockSpec((1, N_KV_HEADS, seq, width), lambda b: (b, 0, 0, 0))

    q, krep, vext, u = pl.pallas_call(
        functools.partial(_inproj_kernel, q_scale=q_scale, n_tiles=n_in_tiles, seq=seq),
        grid=(n_in_tiles + 1,),
        in_specs=[
            pl.BlockSpec((INPROJ_ROWS, D_MODEL), lambda i: (in_tile(i), 0)),
            _const_spec((1, D_MODEL)),
            _const_spec((D_MODEL, IN_WIDTH)),
            _const_spec((1, HEAD_DIM)),
            _const_spec((1, HEAD_DIM)),
            _const_spec((seq, LANES)),
            _const_spec((seq, LANES)),
            _const_spec((256, 256)),
        ],
        out_specs=[
            grouped_rows(GROUP_WIDTH, INPROJ_ROWS, lag_tile),
            grouped_rows(LANES, INPROJ_ROWS, lag_tile),
            grouped_rows(LANES, INPROJ_ROWS, lag_tile),
            pl.BlockSpec((INPROJ_ROWS, POOL_WIDTH), lambda i: (in_tile(i), 0)),
        ],
        out_shape=[
            jax.ShapeDtypeStruct((batch, N_KV_HEADS, seq, GROUP_WIDTH), BF16),
            jax.ShapeDtypeStruct((batch, N_KV_HEADS, seq, LANES), BF16),
            jax.ShapeDtypeStruct((batch, N_KV_HEADS, seq, LANES), BF16),
            jax.ShapeDtypeStruct((n_tok, POOL_WIDTH), F32),
        ],
        scratch_shapes=[pltpu.VMEM((INPROJ_ROWS, IN_WIDTH - POOL_WIDTH), F32),
                        pltpu.VMEM((INPROJ_ROWS, IN_WIDTH - POOL_WIDTH), F32),
                        pltpu.VMEM((D_MODEL, IN_WIDTH), BF16)],
        compiler_params=params,
        name="inproj",
    )(x2, norm1_g.astype(F32)[None, :], w_in.astype(F32),
      q_norm_g.astype(F32)[None, :], k_norm_g.astype(F32)[None, :],
      jnp.asarray(cos), jnp.asarray(sin), seg)

    late_weights = [w_out, w_gate, w_up, w_down]
    for w in late_weights:
        assert w.shape[0] % (batch * BF16_SUBLANES) == 0

    def next_lead(rows, width):
        return pl.BlockSpec((1, 1, rows, width),
                            lambda b: (jnp.minimum(b + 1, batch - 1), 0, 0, 0))

    def row_slab(w):
        return pl.BlockSpec((w.shape[0] // batch, w.shape[1]), lambda b: (b, 0))

    attn, w_out_b, w_gate_b, w_up_b, w_down_b = pl.pallas_call(
        functools.partial(_attention_kernel, seq=seq),
        grid=(batch,),
        in_specs=([grouped_seq(GROUP_WIDTH), grouped_seq(LANES), grouped_seq(LANES),
                   next_lead(NEXT_TILES * Q_TILE, GROUP_WIDTH), next_lead(seq, LANES)]
                  + [row_slab(w) for w in late_weights]),
        out_specs=[grouped_seq(GROUP_WIDTH)] + [row_slab(w) for w in late_weights],
        out_shape=([jax.ShapeDtypeStruct((batch, N_KV_HEADS, seq, GROUP_WIDTH), BF16)]
                   + [jax.ShapeDtypeStruct(w.shape, BF16) for w in late_weights]),
        scratch_shapes=([pltpu.VMEM((Q_PER_KV * Q_TILE, seq), F32)] * N_SCORE_BUFS
                        + [pltpu.VMEM((Q_PER_KV * Q_TILE, LANES), F32)] * N_SCORE_BUFS
                        + [pltpu.VMEM((Q_PER_KV * Q_TILE, 1), F32)] * N_SCORE_BUFS
                        + [pltpu.VMEM((Q_PER_KV * Q_TILE, LANES), F32)] * N_SCORE_BUFS),
        compiler_params=params,
        name="attention",
    )(q, krep, vext, q, krep, *[w.astype(F32) for w in late_weights])

    halo_blocks = ROW_TILE // POOL_HALO
    last_halo = n_tok // POOL_HALO - 1
    out = pl.pallas_call(
        functools.partial(_mix_ffn_kernel, seq=seq, d_ff=d_ff, n_tiles=n_row_tiles),
        grid=(n_row_tiles + 1,),
        in_specs=[
            pl.BlockSpec((ROW_TILE, D_MODEL), lambda i: (mix_tile(i), 0)),
            grouped_rows(GROUP_WIDTH, ROW_TILE, mix_tile),
            pl.BlockSpec((ROW_TILE, POOL_WIDTH), lambda i: (mix_tile(i), 0)),
            pl.BlockSpec((POOL_HALO, POOL_WIDTH),
                         lambda i: (jnp.maximum(mix_tile(i) * halo_blocks - 1, 0), 0)),
            pl.BlockSpec((POOL_HALO, POOL_WIDTH),
                         lambda i: (jnp.minimum((mix_tile(i) + 1) * halo_blocks, last_halo), 0)),
            _const_spec((N_POOL_GROUPS, POOL_GROUP_DIM, POOL_GROUP_DIM)),
            _const_spec((N_POOL_GROUPS, 1, POOL_GROUP_DIM)),
            _const_spec((1, POOL_WIDTH)),
            _const_spec((D_MODEL, D_MODEL)),
            _const_spec((1, D_MODEL)),
            _const_spec((D_MODEL, d_ff)),
            _const_spec((D_MODEL, d_ff)),
            _const_spec((d_ff, D_MODEL)),
        ],
        out_specs=pl.BlockSpec((ROW_TILE, D_MODEL), lambda i: (jnp.maximum(i - 1, 0), 0)),
        out_shape=jax.ShapeDtypeStruct((n_tok, D_MODEL), F32),
        scratch_shapes=[pltpu.VMEM((ROW_TILE + 2 * POOL_HALO, POOL_WIDTH), F32),
                        pltpu.VMEM((ROW_TILE, POOL_WIDTH), BF16),
                        pltpu.VMEM((ROW_TILE, D_MODEL), F32), pltpu.VMEM((ROW_TILE, D_MODEL), BF16)],
        compiler_params=params,
        name="mix_ffn",
    )(x2, attn, u, u, u, pool_w.astype(F32), pool_b.astype(F32)[:, None, :],
      pool_scale.astype(F32)[None, :], w_out_b, norm2_g.astype(F32)[None, :],
      w_gate_b, w_up_b, w_down_b)
    return out


def kernel(x, norm1_g, w_in, q_norm_g, k_norm_g, pool_w, pool_b, pool_scale,
           w_out, norm2_g, w_gate, w_up, w_down):
    batch, seq, d_model = x.shape
    assert d_model == D_MODEL and w_in.shape[-1] == IN_WIDTH
    x2 = x.reshape(batch * seq, d_model)
    for l in range(norm1_g.shape[0]):
        x2 = _layer(x2, seq, norm1_g[l], w_in[l], q_norm_g[l], k_norm_g[l], pool_w[l],
                    pool_b[l], pool_scale[l], w_out[l], norm2_g[l], w_gate[l], w_up[l],
                    w_down[l])
    return x2.reshape(batch, seq, d_model)
```

```python
import functools
import math

import jax
import jax.numpy as jnp
import numpy as np
from jax import lax
from jax.experimental import pallas as pl
from jax.experimental.pallas import tpu as pltpu

D_MODEL = 1024
HEAD_DIM = 64
N_Q_HEADS = 8
N_KV_HEADS = 2
Q_PER_KV = N_Q_HEADS // N_KV_HEADS
ATTN_WIDTH = N_Q_HEADS * HEAD_DIM
KV_WIDTH = N_KV_HEADS * HEAD_DIM
GROUP_WIDTH = Q_PER_KV * HEAD_DIM
ROPE_PAIRS = HEAD_DIM // 4
ROPE_THETA = 10000.0
POOL_WINDOWS = (2, 4, 8, 16)
N_POOL_GROUPS = len(POOL_WINDOWS)
POOL_WIDTH = D_MODEL - ATTN_WIDTH
POOL_GROUP_DIM = POOL_WIDTH // N_POOL_GROUPS
IN_WIDTH = ATTN_WIDTH + 2 * KV_WIDTH + POOL_WIDTH
GRID_W = 64
EPS = 1e-6

LANES = 128
BF16_SUBLANES = 16
POOL_HALO = 8
VMEM_LIMIT_BYTES = 56 * 1024 * 1024

ROW_TILE = 512
INPROJ_ROWS = 1024
Q_TILE = 128
ATTN_SLOTS_PER_STEP = 4
N_SCORE_BUFS = 4
NEXT_TILES = 2
FF_CHUNK = 512
INPROJ_CHUNK = 256

F32 = jnp.float32
BF16 = jnp.bfloat16


def _lane_iota(shape):
    return lax.broadcasted_iota(jnp.int32, shape, len(shape) - 1)


def _rope_tables(seq):
    t = np.arange(seq)
    row = (t // GRID_W).astype(np.float64)
    col = (t % GRID_W).astype(np.float64)
    inv_freq = ROPE_THETA ** (-np.arange(ROPE_PAIRS, dtype=np.float64) / ROPE_PAIRS)
    ang_row = row[:, None] * inv_freq[None, :]
    ang_col = col[:, None] * inv_freq[None, :]
    ang = np.concatenate([ang_row, ang_row, ang_col, ang_col], axis=1)
    sign = np.concatenate([-np.ones(ROPE_PAIRS), np.ones(ROPE_PAIRS)] * 2)
    cos = np.tile(np.cos(ang), (1, 2)).astype(np.float32)
    sin = np.tile(np.sin(ang) * sign[None, :], (1, 2)).astype(np.float32)
    return cos, sin


def _segment_ones(width):
    idx = np.arange(width) // HEAD_DIM
    return (idx[:, None] == idx[None, :]).astype(np.float32)


def _head_sumsq(x, seg):
    sq = x * x
    hi = sq.astype(BF16)
    lo = (sq - hi.astype(F32)).astype(BF16)
    return (jnp.dot(hi, seg, preferred_element_type=F32)
            + jnp.dot(lo, seg, preferred_element_type=F32))


def _rope_partner(x):
    lane = _lane_iota(x.shape)
    fwd = pltpu.roll(x, LANES - ROPE_PAIRS, axis=1)
    bwd = pltpu.roll(x, ROPE_PAIRS, axis=1)
    return jnp.where((lane & ROPE_PAIRS) == 0, fwd, bwd)


def _norm_rope(x, sumsq, gain, cos, sin):
    y = x * lax.rsqrt(sumsq * (1.0 / HEAD_DIM) + EPS) * gain
    return y * cos + _rope_partner(y) * sin


def _inproj_kernel(x_ref, g1_ref, win_ref, qg_ref, kg_ref, cos_ref, sin_ref, seg_ref,
                   q_ref, krep_ref, vext_ref, u_ref, proja_ref, projb_ref, wbf_ref,
                   *, q_scale, n_tiles, seq):
    step = pl.program_id(0)
    rows = x_ref.shape[0]
    kv0 = ATTN_WIDTH
    u0 = ATTN_WIDTH + 2 * KV_WIDTH
    pos0 = pl.multiple_of((jnp.maximum(step - 1, 0) % (seq // rows)) * rows, rows)

    def rope_tables():
        return cos_ref[pl.ds(pos0, rows), :], sin_ref[pl.ds(pos0, rows), :]

    def head_pair_gain(g_ref):
        g = g_ref[...]
        return jnp.concatenate([g] * (LANES // HEAD_DIM), axis=1)

    def normed_input():
        x = x_ref[...]
        ms = jnp.mean(x * x, axis=-1, keepdims=True)
        return (x * lax.rsqrt(ms + EPS) * g1_ref[...]).astype(BF16)

    def finish_q(src_ref, g):
        qh = src_ref[:, g * GROUP_WIDTH:(g + 1) * GROUP_WIDTH]
        ss = _head_sumsq(qh, seg_ref[...])
        for j in range(GROUP_WIDTH // LANES):
            sl = slice(j * LANES, (j + 1) * LANES)
            qr = _norm_rope(qh[:, sl], ss[:, sl], head_pair_gain(qg_ref), *rope_tables()) * q_scale
            q_ref[0, g, :, sl] = qr.astype(BF16)

    def finish_kv(src_ref):
        k = src_ref[:, kv0:kv0 + KV_WIDTH]
        kss = _head_sumsq(k, seg_ref[:KV_WIDTH, :KV_WIDTH])
        kr = _norm_rope(k, kss, head_pair_gain(kg_ref), *rope_tables())
        first = _lane_iota(kr.shape) < HEAD_DIM
        kswap = pltpu.roll(kr, HEAD_DIM, axis=1)
        krep_ref[0, 0] = jnp.where(first, kr, kswap).astype(BF16)
        krep_ref[0, 1] = jnp.where(first, kswap, kr).astype(BF16)
        v = src_ref[:, kv0 + KV_WIDTH:kv0 + 2 * KV_WIDTH]
        vswap = pltpu.roll(v, HEAD_DIM, axis=1)
        vext_ref[0, 0] = jnp.where(first, v, 1.0).astype(BF16)
        vext_ref[0, 1] = jnp.where(first, vswap, 1.0).astype(BF16)

    def finish_pieces(src_ref):
        return [functools.partial(finish_q, src_ref, 0), functools.partial(finish_q, src_ref, 1),
                functools.partial(finish_kv, src_ref)]

    def project(dst_ref, between=()):
        assert u0 % INPROJ_CHUNK == 0
        h = normed_input()
        for c in range(IN_WIDTH // INPROJ_CHUNK):
            c0 = c * INPROJ_CHUNK
            res = jnp.dot(h, wbf_ref[:, c0:c0 + INPROJ_CHUNK], preferred_element_type=F32)
            if c0 < u0:
                dst_ref[:, c0:c0 + INPROJ_CHUNK] = res
            else:
                u_ref[:, c0 - u0:c0 - u0 + INPROJ_CHUNK] = res
            if c < len(between):
                between[c]()

    even = step % 2 == 0
    assert n_tiles % 2 == 0

    @pl.when(step == 0)
    def _():
        wbf_ref[...] = win_ref[...].astype(BF16)
        project(proja_ref)

    @pl.when(jnp.logical_and(even, jnp.logical_and(step > 0, step < n_tiles)))
    def _():
        project(proja_ref, finish_pieces(projb_ref))

    @pl.when(jnp.logical_not(even))
    def _():
        project(projb_ref, finish_pieces(proja_ref))

    @pl.when(step == n_tiles)
    def _():
        for piece in finish_pieces(projb_ref):
            piece()


def _attention_kernel(q_ref, krep_ref, vext_ref, qn_ref, kn_ref,
                      wf0_ref, wf1_ref, wf2_ref, wf3_ref,
                      o_ref, wb0_ref, wb1_ref, wb2_ref, wb3_ref, *scratch, seq):
    tiles_per_group = seq // Q_TILE
    n_tiles = N_KV_HEADS * tiles_per_group
    lane_o = _lane_iota((Q_TILE, LANES))

    def tile_pos(t):
        g = t // tiles_per_group
        r0 = pl.multiple_of((t % tiles_per_group) * Q_TILE, Q_TILE)
        return g, r0

    def scores(t, s_ref, mv_ref):
        if isinstance(t, int) and t >= n_tiles:
            assert t - n_tiles < NEXT_TILES
            qt = qn_ref[0, 0, (t - n_tiles) * Q_TILE:(t - n_tiles + 1) * Q_TILE, :]
            keys = kn_ref[0, 0]
        else:
            g, r0 = tile_pos(t)
            qt = q_ref[0, g, pl.ds(r0, Q_TILE), :]
            keys = krep_ref[0, g]
        halves = [qt[:, j * LANES:(j + 1) * LANES] for j in range(GROUP_WIDTH // LANES)]
        zero = jnp.zeros_like(halves[0])
        per_half = LANES // HEAD_DIM
        qm = jnp.concatenate(
            [jnp.where((lane_o // HEAD_DIM) == hh % per_half, halves[hh // per_half], zero)
             for hh in range(Q_PER_KV)], axis=0)
        s = lax.dot_general(qm, keys, (((1,), (1,)), ((), ())),
                            preferred_element_type=F32)
        s_ref[...] = s
        mv_ref[...] = functools.reduce(
            jnp.maximum, [s[:, j * LANES:(j + 1) * LANES] for j in range(seq // LANES)])

    def row_max(mv_ref, m_ref):
        m_ref[...] = jnp.max(mv_ref[...], axis=-1, keepdims=True)

    def output(t, s_ref, m_ref, oe_ref):
        g, _ = tile_pos(t)
        p = jnp.exp2(s_ref[...] - m_ref[...]).astype(BF16)
        oe_ref[...] = jnp.dot(p, vext_ref[0, g], preferred_element_type=F32)

    def finalize(t, oe_ref):
        g, r0 = tile_pos(t)
        low = lane_o < HEAD_DIM
        for j in range(Q_PER_KV // 2):
            even = oe_ref[2 * j * Q_TILE:(2 * j + 1) * Q_TILE, :]
            odd = oe_ref[(2 * j + 1) * Q_TILE:(2 * j + 2) * Q_TILE, :]
            even_sw = pltpu.roll(even, HEAD_DIM, axis=1)
            odd_sw = pltpu.roll(odd, HEAD_DIM, axis=1)
            pair = jnp.where(low, even, odd_sw) / jnp.where(low, even_sw, odd)
            o_ref[0, g, pl.ds(r0, Q_TILE), j * LANES:(j + 1) * LANES] = pair.astype(BF16)

    n_buf = N_SCORE_BUFS
    assert n_tiles % n_buf == 0 and tiles_per_group >= NEXT_TILES and len(scratch) == 4 * n_buf
    s_refs, mv_refs = scratch[:n_buf], scratch[n_buf:2 * n_buf]
    m_refs, oe_refs = scratch[2 * n_buf:3 * n_buf], scratch[3 * n_buf:]

    def slot(t, k, first=False, side_work=None):
        row_max(mv_refs[(k + 1) % n_buf], m_refs[(k + 1) % n_buf])
        if not first:
            finalize(t - 1, oe_refs[(k - 1) % n_buf])
        scores(t + NEXT_TILES, s_refs[(k + NEXT_TILES) % n_buf], mv_refs[(k + NEXT_TILES) % n_buf])
        if side_work is not None:
            side_work()
        output(t, s_refs[k % n_buf], m_refs[k % n_buf], oe_refs[k % n_buf])

    def round_weights(wf_ref, wb_ref):
        wb_ref[...] = wf_ref[...].astype(BF16)

    side = [functools.partial(round_weights, wf, wb)
            for wf, wb in ((wf0_ref, wb0_ref), (wf1_ref, wb1_ref),
                           (wf2_ref, wb2_ref), (wf3_ref, wb3_ref))]

    @pl.when(pl.program_id(0) == 0)
    def _():
        for t in range(NEXT_TILES):
            scores(t, s_refs[t], mv_refs[t])
        row_max(mv_refs[0], m_refs[0])

    n_loop = (n_tiles - NEXT_TILES - 1) // ATTN_SLOTS_PER_STEP * ATTN_SLOTS_PER_STEP
    assert n_loop > 0 and ATTN_SLOTS_PER_STEP % n_buf == 0
    slot(0, 0, first=True, side_work=side.pop())

    def body(jj, carry):
        for i in range(ATTN_SLOTS_PER_STEP):
            slot(1 + jj * ATTN_SLOTS_PER_STEP + i, (1 + i) % n_buf)
        return carry

    lax.fori_loop(0, n_loop // ATTN_SLOTS_PER_STEP, body, 0)

    for t in range(1 + n_loop, n_tiles):
        slot(t, t % n_buf, side_work=side.pop() if side else None)
    assert not side
    finalize(n_tiles - 1, oe_refs[(n_tiles - 1) % n_buf])


def _mix_ffn_kernel(x_ref, a_ref, u_ref, uprev_ref, unext_ref, pw_ref, pb_ref, ps_ref,
                    wout_ref, g2_ref, wg_ref, wu_ref, wd_ref, o_ref,
                    ext_ref, mixp_ref, x1_ref, h_ref, *, seq, d_ff, n_tiles):
    rows = x_ref.shape[0]
    tiles_per_seq = seq // rows
    step = pl.program_id(0)

    st = jnp.minimum(step, n_tiles - 1) % tiles_per_seq

    def stage_windows():
        ext_ref[0:POOL_HALO, :] = jnp.where(st > 0, uprev_ref[...], 0.0)
        ext_ref[POOL_HALO:POOL_HALO + rows, :] = u_ref[...]
        ext_ref[POOL_HALO + rows:, :] = jnp.where(st < tiles_per_seq - 1, unext_ref[...], 0.0)

    def pool_group(g):
        n_ext = rows + 2 * POOL_HALO
        w = POOL_WINDOWS[g]
        assert w // 2 <= POOL_HALO
        cs = slice(g * POOL_GROUP_DIM, (g + 1) * POOL_GROUP_DIM)
        ext = ext_ref[:, cs]
        f = ext
        k = 1
        while k < w // 2:
            f = f + pltpu.roll(f, n_ext - k, axis=0)
            k *= 2
        both = f + pltpu.roll(f, w // 2, axis=0)
        win = both[POOL_HALO:POOL_HALO + rows]

        def edge_mean(r0):
            t = st * rows + r0 + lax.broadcasted_iota(jnp.int32, (POOL_HALO, LANES), 0)
            lo = jnp.maximum(t - w // 2, 0)
            hi = jnp.minimum(t - w // 2 + w, seq)
            return win[r0:r0 + POOL_HALO] / (hi - lo).astype(F32)

        mean = jnp.concatenate([edge_mean(0), win[POOL_HALO:rows - POOL_HALO] * (1.0 / w),
                                edge_mean(rows - POOL_HALO)], axis=0)
        pooled = (mean - ext[POOL_HALO:POOL_HALO + rows]).astype(BF16)
        y = jnp.dot(pooled, pw_ref[g].astype(BF16), preferred_element_type=F32) + pb_ref[g]
        mixp_ref[:, cs] = (y * ps_ref[:, cs]).astype(BF16)

    def project_out():
        mix = jnp.concatenate([a_ref[0, g] for g in range(N_KV_HEADS)] + [mixp_ref[...]],
                              axis=1)
        x1 = x_ref[...] + jnp.dot(mix, wout_ref[...], preferred_element_type=F32)
        ms = jnp.mean(x1 * x1, axis=-1, keepdims=True)
        x1_ref[...] = x1
        return (x1 * lax.rsqrt(ms + EPS) * g2_ref[...]).astype(BF16)

    bounds = [(c0, min(c0 + FF_CHUNK, d_ff)) for c0 in range(0, d_ff, FF_CHUNK)]

    @pl.when(step == 0)
    def _():
        stage_windows()
        for g in range(N_POOL_GROUPS):
            pool_group(g)
        h_ref[...] = project_out()

    @pl.when(step > 0)
    def _():
        stage_windows()
        h = h_ref[...]
        acc = x1_ref[...]
        h_next = []
        between = [[functools.partial(pool_group, g)] for g in range(N_POOL_GROUPS)]
        between += [[lambda: h_next.append(project_out())], []]
        assert len(between) == len(bounds)
        for c, (c0, c1) in enumerate(bounds):
            gate = jnp.dot(h, wg_ref[:, c0:c1], preferred_element_type=F32)
            up = jnp.dot(h, wu_ref[:, c0:c1], preferred_element_type=F32)
            if c == len(bounds) - 1:
                h_ref[...] = h_next[0]
            act = (gate * jax.nn.sigmoid(gate) * up).astype(BF16)
            acc = acc + jnp.dot(act, wd_ref[c0:c1, :], preferred_element_type=F32)
            for piece in between[c]:
                piece()
        o_ref[...] = acc


def _const_spec(shape):
    nd = len(shape)
    return pl.BlockSpec(shape, lambda *_: (0,) * nd)


def _layer(x2, seq, norm1_g, w_in, q_norm_g, k_norm_g, pool_w, pool_b, pool_scale,
           w_out, norm2_g, w_gate, w_up, w_down):
    n_tok = x2.shape[0]
    batch = n_tok // seq
    d_ff = w_gate.shape[1]
    assert seq % ROW_TILE == 0 and seq % Q_TILE == 0 and ROW_TILE % POOL_HALO == 0
    assert seq % INPROJ_ROWS == 0
    n_row_tiles = n_tok // ROW_TILE
    n_in_tiles = n_tok // INPROJ_ROWS

    cos, sin = _rope_tables(seq)
    seg = jnp.asarray(_segment_ones(256), BF16)
    q_scale = HEAD_DIM ** -0.5 * math.log2(math.e)

    params = pltpu.CompilerParams(dimension_semantics=("arbitrary",),
                                  vmem_limit_bytes=VMEM_LIMIT_BYTES)

    def grouped_rows(width, rows, tile):
        per_seq = seq // rows
        return pl.BlockSpec((1, N_KV_HEADS, rows, width),
                            lambda i: (tile(i) // per_seq, 0, tile(i) % per_seq, 0))

    def lead_tile(n):
        return lambda i: jnp.minimum(i, n - 1)

    mix_tile = lead_tile(n_row_tiles)
    in_tile = lead_tile(n_in_tiles)

    def lag_tile(i):
        return jnp.maximum(i - 1, 0)

    def grouped_seq(width):
        return pl.BlockSpec((1, N_KV_HEADS, seq, width), lambda b: (b, 0, 0, 0))

    q, krep, vext, u = pl.pallas_call(
        functools.partial(_inproj_kernel, q_scale=q_scale, n_tiles=n_in_tiles, seq=seq),
        grid=(n_in_tiles + 1,),
        in_specs=[
            pl.BlockSpec((INPROJ_ROWS, D_MODEL), lambda i: (in_tile(i), 0)),
            _const_spec((1, D_MODEL)),
            _const_spec((D_MODEL, IN_WIDTH)),
            _const_spec((1, HEAD_DIM)),
            _const_spec((1, HEAD_DIM)),
            _const_spec((seq, LANES)),
            _const_spec((seq, LANES)),
            _const_spec((256, 256)),
        ],
        out_specs=[
            grouped_rows(GROUP_WIDTH, INPROJ_ROWS, lag_tile),
            grouped_rows(LANES, INPROJ_ROWS, lag_tile),
            grouped_rows(LANES, INPROJ_ROWS, lag_tile),
            pl.BlockSpec((INPROJ_ROWS, POOL_WIDTH), lambda i: (in_tile(i), 0)),
        ],
        out_shape=[
            jax.ShapeDtypeStruct((batch, N_KV_HEADS, seq, GROUP_WIDTH), BF16),
            jax.ShapeDtypeStruct((batch, N_KV_HEADS, seq, LANES), BF16),
            jax.ShapeDtypeStruct((batch, N_KV_HEADS, seq, LANES), BF16),
            jax.ShapeDtypeStruct((n_tok, POOL_WIDTH), F32),
        ],
        scratch_shapes=[pltpu.VMEM((INPROJ_ROWS, IN_WIDTH - POOL_WIDTH), F32),
                        pltpu.VMEM((INPROJ_ROWS, IN_WIDTH - POOL_WIDTH), F32),
                        pltpu.VMEM((D_MODEL, IN_WIDTH), BF16)],
        compiler_params=params,
        name="inproj",
    )(x2, norm1_g.astype(F32)[None, :], w_in.astype(F32),
      q_norm_g.astype(F32)[None, :], k_norm_g.astype(F32)[None, :],
      jnp.asarray(cos), jnp.asarray(sin), seg)

    late_weights = [w_out, w_gate, w_up, w_down]
    for w in late_weights:
        assert w.shape[0] % (batch * BF16_SUBLANES) == 0

    def next_lead(rows, width):
        return pl.BlockSpec((1, 1, rows, width),
                            lambda b: (jnp.minimum(b + 1, batch - 1), 0, 0, 0))

    def row_slab(w):
        return pl.BlockSpec((w.shape[0] // batch, w.shape[1]), lambda b: (b, 0))

    attn, w_out_b, w_gate_b, w_up_b, w_down_b = pl.pallas_call(
        functools.partial(_attention_kernel, seq=seq),
        grid=(batch,),
        in_specs=([grouped_seq(GROUP_WIDTH), grouped_seq(LANES), grouped_seq(LANES),
                   next_lead(NEXT_TILES * Q_TILE, GROUP_WIDTH), next_lead(seq, LANES)]
                  + [row_slab(w) for w in late_weights]),
        out_specs=[grouped_seq(GROUP_WIDTH)] + [row_slab(w) for w in late_weights],
        out_shape=([jax.ShapeDtypeStruct((batch, N_KV_HEADS, seq, GROUP_WIDTH), BF16)]
                   + [jax.ShapeDtypeStruct(w.shape, BF16) for w in late_weights]),
        scratch_shapes=([pltpu.VMEM((Q_PER_KV * Q_TILE, seq), F32)] * N_SCORE_BUFS
                        + [pltpu.VMEM((Q_PER_KV * Q_TILE, LANES), F32)] * N_SCORE_BUFS
                        + [pltpu.VMEM((Q_PER_KV * Q_TILE, 1), F32)] * N_SCORE_BUFS
                        + [pltpu.VMEM((Q_PER_KV * Q_TILE, LANES), F32)] * N_SCORE_BUFS),
        compiler_params=params,
        name="attention",
    )(q, krep, vext, q, krep, *[w.astype(F32) for w in late_weights])

    halo_blocks = ROW_TILE // POOL_HALO
    last_halo = n_tok // POOL_HALO - 1
    out = pl.pallas_call(
        functools.partial(_mix_ffn_kernel, seq=seq, d_ff=d_ff, n_tiles=n_row_tiles),
        grid=(n_row_tiles + 1,),
        in_specs=[
            pl.BlockSpec((ROW_TILE, D_MODEL), lambda i: (mix_tile(i), 0)),
            grouped_rows(GROUP_WIDTH, ROW_TILE, mix_tile),
            pl.BlockSpec((ROW_TILE, POOL_WIDTH), lambda i: (mix_tile(i), 0)),
            pl.BlockSpec((POOL_HALO, POOL_WIDTH),
                         lambda i: (jnp.maximum(mix_tile(i) * halo_blocks - 1, 0), 0)),
            pl.BlockSpec((POOL_HALO, POOL_WIDTH),
                         lambda i: (jnp.minimum((mix_tile(i) + 1) * halo_blocks, last_halo), 0)),
            _const_spec((N_POOL_GROUPS, POOL_GROUP_DIM, POOL_GROUP_DIM)),
            _const_spec((N_POOL_GROUPS, 1, POOL_GROUP_DIM)),
            _const_spec((1, POOL_WIDTH)),
            _const_spec((D_MODEL, D_MODEL)),
            _const_spec((1, D_MODEL)),
            _const_spec((D_MODEL, d_ff)),
            _const_spec((D_MODEL, d_ff)),
            _const_spec((d_ff, D_MODEL)),
        ],
        out_specs=pl.BlockSpec((ROW_TILE, D_MODEL), lambda i: (jnp.maximum(i - 1, 0), 0)),
        out_shape=jax.ShapeDtypeStruct((n_tok, D_MODEL), F32),
        scratch_shapes=[pltpu.VMEM((ROW_TILE + 2 * POOL_HALO, POOL_WIDTH), F32),
                        pltpu.VMEM((ROW_TILE, POOL_WIDTH), BF16),
                        pltpu.VMEM((ROW_TILE, D_MODEL), F32), pltpu.VMEM((ROW_TILE, D_MODEL), BF16)],
        compiler_params=params,
        name="mix_ffn",
    )(x2, attn, u, u, u, pool_w.astype(F32), pool_b.astype(F32)[:, None, :],
      pool_scale.astype(F32)[None, :], w_out_b, norm2_g.astype(F32)[None, :],
      w_gate_b, w_up_b, w_down_b)
    return out


def kernel(x, norm1_g, w_in, q_norm_g, k_norm_g, pool_w, pool_b, pool_scale,
           w_out, norm2_g, w_gate, w_up, w_down):
    batch, seq, d_model = x.shape
    assert d_model == D_MODEL and w_in.shape[-1] == IN_WIDTH
    x2 = x.reshape(batch * seq, d_model)
    for l in range(norm1_g.shape[0]):
        x2 = _layer(x2, seq, norm1_g[l], w_in[l], q_norm_g[l], k_norm_g[l], pool_w[l],
                    pool_b[l], pool_scale[l], w_out[l], norm2_g[l], w_gate[l], w_up[l],
                    w_down[l])
    return x2.reshape(batch, seq, d_model)
```

```python
import functools
import math

import jax
import jax.numpy as jnp
import numpy as np
from jax import lax
from jax.experimental import pallas as pl
from jax.experimental.pallas import tpu as pltpu

D_MODEL = 1024
HEAD_DIM = 64
N_Q_HEADS = 8
N_KV_HEADS = 2
Q_PER_KV = N_Q_HEADS // N_KV_HEADS
ATTN_WIDTH = N_Q_HEADS * HEAD_DIM
KV_WIDTH = N_KV_HEADS * HEAD_DIM
GROUP_WIDTH = Q_PER_KV * HEAD_DIM
ROPE_PAIRS = HEAD_DIM // 4
ROPE_THETA = 10000.0
POOL_WINDOWS = (2, 4, 8, 16)
N_POOL_GROUPS = len(POOL_WINDOWS)
POOL_WIDTH = D_MODEL - ATTN_WIDTH
POOL_GROUP_DIM = POOL_WIDTH // N_POOL_GROUPS
IN_WIDTH = ATTN_WIDTH + 2 * KV_WIDTH + POOL_WIDTH
GRID_W = 64
EPS = 1e-6

LANES = 128
BF16_SUBLANES = 16
POOL_HALO = 8
VMEM_LIMIT_BYTES = 56 * 1024 * 1024

ROW_TILE = 512
MIX_ROWS = 1024
INPROJ_ROWS = 1024
Q_TILE = 128
ATTN_SLOTS_PER_STEP = 4
N_SCORE_BUFS = 4
NEXT_TILES = 2
FF_CHUNK = 512
INPROJ_CHUNK = 256

F32 = jnp.float32
BF16 = jnp.bfloat16


def _lane_iota(shape):
    return lax.broadcasted_iota(jnp.int32, shape, len(shape) - 1)


def _rope_tables(seq):
    t = np.arange(seq)
    row = (t // GRID_W).astype(np.float64)
    col = (t % GRID_W).astype(np.float64)
    inv_freq = ROPE_THETA ** (-np.arange(ROPE_PAIRS, dtype=np.float64) / ROPE_PAIRS)
    ang_row = row[:, None] * inv_freq[None, :]
    ang_col = col[:, None] * inv_freq[None, :]
    ang = np.concatenate([ang_row, ang_row, ang_col, ang_col], axis=1)
    sign = np.concatenate([-np.ones(ROPE_PAIRS), np.ones(ROPE_PAIRS)] * 2)
    cos = np.tile(np.cos(ang), (1, 2)).astype(np.float32)
    sin = np.tile(np.sin(ang) * sign[None, :], (1, 2)).astype(np.float32)
    return cos, sin


def _segment_ones(width):
    idx = np.arange(width) // HEAD_DIM
    return (idx[:, None] == idx[None, :]).astype(np.float32)


def _head_sumsq(x, seg):
    sq = x * x
    hi = sq.astype(BF16)
    lo = (sq - hi.astype(F32)).astype(BF16)
    return (jnp.dot(hi, seg, preferred_element_type=F32)
            + jnp.dot(lo, seg, preferred_element_type=F32))


def _rope_partner(x):
    lane = _lane_iota(x.shape)
    fwd = pltpu.roll(x, LANES - ROPE_PAIRS, axis=1)
    bwd = pltpu.roll(x, ROPE_PAIRS, axis=1)
    return jnp.where((lane & ROPE_PAIRS) == 0, fwd, bwd)


def _norm_rope(x, sumsq, gain, cos, sin):
    y = x * lax.rsqrt(sumsq * (1.0 / HEAD_DIM) + EPS) * gain
    return y * cos + _rope_partner(y) * sin


def _inproj_kernel(x_ref, g1_ref, win_ref, qg_ref, kg_ref, cos_ref, sin_ref, seg_ref,
                   q_ref, krep_ref, vext_ref, u_ref, proja_ref, projb_ref, wbf_ref,
                   *, q_scale, n_tiles, seq):
    step = pl.program_id(0)
    rows = x_ref.shape[0]
    kv0 = ATTN_WIDTH
    u0 = ATTN_WIDTH + 2 * KV_WIDTH
    pos0 = pl.multiple_of((jnp.maximum(step - 1, 0) % (seq // rows)) * rows, rows)

    def rope_tables():
        return cos_ref[pl.ds(pos0, rows), :], sin_ref[pl.ds(pos0, rows), :]

    def head_pair_gain(g_ref):
        g = g_ref[...]
        return jnp.concatenate([g] * (LANES // HEAD_DIM), axis=1)

    def normed_input():
        x = x_ref[...]
        ms = jnp.mean(x * x, axis=-1, keepdims=True)
        return (x * lax.rsqrt(ms + EPS) * g1_ref[...]).astype(BF16)

    def finish_q(src_ref, g):
        qh = src_ref[:, g * GROUP_WIDTH:(g + 1) * GROUP_WIDTH]
        ss = _head_sumsq(qh, seg_ref[...])
        for j in range(GROUP_WIDTH // LANES):
            sl = slice(j * LANES, (j + 1) * LANES)
            qr = _norm_rope(qh[:, sl], ss[:, sl], head_pair_gain(qg_ref), *rope_tables()) * q_scale
            q_ref[0, g, :, sl] = qr.astype(BF16)

    def finish_kv(src_ref):
        k = src_ref[:, kv0:kv0 + KV_WIDTH]
        kss = _head_sumsq(k, seg_ref[:KV_WIDTH, :KV_WIDTH])
        kr = _norm_rope(k, kss, head_pair_gain(kg_ref), *rope_tables())
        first = _lane_iota(kr.shape) < HEAD_DIM
        kswap = pltpu.roll(kr, HEAD_DIM, axis=1)
        krep_ref[0, 0] = jnp.where(first, kr, kswap).astype(BF16)
        krep_ref[0, 1] = jnp.where(first, kswap, kr).astype(BF16)
        v = src_ref[:, kv0 + KV_WIDTH:kv0 + 2 * KV_WIDTH]
        vswap = pltpu.roll(v, HEAD_DIM, axis=1)
        vext_ref[0, 0] = jnp.where(first, v, 1.0).astype(BF16)
        vext_ref[0, 1] = jnp.where(first, vswap, 1.0).astype(BF16)

    def finish_pieces(src_ref):
        return [functools.partial(finish_q, src_ref, 0), functools.partial(finish_q, src_ref, 1),
                functools.partial(finish_kv, src_ref)]

    def project(dst_ref, between=()):
        assert u0 % INPROJ_CHUNK == 0
        h = normed_input()
        for c in range(IN_WIDTH // INPROJ_CHUNK):
            c0 = c * INPROJ_CHUNK
            res = jnp.dot(h, wbf_ref[:, c0:c0 + INPROJ_CHUNK], preferred_element_type=F32)
            if c0 < u0:
                dst_ref[:, c0:c0 + INPROJ_CHUNK] = res
            else:
                u_ref[:, c0 - u0:c0 - u0 + INPROJ_CHUNK] = res
            if c < len(between):
                between[c]()

    even = step % 2 == 0
    assert n_tiles % 2 == 0

    @pl.when(step == 0)
    def _():
        wbf_ref[...] = win_ref[...].astype(BF16)
        project(proja_ref)

    @pl.when(jnp.logical_and(even, jnp.logical_and(step > 0, step < n_tiles)))
    def _():
        project(proja_ref, finish_pieces(projb_ref))

    @pl.when(jnp.logical_not(even))
    def _():
        project(projb_ref, finish_pieces(proja_ref))

    @pl.when(step == n_tiles)
    def _():
        for piece in finish_pieces(projb_ref):
            piece()


def _attention_kernel(q_ref, krep_ref, vext_ref, qn_ref, kn_ref,
                      wf0_ref, wf1_ref, wf2_ref, wf3_ref,
                      o_ref, wb0_ref, wb1_ref, wb2_ref, wb3_ref, *scratch, seq):
    tiles_per_group = seq // Q_TILE
    n_tiles = N_KV_HEADS * tiles_per_group
    lane_o = _lane_iota((Q_TILE, LANES))

    def tile_pos(t):
        g = t // tiles_per_group
        r0 = pl.multiple_of((t % tiles_per_group) * Q_TILE, Q_TILE)
        return g, r0

    def scores(t, s_ref, mv_ref):
        if isinstance(t, int) and t >= n_tiles:
            assert t - n_tiles < NEXT_TILES
            qt = qn_ref[0, 0, (t - n_tiles) * Q_TILE:(t - n_tiles + 1) * Q_TILE, :]
            keys = kn_ref[0, 0]
        else:
            g, r0 = tile_pos(t)
            qt = q_ref[0, g, pl.ds(r0, Q_TILE), :]
            keys = krep_ref[0, g]
        halves = [qt[:, j * LANES:(j + 1) * LANES] for j in range(GROUP_WIDTH // LANES)]
        zero = jnp.zeros_like(halves[0])
        per_half = LANES // HEAD_DIM
        qm = jnp.concatenate(
            [jnp.where((lane_o // HEAD_DIM) == hh % per_half, halves[hh // per_half], zero)
             for hh in range(Q_PER_KV)], axis=0)
        s = lax.dot_general(qm, keys, (((1,), (1,)), ((), ())),
                            preferred_element_type=F32)
        s_ref[...] = s
        mv_ref[...] = functools.reduce(
            jnp.maximum, [s[:, j * LANES:(j + 1) * LANES] for j in range(seq // LANES)])

    def row_max(mv_ref, m_ref):
        m_ref[...] = jnp.max(mv_ref[...], axis=-1, keepdims=True)

    def output(t, s_ref, m_ref, oe_ref):
        g, _ = tile_pos(t)
        p = jnp.exp2(s_ref[...] - m_ref[...]).astype(BF16)
        oe_ref[...] = jnp.dot(p, vext_ref[0, g], preferred_element_type=F32)

    def finalize(t, oe_ref):
        g, r0 = tile_pos(t)
        low = lane_o < HEAD_DIM
        for j in range(Q_PER_KV // 2):
            even = oe_ref[2 * j * Q_TILE:(2 * j + 1) * Q_TILE, :]
            odd = oe_ref[(2 * j + 1) * Q_TILE:(2 * j + 2) * Q_TILE, :]
            even_sw = pltpu.roll(even, HEAD_DIM, axis=1)
            odd_sw = pltpu.roll(odd, HEAD_DIM, axis=1)
            pair = jnp.where(low, even, odd_sw) / jnp.where(low, even_sw, odd)
            o_ref[0, g, pl.ds(r0, Q_TILE), j * LANES:(j + 1) * LANES] = pair.astype(BF16)

    n_buf = N_SCORE_BUFS
    assert n_tiles % n_buf == 0 and tiles_per_group >= NEXT_TILES and len(scratch) == 4 * n_buf
    s_refs, mv_refs = scratch[:n_buf], scratch[n_buf:2 * n_buf]
    m_refs, oe_refs = scratch[2 * n_buf:3 * n_buf], scratch[3 * n_buf:]

    def slot(t, k, first=False, side_work=None):
        row_max(mv_refs[(k + 1) % n_buf], m_refs[(k + 1) % n_buf])
        if not first:
            finalize(t - 1, oe_refs[(k - 1) % n_buf])
        scores(t + NEXT_TILES, s_refs[(k + NEXT_TILES) % n_buf], mv_refs[(k + NEXT_TILES) % n_buf])
        if side_work is not None:
            side_work()
        output(t, s_refs[k % n_buf], m_refs[k % n_buf], oe_refs[k % n_buf])

    def round_weights(wf_ref, wb_ref):
        wb_ref[...] = wf_ref[...].astype(BF16)

    side = [functools.partial(round_weights, wf, wb)
            for wf, wb in ((wf0_ref, wb0_ref), (wf1_ref, wb1_ref),
                           (wf2_ref, wb2_ref), (wf3_ref, wb3_ref))]

    @pl.when(pl.program_id(0) == 0)
    def _():
        for t in range(NEXT_TILES):
            scores(t, s_refs[t], mv_refs[t])
        row_max(mv_refs[0], m_refs[0])

    n_loop = (n_tiles - NEXT_TILES - 1) // ATTN_SLOTS_PER_STEP * ATTN_SLOTS_PER_STEP
    assert n_loop > 0 and ATTN_SLOTS_PER_STEP % n_buf == 0
    slot(0, 0, first=True, side_work=side.pop())

    def body(jj, carry):
        for i in range(ATTN_SLOTS_PER_STEP):
            slot(1 + jj * ATTN_SLOTS_PER_STEP + i, (1 + i) % n_buf)
        return carry

    lax.fori_loop(0, n_loop // ATTN_SLOTS_PER_STEP, body, 0)

    for t in range(1 + n_loop, n_tiles):
        slot(t, t % n_buf, side_work=side.pop() if side else None)
    assert not side
    finalize(n_tiles - 1, oe_refs[(n_tiles - 1) % n_buf])


def _mix_ffn_kernel(x_ref, a_ref, u_ref, uprev_ref, unext_ref, pw_ref, pb_ref, ps_ref,
                    wout_ref, g2_ref, wg_ref, wu_ref, wd_ref, o_ref,
                    ext_ref, mixp_ref, x1_ref, h_ref, *, seq, d_ff, n_blocks):
    blk_rows = x_ref.shape[0]
    rows = ROW_TILE
    n_sub = blk_rows // rows
    blocks_per_seq = seq // blk_rows
    step = pl.program_id(0)

    blk = jnp.minimum(step, n_blocks - 1) % blocks_per_seq

    def sub_rows(j):
        return pl.ds(pl.multiple_of(j * rows, rows), rows)

    def stage_windows(j):
        r0 = j * rows
        inner_prev = u_ref[pl.ds(pl.multiple_of(jnp.maximum(r0 - POOL_HALO, 0), POOL_HALO),
                                 POOL_HALO), :]
        inner_next = u_ref[pl.ds(pl.multiple_of(jnp.minimum(r0 + rows, blk_rows - POOL_HALO),
                                                POOL_HALO), POOL_HALO), :]
        outer_prev = jnp.where(blk > 0, uprev_ref[...], 0.0)
        outer_next = jnp.where(blk < blocks_per_seq - 1, unext_ref[...], 0.0)
        ext_ref[0:POOL_HALO, :] = jnp.where(j == 0, outer_prev, inner_prev)
        ext_ref[POOL_HALO:POOL_HALO + rows, :] = u_ref[sub_rows(j), :]
        ext_ref[POOL_HALO + rows:, :] = jnp.where(j == n_sub - 1, outer_next, inner_next)

    def pool_group(j, g):
        n_ext = rows + 2 * POOL_HALO
        w = POOL_WINDOWS[g]
        assert w // 2 <= POOL_HALO
        cs = slice(g * POOL_GROUP_DIM, (g + 1) * POOL_GROUP_DIM)
        ext = ext_ref[:, cs]
        f = ext
        k = 1
        while k < w // 2:
            f = f + pltpu.roll(f, n_ext - k, axis=0)
            k *= 2
        both = f + pltpu.roll(f, w // 2, axis=0)
        win = both[POOL_HALO:POOL_HALO + rows]

        def edge_mean(r0):
            t = (blk * blk_rows + j * rows + r0
                 + lax.broadcasted_iota(jnp.int32, (POOL_HALO, LANES), 0))
            lo = jnp.maximum(t - w // 2, 0)
            hi = jnp.minimum(t - w // 2 + w, seq)
            return win[r0:r0 + POOL_HALO] / (hi - lo).astype(F32)

        mean = jnp.concatenate([edge_mean(0), win[POOL_HALO:rows - POOL_HALO] * (1.0 / w),
                                edge_mean(rows - POOL_HALO)], axis=0)
        pooled = (mean - ext[POOL_HALO:POOL_HALO + rows]).astype(BF16)
        y = jnp.dot(pooled, pw_ref[g].astype(BF16), preferred_element_type=F32) + pb_ref[g]
        mixp_ref[:, cs] = (y * ps_ref[:, cs]).astype(BF16)

    def project_out(j):
        rs = sub_rows(j)
        mix = jnp.concatenate([a_ref[0, g, rs, :] for g in range(N_KV_HEADS)] + [mixp_ref[...]],
                              axis=1)
        x1 = x_ref[rs, :] + jnp.dot(mix, wout_ref[...], preferred_element_type=F32)
        ms = jnp.mean(x1 * x1, axis=-1, keepdims=True)
        x1_ref[j] = x1
        return (x1 * lax.rsqrt(ms + EPS) * g2_ref[...]).astype(BF16)

    bounds = [(c0, min(c0 + FF_CHUNK, d_ff)) for c0 in range(0, d_ff, FF_CHUNK)]

    def mixer_only(j, carry):
        stage_windows(j)
        for g in range(N_POOL_GROUPS):
            pool_group(j, g)
        h_ref[j] = project_out(j)
        return carry

    def ffn_and_next_mixer(j, carry):
        stage_windows(j)
        h = h_ref[j]
        acc = x1_ref[j]
        h_next = []
        between = [[functools.partial(pool_group, j, g)] for g in range(N_POOL_GROUPS)]
        between += [[lambda: h_next.append(project_out(j))], []]
        assert len(between) == len(bounds)
        for c, (c0, c1) in enumerate(bounds):
            gate = jnp.dot(h, wg_ref[:, c0:c1], preferred_element_type=F32)
            up = jnp.dot(h, wu_ref[:, c0:c1], preferred_element_type=F32)
            if c == len(bounds) - 1:
                h_ref[j] = h_next[0]
            act = (gate * jax.nn.sigmoid(gate) * up).astype(BF16)
            acc = acc + jnp.dot(act, wd_ref[c0:c1, :], preferred_element_type=F32)
            for piece in between[c]:
                piece()
        o_ref[sub_rows(j), :] = acc
        return carry

    @pl.when(step == 0)
    def _():
        lax.fori_loop(0, n_sub, mixer_only, 0)

    @pl.when(step > 0)
    def _():
        lax.fori_loop(0, n_sub, ffn_and_next_mixer, 0)


def _const_spec(shape):
    nd = len(shape)
    return pl.BlockSpec(shape, lambda *_: (0,) * nd)


def _layer(x2, seq, norm1_g, w_in, q_norm_g, k_norm_g, pool_w, pool_b, pool_scale,
           w_out, norm2_g, w_gate, w_up, w_down):
    n_tok = x2.shape[0]
    batch = n_tok // seq
    d_ff = w_gate.shape[1]
    assert seq % ROW_TILE == 0 and seq % Q_TILE == 0 and ROW_TILE % POOL_HALO == 0
    assert seq % INPROJ_ROWS == 0
    assert MIX_ROWS % ROW_TILE == 0 and seq % MIX_ROWS == 0
    n_mix_blocks = n_tok // MIX_ROWS
    n_in_tiles = n_tok // INPROJ_ROWS

    cos, sin = _rope_tables(seq)
    seg = jnp.asarray(_segment_ones(256), BF16)
    q_scale = HEAD_DIM ** -0.5 * math.log2(math.e)

    params = pltpu.CompilerParams(dimension_semantics=("arbitrary",),
                                  vmem_limit_bytes=VMEM_LIMIT_BYTES)

    def grouped_rows(width, rows, tile):
        per_seq = seq // rows
        return pl.BlockSpec((1, N_KV_HEADS, rows, width),
                            lambda i: (tile(i) // per_seq, 0, tile(i) % per_seq, 0))

    def lead_tile(n):
        return lambda i: jnp.minimum(i, n - 1)

    mix_tile = lead_tile(n_mix_blocks)
    in_tile = lead_tile(n_in_tiles)

    def lag_tile(i):
        return jnp.maximum(i - 1, 0)

    def grouped_seq(width):
        return pl.BlockSpec((1, N_KV_HEADS, seq, width), lambda b: (b, 0, 0, 0))

    q, krep, vext, u = pl.pallas_call(
        functools.partial(_inproj_kernel, q_scale=q_scale, n_tiles=n_in_tiles, seq=seq),
        grid=(n_in_tiles + 1,),
        in_specs=[
            pl.BlockSpec((INPROJ_ROWS, D_MODEL), lambda i: (in_tile(i), 0)),
            _const_spec((1, D_MODEL)),
            _const_spec((D_MODEL, IN_WIDTH)),
            _const_spec((1, HEAD_DIM)),
            _const_spec((1, HEAD_DIM)),
            _const_spec((seq, LANES)),
            _const_spec((seq, LANES)),
            _const_spec((256, 256)),
        ],
        out_specs=[
            grouped_rows(GROUP_WIDTH, INPROJ_ROWS, lag_tile),
            grouped_rows(LANES, INPROJ_ROWS, lag_tile),
            grouped_rows(LANES, INPROJ_ROWS, lag_tile),
            pl.BlockSpec((INPROJ_ROWS, POOL_WIDTH), lambda i: (in_tile(i), 0)),
        ],
        out_shape=[
            jax.ShapeDtypeStruct((batch, N_KV_HEADS, seq, GROUP_WIDTH), BF16),
            jax.ShapeDtypeStruct((batch, N_KV_HEADS, seq, LANES), BF16),
            jax.ShapeDtypeStruct((batch, N_KV_HEADS, seq, LANES), BF16),
            jax.ShapeDtypeStruct((n_tok, POOL_WIDTH), F32),
        ],
        scratch_shapes=[pltpu.VMEM((INPROJ_ROWS, IN_WIDTH - POOL_WIDTH), F32),
                        pltpu.VMEM((INPROJ_ROWS, IN_WIDTH - POOL_WIDTH), F32),
                        pltpu.VMEM((D_MODEL, IN_WIDTH), BF16)],
        compiler_params=params,
        name="inproj",
    )(x2, norm1_g.astype(F32)[None, :], w_in.astype(F32),
      q_norm_g.astype(F32)[None, :], k_norm_g.astype(F32)[None, :],
      jnp.asarray(cos), jnp.asarray(sin), seg)

    late_weights = [w_out, w_gate, w_up, w_down]
    for w in late_weights:
        assert w.shape[0] % (batch * BF16_SUBLANES) == 0

    def next_lead(rows, width):
        return pl.BlockSpec((1, 1, rows, width),
                            lambda b: (jnp.minimum(b + 1, batch - 1), 0, 0, 0))

    def row_slab(w):
        return pl.BlockSpec((w.shape[0] // batch, w.shape[1]), lambda b: (b, 0))

    attn, w_out_b, w_gate_b, w_up_b, w_down_b = pl.pallas_call(
        functools.partial(_attention_kernel, seq=seq),
        grid=(batch,),
        in_specs=([grouped_seq(GROUP_WIDTH), grouped_seq(LANES), grouped_seq(LANES),
                   next_lead(NEXT_TILES * Q_TILE, GROUP_WIDTH), next_lead(seq, LANES)]
                  + [row_slab(w) for w in late_weights]),
        out_specs=[grouped_seq(GROUP_WIDTH)] + [row_slab(w) for w in late_weights],
        out_shape=([jax.ShapeDtypeStruct((batch, N_KV_HEADS, seq, GROUP_WIDTH), BF16)]
                   + [jax.ShapeDtypeStruct(w.shape, BF16) for w in late_weights]),
        scratch_shapes=([pltpu.VMEM((Q_PER_KV * Q_TILE, seq), F32)] * N_SCORE_BUFS
                        + [pltpu.VMEM((Q_PER_KV * Q_TILE, LANES), F32)] * N_SCORE_BUFS
                        + [pltpu.VMEM((Q_PER_KV * Q_TILE, 1), F32)] * N_SCORE_BUFS
                        + [pltpu.VMEM((Q_PER_KV * Q_TILE, LANES), F32)] * N_SCORE_BUFS),
        compiler_params=params,
        name="attention",
    )(q, krep, vext, q, krep, *[w.astype(F32) for w in late_weights])

    halo_blocks = MIX_ROWS // POOL_HALO
    last_halo = n_tok // POOL_HALO - 1
    out = pl.pallas_call(
        functools.partial(_mix_ffn_kernel, seq=seq, d_ff=d_ff, n_blocks=n_mix_blocks),
        grid=(n_mix_blocks + 1,),
        in_specs=[
            pl.BlockSpec((MIX_ROWS, D_MODEL), lambda i: (mix_tile(i), 0)),
            grouped_rows(GROUP_WIDTH, MIX_ROWS, mix_tile),
            pl.BlockSpec((MIX_ROWS, POOL_WIDTH), lambda i: (mix_tile(i), 0)),
            pl.BlockSpec((POOL_HALO, POOL_WIDTH),
                         lambda i: (jnp.maximum(mix_tile(i) * halo_blocks - 1, 0), 0)),
            pl.BlockSpec((POOL_HALO, POOL_WIDTH),
                         lambda i: (jnp.minimum((mix_tile(i) + 1) * halo_blocks, last_halo), 0)),
            _const_spec((N_POOL_GROUPS, POOL_GROUP_DIM, POOL_GROUP_DIM)),
            _const_spec((N_POOL_GROUPS, 1, POOL_GROUP_DIM)),
            _const_spec((1, POOL_WIDTH)),
            _const_spec((D_MODEL, D_MODEL)),
            _const_spec((1, D_MODEL)),
            _const_spec((D_MODEL, d_ff)),
            _const_spec((D_MODEL, d_ff)),
            _const_spec((d_ff, D_MODEL)),
        ],
        out_specs=pl.BlockSpec((MIX_ROWS, D_MODEL), lambda i: (lag_tile(i), 0)),
        out_shape=jax.ShapeDtypeStruct((n_tok, D_MODEL), F32),
        scratch_shapes=[pltpu.VMEM((ROW_TILE + 2 * POOL_HALO, POOL_WIDTH), F32),
                        pltpu.VMEM((ROW_TILE, POOL_WIDTH), BF16),
                        pltpu.VMEM((MIX_ROWS // ROW_TILE, ROW_TILE, D_MODEL), F32),
                        pltpu.VMEM((MIX_ROWS // ROW_TILE, ROW_TILE, D_MODEL), BF16)],
        compiler_params=params,
        name="mix_ffn",
    )(x2, attn, u, u, u, pool_w.astype(F32), pool_b.astype(F32)[:, None, :],
      pool_scale.astype(F32)[None, :], w_out_b, norm2_g.astype(F32)[None, :],
      w_gate_b, w_up_b, w_down_b)
    return out


def kernel(x, norm1_g, w_in, q_norm_g, k_norm_g, pool_w, pool_b, pool_scale,
           w_out, norm2_g, w_gate, w_up, w_down):
    batch, seq, d_model = x.shape
    assert d_model == D_MODEL and w_in.shape[-1] == IN_WIDTH
    x2 = x.reshape(batch * seq, d_model)
    for l in range(norm1_g.shape[0]):
        x2 = _layer(x2, seq, norm1_g[l], w_in[l], q_norm_g[l], k_norm_g[l], pool_w[l],
                    pool_b[l], pool_scale[l], w_out[l], norm2_g[l], w_gate[l], w_up[l],
                    w_down[l])
    return x2.reshape(batch, seq, d_model)
```

```python
import functools
import math

import jax
import jax.numpy as jnp
import numpy as np
from jax import lax
from jax.experimental import pallas as pl
from jax.experimental.pallas import tpu as pltpu

D_MODEL = 1024
HEAD_DIM = 64
N_Q_HEADS = 8
N_KV_HEADS = 2
Q_PER_KV = N_Q_HEADS // N_KV_HEADS
ATTN_WIDTH = N_Q_HEADS * HEAD_DIM
KV_WIDTH = N_KV_HEADS * HEAD_DIM
GROUP_WIDTH = Q_PER_KV * HEAD_DIM
ROPE_PAIRS = HEAD_DIM // 4
ROPE_THETA = 10000.0
POOL_WINDOWS = (2, 4, 8, 16)
N_POOL_GROUPS = len(POOL_WINDOWS)
POOL_WIDTH = D_MODEL - ATTN_WIDTH
POOL_GROUP_DIM = POOL_WIDTH // N_POOL_GROUPS
IN_WIDTH = ATTN_WIDTH + 2 * KV_WIDTH + POOL_WIDTH
GRID_W = 64
EPS = 1e-6

LANES = 128
BF16_SUBLANES = 16
POOL_HALO = 8
VMEM_LIMIT_BYTES = 56 * 1024 * 1024

ROW_TILE = 512
INPROJ_ROWS = 1024
Q_TILE = 128
ATTN_SLOTS_PER_STEP = 4
N_SCORE_BUFS = 4
NEXT_TILES = 2
FF_CHUNK = 512
INPROJ_CHUNK = 256

F32 = jnp.float32
BF16 = jnp.bfloat16


def _lane_iota(shape):
    return lax.broadcasted_iota(jnp.int32, shape, len(shape) - 1)


def _rope_tables(seq):
    t = np.arange(seq)
    row = (t // GRID_W).astype(np.float64)
    col = (t % GRID_W).astype(np.float64)
    inv_freq = ROPE_THETA ** (-np.arange(ROPE_PAIRS, dtype=np.float64) / ROPE_PAIRS)
    ang_row = row[:, None] * inv_freq[None, :]
    ang_col = col[:, None] * inv_freq[None, :]
    ang = np.concatenate([ang_row, ang_row, ang_col, ang_col], axis=1)
    sign = np.concatenate([-np.ones(ROPE_PAIRS), np.ones(ROPE_PAIRS)] * 2)
    cos = np.tile(np.cos(ang), (1, 2)).astype(np.float32)
    sin = np.tile(np.sin(ang) * sign[None, :], (1, 2)).astype(np.float32)
    return cos, sin


def _segment_ones(width):
    idx = np.arange(width) // HEAD_DIM
    return (idx[:, None] == idx[None, :]).astype(np.float32)


def _head_sumsq(x, seg):
    sq = x * x
    hi = sq.astype(BF16)
    lo = (sq - hi.astype(F32)).astype(BF16)
    return (jnp.dot(hi, seg, preferred_element_type=F32)
            + jnp.dot(lo, seg, preferred_element_type=F32))


def _rope_partner(x):
    lane = _lane_iota(x.shape)
    fwd = pltpu.roll(x, LANES - ROPE_PAIRS, axis=1)
    bwd = pltpu.roll(x, ROPE_PAIRS, axis=1)
    return jnp.where((lane & ROPE_PAIRS) == 0, fwd, bwd)


def _norm_rope(x, sumsq, gain, cos, sin):
    y = x * lax.rsqrt(sumsq * (1.0 / HEAD_DIM) + EPS) * gain
    return y * cos + _rope_partner(y) * sin


def _inproj_kernel(x_ref, g1_ref, win_ref, qg_ref, kg_ref, cos_ref, sin_ref, seg_ref,
                   q_ref, krep_ref, vext_ref, u_ref, proja_ref, projb_ref, wbf_ref,
                   *, q_scale, n_tiles, seq):
    step = pl.program_id(0)
    rows = x_ref.shape[0]
    kv0 = ATTN_WIDTH
    u0 = ATTN_WIDTH + 2 * KV_WIDTH
    pos0 = pl.multiple_of((jnp.maximum(step - 1, 0) % (seq // rows)) * rows, rows)

    def rope_tables():
        return cos_ref[pl.ds(pos0, rows), :], sin_ref[pl.ds(pos0, rows), :]

    def head_pair_gain(g_ref):
        g = g_ref[...]
        return jnp.concatenate([g] * (LANES // HEAD_DIM), axis=1)

    def normed_input():
        x = x_ref[...]
        ms = jnp.mean(x * x, axis=-1, keepdims=True)
        return (x * lax.rsqrt(ms + EPS) * g1_ref[...]).astype(BF16)

    def finish_q(src_ref, g):
        qh = src_ref[:, g * GROUP_WIDTH:(g + 1) * GROUP_WIDTH]
        ss = _head_sumsq(qh, seg_ref[...])
        for j in range(GROUP_WIDTH // LANES):
            sl = slice(j * LANES, (j + 1) * LANES)
            qr = _norm_rope(qh[:, sl], ss[:, sl], head_pair_gain(qg_ref), *rope_tables()) * q_scale
            q_ref[0, g, :, sl] = qr.astype(BF16)

    def finish_kv(src_ref):
        k = src_ref[:, kv0:kv0 + KV_WIDTH]
        kss = _head_sumsq(k, seg_ref[:KV_WIDTH, :KV_WIDTH])
        kr = _norm_rope(k, kss, head_pair_gain(kg_ref), *rope_tables())
        first = _lane_iota(kr.shape) < HEAD_DIM
        kswap = pltpu.roll(kr, HEAD_DIM, axis=1)
        krep_ref[0, 0] = jnp.where(first, kr, kswap).astype(BF16)
        krep_ref[0, 1] = jnp.where(first, kswap, kr).astype(BF16)
        v = src_ref[:, kv0 + KV_WIDTH:kv0 + 2 * KV_WIDTH]
        vswap = pltpu.roll(v, HEAD_DIM, axis=1)
        vext_ref[0, 0] = jnp.where(first, v, 1.0).astype(BF16)
        vext_ref[0, 1] = jnp.where(first, vswap, 1.0).astype(BF16)

    def finish_pieces(src_ref):
        return [functools.partial(finish_q, src_ref, 0), functools.partial(finish_q, src_ref, 1),
                functools.partial(finish_kv, src_ref)]

    def project(dst_ref, between=()):
        assert u0 % INPROJ_CHUNK == 0
        h = normed_input()
        for c in range(IN_WIDTH // INPROJ_CHUNK):
            c0 = c * INPROJ_CHUNK
            res = jnp.dot(h, wbf_ref[:, c0:c0 + INPROJ_CHUNK], preferred_element_type=F32)
            if c0 < u0:
                dst_ref[:, c0:c0 + INPROJ_CHUNK] = res
            else:
                u_ref[:, c0 - u0:c0 - u0 + INPROJ_CHUNK] = res
            if c < len(between):
                between[c]()

    even = step % 2 == 0
    assert n_tiles % 2 == 0

    @pl.when(step == 0)
    def _():
        wbf_ref[...] = win_ref[...].astype(BF16)
        project(proja_ref)

    @pl.when(jnp.logical_and(even, jnp.logical_and(step > 0, step < n_tiles)))
    def _():
        project(proja_ref, finish_pieces(projb_ref))

    @pl.when(jnp.logical_not(even))
    def _():
        project(projb_ref, finish_pieces(proja_ref))

    @pl.when(step == n_tiles)
    def _():
        for piece in finish_pieces(projb_ref):
            piece()


def _attention_kernel(q_ref, krep_ref, vext_ref, qn_ref, kn_ref,
                      wf0_ref, wf1_ref, wf2_ref, wf3_ref,
                      o_ref, wb0_ref, wb1_ref, wb2_ref, wb3_ref, *scratch, seq):
    tiles_per_group = seq // Q_TILE
    n_tiles = N_KV_HEADS * tiles_per_group
    lane_o = _lane_iota((Q_TILE, LANES))

    def tile_pos(t):
        g = t // tiles_per_group
        r0 = pl.multiple_of((t % tiles_per_group) * Q_TILE, Q_TILE)
        return g, r0

    def scores(t, s_ref, mv_ref):
        if isinstance(t, int) and t >= n_tiles:
            assert t - n_tiles < NEXT_TILES
            qt = qn_ref[0, 0, (t - n_tiles) * Q_TILE:(t - n_tiles + 1) * Q_TILE, :]
            keys = kn_ref[0, 0]
        else:
            g, r0 = tile_pos(t)
            qt = q_ref[0, g, pl.ds(r0, Q_TILE), :]
            keys = krep_ref[0, g]
        halves = [qt[:, j * LANES:(j + 1) * LANES] for j in range(GROUP_WIDTH // LANES)]
        zero = jnp.zeros_like(halves[0])
        per_half = LANES // HEAD_DIM
        qm = jnp.concatenate(
            [jnp.where((lane_o // HEAD_DIM) == hh % per_half, halves[hh // per_half], zero)
             for hh in range(Q_PER_KV)], axis=0)
        s = lax.dot_general(qm, keys, (((1,), (1,)), ((), ())),
                            preferred_element_type=F32)
        s_ref[...] = s
        mv_ref[...] = functools.reduce(
            jnp.maximum, [s[:, j * LANES:(j + 1) * LANES] for j in range(seq // LANES)])

    def row_max(mv_ref, m_ref):
        m_ref[...] = jnp.broadcast_to(jnp.max(mv_ref[...], axis=-1, keepdims=True), m_ref.shape)

    def output(t, s_ref, m_ref, oe_ref):
        g, _ = tile_pos(t)
        m = jnp.concatenate([m_ref[...]] * (seq // LANES), axis=1)
        p = jnp.exp2(s_ref[...] - m).astype(BF16)
        oe_ref[...] = jnp.dot(p, vext_ref[0, g], preferred_element_type=F32)

    def finalize(t, oe_ref):
        g, r0 = tile_pos(t)
        low = lane_o < HEAD_DIM
        for j in range(Q_PER_KV // 2):
            even = oe_ref[2 * j * Q_TILE:(2 * j + 1) * Q_TILE, :]
            odd = oe_ref[(2 * j + 1) * Q_TILE:(2 * j + 2) * Q_TILE, :]
            even_sw = pltpu.roll(even, HEAD_DIM, axis=1)
            odd_sw = pltpu.roll(odd, HEAD_DIM, axis=1)
            pair = jnp.where(low, even, odd_sw) / jnp.where(low, even_sw, odd)
            o_ref[0, g, pl.ds(r0, Q_TILE), j * LANES:(j + 1) * LANES] = pair.astype(BF16)

    n_buf = N_SCORE_BUFS
    assert n_tiles % n_buf == 0 and tiles_per_group >= NEXT_TILES and len(scratch) == 4 * n_buf
    s_refs, mv_refs = scratch[:n_buf], scratch[n_buf:2 * n_buf]
    m_refs, oe_refs = scratch[2 * n_buf:3 * n_buf], scratch[3 * n_buf:]

    def slot(t, k, first=False, side_work=None):
        row_max(mv_refs[(k + 1) % n_buf], m_refs[(k + 1) % n_buf])
        if not first:
            finalize(t - 1, oe_refs[(k - 1) % n_buf])
        scores(t + NEXT_TILES, s_refs[(k + NEXT_TILES) % n_buf], mv_refs[(k + NEXT_TILES) % n_buf])
        if side_work is not None:
            side_work()
        output(t, s_refs[k % n_buf], m_refs[k % n_buf], oe_refs[k % n_buf])

    def round_weights(wf_ref, wb_ref):
        wb_ref[...] = wf_ref[...].astype(BF16)

    side = [functools.partial(round_weights, wf, wb)
            for wf, wb in ((wf0_ref, wb0_ref), (wf1_ref, wb1_ref),
                           (wf2_ref, wb2_ref), (wf3_ref, wb3_ref))]

    @pl.when(pl.program_id(0) == 0)
    def _():
        for t in range(NEXT_TILES):
            scores(t, s_refs[t], mv_refs[t])
        row_max(mv_refs[0], m_refs[0])

    n_loop = (n_tiles - NEXT_TILES - 1) // ATTN_SLOTS_PER_STEP * ATTN_SLOTS_PER_STEP
    assert n_loop > 0 and ATTN_SLOTS_PER_STEP % n_buf == 0
    slot(0, 0, first=True, side_work=side.pop())

    def body(jj, carry):
        for i in range(ATTN_SLOTS_PER_STEP):
            slot(1 + jj * ATTN_SLOTS_PER_STEP + i, (1 + i) % n_buf)
        return carry

    lax.fori_loop(0, n_loop // ATTN_SLOTS_PER_STEP, body, 0)

    for t in range(1 + n_loop, n_tiles):
        slot(t, t % n_buf, side_work=side.pop() if side else None)
    assert not side
    finalize(n_tiles - 1, oe_refs[(n_tiles - 1) % n_buf])


def _mix_ffn_kernel(x_ref, a_ref, u_ref, uprev_ref, unext_ref, pw_ref, pb_ref, ps_ref,
                    wout_ref, g2_ref, wg_ref, wu_ref, wd_ref, o_ref,
                    ext_ref, mixp_ref, x1_ref, h_ref, *, seq, d_ff, n_tiles):
    rows = x_ref.shape[0]
    tiles_per_seq = seq // rows
    step = pl.program_id(0)

    st = jnp.minimum(step, n_tiles - 1) % tiles_per_seq

    def stage_windows():
        ext_ref[0:POOL_HALO, :] = jnp.where(st > 0, uprev_ref[...], 0.0)
        ext_ref[POOL_HALO:POOL_HALO + rows, :] = u_ref[...]
        ext_ref[POOL_HALO + rows:, :] = jnp.where(st < tiles_per_seq - 1, unext_ref[...], 0.0)

    def pool_group(g):
        n_ext = rows + 2 * POOL_HALO
        w = POOL_WINDOWS[g]
        assert w // 2 <= POOL_HALO
        cs = slice(g * POOL_GROUP_DIM, (g + 1) * POOL_GROUP_DIM)
        ext = ext_ref[:, cs]
        f = ext
        k = 1
        while k < w // 2:
            f = f + pltpu.roll(f, n_ext - k, axis=0)
            k *= 2
        both = f + pltpu.roll(f, w // 2, axis=0)
        win = both[POOL_HALO:POOL_HALO + rows]

        def edge_mean(r0):
            t = st * rows + r0 + lax.broadcasted_iota(jnp.int32, (POOL_HALO, LANES), 0)
            lo = jnp.maximum(t - w // 2, 0)
            hi = jnp.minimum(t - w // 2 + w, seq)
            return win[r0:r0 + POOL_HALO] / (hi - lo).astype(F32)

        mean = jnp.concatenate([edge_mean(0), win[POOL_HALO:rows - POOL_HALO] * (1.0 / w),
                                edge_mean(rows - POOL_HALO)], axis=0)
        pooled = (mean - ext[POOL_HALO:POOL_HALO + rows]).astype(BF16)
        y = jnp.dot(pooled, pw_ref[g].astype(BF16), preferred_element_type=F32) + pb_ref[g]
        mixp_ref[:, cs] = (y * ps_ref[:, cs]).astype(BF16)

    def project_out():
        mix = jnp.concatenate([a_ref[0, g] for g in range(N_KV_HEADS)] + [mixp_ref[...]],
                              axis=1)
        x1 = x_ref[...] + jnp.dot(mix, wout_ref[...], preferred_element_type=F32)
        ms = jnp.mean(x1 * x1, axis=-1, keepdims=True)
        x1_ref[...] = x1
        return (x1 * lax.rsqrt(ms + EPS) * g2_ref[...]).astype(BF16)

    bounds = [(c0, min(c0 + FF_CHUNK, d_ff)) for c0 in range(0, d_ff, FF_CHUNK)]

    @pl.when(step == 0)
    def _():
        stage_windows()
        for g in range(N_POOL_GROUPS):
            pool_group(g)
        h_ref[...] = project_out()

    @pl.when(step > 0)
    def _():
        stage_windows()
        h = h_ref[...]
        acc = x1_ref[...]
        h_next = []
        between = [[functools.partial(pool_group, g)] for g in range(N_POOL_GROUPS)]
        between += [[lambda: h_next.append(project_out())], []]
        assert len(between) == len(bounds)
        for c, (c0, c1) in enumerate(bounds):
            gate = jnp.dot(h, wg_ref[:, c0:c1], preferred_element_type=F32)
            up = jnp.dot(h, wu_ref[:, c0:c1], preferred_element_type=F32)
            if c == len(bounds) - 1:
                h_ref[...] = h_next[0]
            act = (gate * jax.nn.sigmoid(gate) * up).astype(BF16)
            acc = acc + jnp.dot(act, wd_ref[c0:c1, :], preferred_element_type=F32)
            for piece in between[c]:
                piece()
        o_ref[...] = acc


def _const_spec(shape):
    nd = len(shape)
    return pl.BlockSpec(shape, lambda *_: (0,) * nd)


def _layer(x2, seq, norm1_g, w_in, q_norm_g, k_norm_g, pool_w, pool_b, pool_scale,
           w_out, norm2_g, w_gate, w_up, w_down):
    n_tok = x2.shape[0]
    batch = n_tok // seq
    d_ff = w_gate.shape[1]
    assert seq % ROW_TILE == 0 and seq % Q_TILE == 0 and ROW_TILE % POOL_HALO == 0
    assert seq % INPROJ_ROWS == 0
    n_row_tiles = n_tok // ROW_TILE
    n_in_tiles = n_tok // INPROJ_ROWS

    cos, sin = _rope_tables(seq)
    seg = jnp.asarray(_segment_ones(256), BF16)
    q_scale = HEAD_DIM ** -0.5 * math.log2(math.e)

    params = pltpu.CompilerParams(dimension_semantics=("arbitrary",),
                                  vmem_limit_bytes=VMEM_LIMIT_BYTES)

    def grouped_rows(width, rows, tile):
        per_seq = seq // rows
        return pl.BlockSpec((1, N_KV_HEADS, rows, width),
                            lambda i: (tile(i) // per_seq, 0, tile(i) % per_seq, 0))

    def lead_tile(n):
        return lambda i: jnp.minimum(i, n - 1)

    mix_tile = lead_tile(n_row_tiles)
    in_tile = lead_tile(n_in_tiles)

    def lag_tile(i):
        return jnp.maximum(i - 1, 0)

    def grouped_seq(width):
        return pl.BlockSpec((1, N_KV_HEADS, seq, width), lambda b: (b, 0, 0, 0))

    q, krep, vext, u = pl.pallas_call(
        functools.partial(_inproj_kernel, q_scale=q_scale, n_tiles=n_in_tiles, seq=seq),
        grid=(n_in_tiles + 1,),
        in_specs=[
            pl.BlockSpec((INPROJ_ROWS, D_MODEL), lambda i: (in_tile(i), 0)),
            _const_spec((1, D_MODEL)),
            _const_spec((D_MODEL, IN_WIDTH)),
            _const_spec((1, HEAD_DIM)),
            _const_spec((1, HEAD_DIM)),
            _const_spec((seq, LANES)),
            _const_spec((seq, LANES)),
            _const_spec((256, 256)),
        ],
        out_specs=[
            grouped_rows(GROUP_WIDTH, INPROJ_ROWS, lag_tile),
            grouped_rows(LANES, INPROJ_ROWS, lag_tile),
            grouped_rows(LANES, INPROJ_ROWS, lag_tile),
            pl.BlockSpec((INPROJ_ROWS, POOL_WIDTH), lambda i: (in_tile(i), 0)),
        ],
        out_shape=[
            jax.ShapeDtypeStruct((batch, N_KV_HEADS, seq, GROUP_WIDTH), BF16),
            jax.ShapeDtypeStruct((batch, N_KV_HEADS, seq, LANES), BF16),
            jax.ShapeDtypeStruct((batch, N_KV_HEADS, seq, LANES), BF16),
            jax.ShapeDtypeStruct((n_tok, POOL_WIDTH), F32),
        ],
        scratch_shapes=[pltpu.VMEM((INPROJ_ROWS, IN_WIDTH - POOL_WIDTH), F32),
                        pltpu.VMEM((INPROJ_ROWS, IN_WIDTH - POOL_WIDTH), F32),
                        pltpu.VMEM((D_MODEL, IN_WIDTH), BF16)],
        compiler_params=params,
        name="inproj",
    )(x2, norm1_g.astype(F32)[None, :], w_in.astype(F32),
      q_norm_g.astype(F32)[None, :], k_norm_g.astype(F32)[None, :],
      jnp.asarray(cos), jnp.asarray(sin), seg)

    late_weights = [w_out, w_gate, w_up, w_down]
    for w in late_weights:
        assert w.shape[0] % (batch * BF16_SUBLANES) == 0

    def next_lead(rows, width):
        return pl.BlockSpec((1, 1, rows, width),
                            lambda b: (jnp.minimum(b + 1, batch - 1), 0, 0, 0))

    def row_slab(w):
        return pl.BlockSpec((w.shape[0] // batch, w.shape[1]), lambda b: (b, 0))

    attn, w_out_b, w_gate_b, w_up_b, w_down_b = pl.pallas_call(
        functools.partial(_attention_kernel, seq=seq),
        grid=(batch,),
        in_specs=([grouped_seq(GROUP_WIDTH), grouped_seq(LANES), grouped_seq(LANES),
                   next_lead(NEXT_TILES * Q_TILE, GROUP_WIDTH), next_lead(seq, LANES)]
                  + [row_slab(w) for w in late_weights]),
        out_specs=[grouped_seq(GROUP_WIDTH)] + [row_slab(w) for w in late_weights],
        out_shape=([jax.ShapeDtypeStruct((batch, N_KV_HEADS, seq, GROUP_WIDTH), BF16)]
                   + [jax.ShapeDtypeStruct(w.shape, BF16) for w in late_weights]),
        scratch_shapes=([pltpu.VMEM((Q_PER_KV * Q_TILE, seq), F32)] * N_SCORE_BUFS
                        + [pltpu.VMEM((Q_PER_KV * Q_TILE, LANES), F32)] * N_SCORE_BUFS
                        + [pltpu.VMEM((Q_PER_KV * Q_TILE, LANES), F32)] * N_SCORE_BUFS
                        + [pltpu.VMEM((Q_PER_KV * Q_TILE, LANES), F32)] * N_SCORE_BUFS),
        compiler_params=params,
        name="attention",
    )(q, krep, vext, q, krep, *[w.astype(F32) for w in late_weights])

    halo_blocks = ROW_TILE // POOL_HALO
    last_halo = n_tok // POOL_HALO - 1
    out = pl.pallas_call(
        functools.partial(_mix_ffn_kernel, seq=seq, d_ff=d_ff, n_tiles=n_row_tiles),
        grid=(n_row_tiles + 1,),
        in_specs=[
            pl.BlockSpec((ROW_TILE, D_MODEL), lambda i: (mix_tile(i), 0)),
            grouped_rows(GROUP_WIDTH, ROW_TILE, mix_tile),
            pl.BlockSpec((ROW_TILE, POOL_WIDTH), lambda i: (mix_tile(i), 0)),
            pl.BlockSpec((POOL_HALO, POOL_WIDTH),
                         lambda i: (jnp.maximum(mix_tile(i) * halo_blocks - 1, 0), 0)),
            pl.BlockSpec((POOL_HALO, POOL_WIDTH),
                         lambda i: (jnp.minimum((mix_tile(i) + 1) * halo_blocks, last_halo), 0)),
            _const_spec((N_POOL_GROUPS, POOL_GROUP_DIM, POOL_GROUP_DIM)),
            _const_spec((N_POOL_GROUPS, 1, POOL_GROUP_DIM)),
            _const_spec((1, POOL_WIDTH)),
            _const_spec((D_MODEL, D_MODEL)),
            _const_spec((1, D_MODEL)),
            _const_spec((D_MODEL, d_ff)),
            _const_spec((D_MODEL, d_ff)),
            _const_spec((d_ff, D_MODEL)),
        ],
        out_specs=pl.BlockSpec((ROW_TILE, D_MODEL), lambda i: (jnp.maximum(i - 1, 0), 0)),
        out_shape=jax.ShapeDtypeStruct((n_tok, D_MODEL), F32),
        scratch_shapes=[pltpu.VMEM((ROW_TILE + 2 * POOL_HALO, POOL_WIDTH), F32),
                        pltpu.VMEM((ROW_TILE, POOL_WIDTH), BF16),
                        pltpu.VMEM((ROW_TILE, D_MODEL), F32), pltpu.VMEM((ROW_TILE, D_MODEL), BF16)],
        compiler_params=params,
        name="mix_ffn",
    )(x2, attn, u, u, u, pool_w.astype(F32), pool_b.astype(F32)[:, None, :],
      pool_scale.astype(F32)[None, :], w_out_b, norm2_g.astype(F32)[None, :],
      w_gate_b, w_up_b, w_down_b)
    return out


def kernel(x, norm1_g, w_in, q_norm_g, k_norm_g, pool_w, pool_b, pool_scale,
           w_out, norm2_g, w_gate, w_up, w_down):
    batch, seq, d_model = x.shape
    assert d_model == D_MODEL and w_in.shape[-1] == IN_WIDTH
    x2 = x.reshape(batch * seq, d_model)
    for l in range(norm1_g.shape[0]):
        x2 = _layer(x2, seq, norm1_g[l], w_in[l], q_norm_g[l], k_norm_g[l], pool_w[l],
                    pool_b[l], pool_scale[l], w_out[l], norm2_g[l], w_gate[l], w_up[l],
                    w_down[l])
    return x2.reshape(batch, seq, d_model)
```

```python
import functools
import math

import jax
import jax.numpy as jnp
import numpy as np
from jax import lax
from jax.experimental import pallas as pl
from jax.experimental.pallas import tpu as pltpu

D_MODEL = 1024
HEAD_DIM = 64
N_Q_HEADS = 8
N_KV_HEADS = 2
Q_PER_KV = N_Q_HEADS // N_KV_HEADS
ATTN_WIDTH = N_Q_HEADS * HEAD_DIM
KV_WIDTH = N_KV_HEADS * HEAD_DIM
GROUP_WIDTH = Q_PER_KV * HEAD_DIM
ROPE_PAIRS = HEAD_DIM // 4
ROPE_THETA = 10000.0
POOL_WINDOWS = (2, 4, 8, 16)
N_POOL_GROUPS = len(POOL_WINDOWS)
POOL_WIDTH = D_MODEL - ATTN_WIDTH
POOL_GROUP_DIM = POOL_WIDTH // N_POOL_GROUPS
IN_WIDTH = ATTN_WIDTH + 2 * KV_WIDTH + POOL_WIDTH
GRID_W = 64
EPS = 1e-6

LANES = 128
KEY_LANES = slice(GROUP_WIDTH, GROUP_WIDTH + LANES)
VALUE_LANES = slice(GROUP_WIDTH + LANES, GROUP_WIDTH + 2 * LANES)
QKV_WIDTH = GROUP_WIDTH + 2 * LANES
BF16_SUBLANES = 16
POOL_HALO = 8
VMEM_LIMIT_BYTES = 56 * 1024 * 1024

ROW_TILE = 512
INPROJ_ROWS = 1024
Q_TILE = 128
ATTN_SLOTS_PER_STEP = 4
N_SCORE_BUFS = 4
NEXT_TILES = 2
FF_CHUNK = 512
INPROJ_CHUNK = 256

F32 = jnp.float32
BF16 = jnp.bfloat16


def _lane_iota(shape):
    return lax.broadcasted_iota(jnp.int32, shape, len(shape) - 1)


def _rope_tables(seq):
    t = np.arange(seq)
    row = (t // GRID_W).astype(np.float64)
    col = (t % GRID_W).astype(np.float64)
    inv_freq = ROPE_THETA ** (-np.arange(ROPE_PAIRS, dtype=np.float64) / ROPE_PAIRS)
    ang_row = row[:, None] * inv_freq[None, :]
    ang_col = col[:, None] * inv_freq[None, :]
    ang = np.concatenate([ang_row, ang_row, ang_col, ang_col], axis=1)
    sign = np.concatenate([-np.ones(ROPE_PAIRS), np.ones(ROPE_PAIRS)] * 2)
    cos = np.tile(np.cos(ang), (1, 2)).astype(np.float32)
    sin = np.tile(np.sin(ang) * sign[None, :], (1, 2)).astype(np.float32)
    return cos, sin


def _segment_ones(width):
    idx = np.arange(width) // HEAD_DIM
    return (idx[:, None] == idx[None, :]).astype(np.float32)


def _head_sumsq(x, seg):
    sq = x * x
    hi = sq.astype(BF16)
    lo = (sq - hi.astype(F32)).astype(BF16)
    return (jnp.dot(hi, seg, preferred_element_type=F32)
            + jnp.dot(lo, seg, preferred_element_type=F32))


def _rope_partner(x):
    lane = _lane_iota(x.shape)
    fwd = pltpu.roll(x, LANES - ROPE_PAIRS, axis=1)
    bwd = pltpu.roll(x, ROPE_PAIRS, axis=1)
    return jnp.where((lane & ROPE_PAIRS) == 0, fwd, bwd)


def _norm_rope(x, sumsq, gain, cos, sin):
    y = x * lax.rsqrt(sumsq * (1.0 / HEAD_DIM) + EPS) * gain
    return y * cos + _rope_partner(y) * sin


def _inproj_kernel(x_ref, g1_ref, win_ref, qg_ref, kg_ref, cos_ref, sin_ref, seg_ref,
                   qkv_ref, u_ref, proja_ref, projb_ref, wbf_ref,
                   *, q_scale, n_tiles, seq):
    step = pl.program_id(0)
    rows = x_ref.shape[0]
    kv0 = ATTN_WIDTH
    u0 = ATTN_WIDTH + 2 * KV_WIDTH
    pos0 = pl.multiple_of((jnp.maximum(step - 1, 0) % (seq // rows)) * rows, rows)

    def rope_tables():
        return cos_ref[pl.ds(pos0, rows), :], sin_ref[pl.ds(pos0, rows), :]

    def head_pair_gain(g_ref):
        g = g_ref[...]
        return jnp.concatenate([g] * (LANES // HEAD_DIM), axis=1)

    def normed_input():
        x = x_ref[...]
        ms = jnp.mean(x * x, axis=-1, keepdims=True)
        return (x * lax.rsqrt(ms + EPS) * g1_ref[...]).astype(BF16)

    def finish_q(src_ref, g):
        qh = src_ref[:, g * GROUP_WIDTH:(g + 1) * GROUP_WIDTH]
        ss = _head_sumsq(qh, seg_ref[...])
        for j in range(GROUP_WIDTH // LANES):
            sl = slice(j * LANES, (j + 1) * LANES)
            qr = _norm_rope(qh[:, sl], ss[:, sl], head_pair_gain(qg_ref), *rope_tables()) * q_scale
            qkv_ref[0, g, :, sl] = qr.astype(BF16)

    def finish_kv(src_ref):
        k = src_ref[:, kv0:kv0 + KV_WIDTH]
        kss = _head_sumsq(k, seg_ref[:KV_WIDTH, :KV_WIDTH])
        kr = _norm_rope(k, kss, head_pair_gain(kg_ref), *rope_tables())
        first = _lane_iota(kr.shape) < HEAD_DIM
        kswap = pltpu.roll(kr, HEAD_DIM, axis=1)
        qkv_ref[0, 0, :, KEY_LANES] = jnp.where(first, kr, kswap).astype(BF16)
        qkv_ref[0, 1, :, KEY_LANES] = jnp.where(first, kswap, kr).astype(BF16)
        v = src_ref[:, kv0 + KV_WIDTH:kv0 + 2 * KV_WIDTH]
        vswap = pltpu.roll(v, HEAD_DIM, axis=1)
        qkv_ref[0, 0, :, VALUE_LANES] = jnp.where(first, v, 1.0).astype(BF16)
        qkv_ref[0, 1, :, VALUE_LANES] = jnp.where(first, vswap, 1.0).astype(BF16)

    def finish_pieces(src_ref):
        return [functools.partial(finish_q, src_ref, 0), functools.partial(finish_q, src_ref, 1),
                functools.partial(finish_kv, src_ref)]

    def project(dst_ref, between=()):
        assert u0 % INPROJ_CHUNK == 0
        h = normed_input()
        for c in range(IN_WIDTH // INPROJ_CHUNK):
            c0 = c * INPROJ_CHUNK
            res = jnp.dot(h, wbf_ref[:, c0:c0 + INPROJ_CHUNK], preferred_element_type=F32)
            if c0 < u0:
                dst_ref[:, c0:c0 + INPROJ_CHUNK] = res
            else:
                u_ref[:, c0 - u0:c0 - u0 + INPROJ_CHUNK] = res
            if c < len(between):
                between[c]()

    even = step % 2 == 0
    assert n_tiles % 2 == 0

    @pl.when(step == 0)
    def _():
        wbf_ref[...] = win_ref[...].astype(BF16)
        project(proja_ref)

    @pl.when(jnp.logical_and(even, jnp.logical_and(step > 0, step < n_tiles)))
    def _():
        project(proja_ref, finish_pieces(projb_ref))

    @pl.when(jnp.logical_not(even))
    def _():
        project(projb_ref, finish_pieces(proja_ref))

    @pl.when(step == n_tiles)
    def _():
        for piece in finish_pieces(projb_ref):
            piece()


def _attention_kernel(qkv_ref, qn_ref, kn_ref,
                      wf0_ref, wf1_ref, wf2_ref, wf3_ref,
                      o_ref, wb0_ref, wb1_ref, wb2_ref, wb3_ref, *scratch, seq):
    tiles_per_group = seq // Q_TILE
    n_tiles = N_KV_HEADS * tiles_per_group
    lane_o = _lane_iota((Q_TILE, LANES))

    def tile_pos(t):
        g = t // tiles_per_group
        r0 = pl.multiple_of((t % tiles_per_group) * Q_TILE, Q_TILE)
        return g, r0

    def scores(t, s_ref, mv_ref):
        if isinstance(t, int) and t >= n_tiles:
            assert t - n_tiles < NEXT_TILES
            qt = qn_ref[0, 0, (t - n_tiles) * Q_TILE:(t - n_tiles + 1) * Q_TILE, :]
            keys = kn_ref[0, 0]
        else:
            g, r0 = tile_pos(t)
            qt = qkv_ref[0, g, pl.ds(r0, Q_TILE), :GROUP_WIDTH]
            keys = qkv_ref[0, g, :, KEY_LANES]
        halves = [qt[:, j * LANES:(j + 1) * LANES] for j in range(GROUP_WIDTH // LANES)]
        zero = jnp.zeros_like(halves[0])
        per_half = LANES // HEAD_DIM
        qm = jnp.concatenate(
            [jnp.where((lane_o // HEAD_DIM) == hh % per_half, halves[hh // per_half], zero)
             for hh in range(Q_PER_KV)], axis=0)
        s = lax.dot_general(qm, keys, (((1,), (1,)), ((), ())),
                            preferred_element_type=F32)
        s_ref[...] = s
        mv_ref[...] = functools.reduce(
            jnp.maximum, [s[:, j * LANES:(j + 1) * LANES] for j in range(seq // LANES)])

    def row_max(mv_ref, m_ref):
        m_ref[...] = jnp.broadcast_to(jnp.max(mv_ref[...], axis=-1, keepdims=True), m_ref.shape)

    def output(t, s_ref, m_ref, oe_ref):
        g, _ = tile_pos(t)
        m = jnp.concatenate([m_ref[...]] * (seq // LANES), axis=1)
        p = jnp.exp2(s_ref[...] - m).astype(BF16)
        oe_ref[...] = jnp.dot(p, qkv_ref[0, g, :, VALUE_LANES], preferred_element_type=F32)

    def finalize(t, oe_ref):
        g, r0 = tile_pos(t)
        low = lane_o < HEAD_DIM
        for j in range(Q_PER_KV // 2):
            even = oe_ref[2 * j * Q_TILE:(2 * j + 1) * Q_TILE, :]
            odd = oe_ref[(2 * j + 1) * Q_TILE:(2 * j + 2) * Q_TILE, :]
            even_sw = pltpu.roll(even, HEAD_DIM, axis=1)
            odd_sw = pltpu.roll(odd, HEAD_DIM, axis=1)
            pair = jnp.where(low, even, odd_sw) / jnp.where(low, even_sw, odd)
            o_ref[0, g, pl.ds(r0, Q_TILE), j * LANES:(j + 1) * LANES] = pair.astype(BF16)

    n_buf = N_SCORE_BUFS
    assert n_tiles % n_buf == 0 and tiles_per_group >= NEXT_TILES and len(scratch) == 4 * n_buf
    s_refs, mv_refs = scratch[:n_buf], scratch[n_buf:2 * n_buf]
    m_refs, oe_refs = scratch[2 * n_buf:3 * n_buf], scratch[3 * n_buf:]

    def slot(t, k, first=False, side_work=None):
        row_max(mv_refs[(k + 1) % n_buf], m_refs[(k + 1) % n_buf])
        if not first:
            finalize(t - 1, oe_refs[(k - 1) % n_buf])
        scores(t + NEXT_TILES, s_refs[(k + NEXT_TILES) % n_buf], mv_refs[(k + NEXT_TILES) % n_buf])
        if side_work is not None:
            side_work()
        output(t, s_refs[k % n_buf], m_refs[k % n_buf], oe_refs[k % n_buf])

    def round_weights(wf_ref, wb_ref):
        wb_ref[...] = wf_ref[...].astype(BF16)

    side = [functools.partial(round_weights, wf, wb)
            for wf, wb in ((wf0_ref, wb0_ref), (wf1_ref, wb1_ref),
                           (wf2_ref, wb2_ref), (wf3_ref, wb3_ref))]

    @pl.when(pl.program_id(0) == 0)
    def _():
        for t in range(NEXT_TILES):
            scores(t, s_refs[t], mv_refs[t])
        row_max(mv_refs[0], m_refs[0])

    n_loop = (n_tiles - NEXT_TILES - 1) // ATTN_SLOTS_PER_STEP * ATTN_SLOTS_PER_STEP
    assert n_loop > 0 and ATTN_SLOTS_PER_STEP % n_buf == 0
    slot(0, 0, first=True, side_work=side.pop())

    def body(jj, carry):
        for i in range(ATTN_SLOTS_PER_STEP):
            slot(1 + jj * ATTN_SLOTS_PER_STEP + i, (1 + i) % n_buf)
        return carry

    lax.fori_loop(0, n_loop // ATTN_SLOTS_PER_STEP, body, 0)

    for t in range(1 + n_loop, n_tiles):
        slot(t, t % n_buf, side_work=side.pop() if side else None)
    assert not side
    finalize(n_tiles - 1, oe_refs[(n_tiles - 1) % n_buf])


def _mix_ffn_kernel(x_ref, a_ref, u_ref, uprev_ref, unext_ref, pw_ref, pb_ref, ps_ref,
                    wout_ref, g2_ref, wg_ref, wu_ref, wd_ref, o_ref,
                    ext_ref, mixp_ref, x1_ref, h_ref, *, seq, d_ff, n_tiles):
    rows = x_ref.shape[0]
    tiles_per_seq = seq // rows
    step = pl.program_id(0)

    st = jnp.minimum(step, n_tiles - 1) % tiles_per_seq

    def stage_windows():
        ext_ref[0:POOL_HALO, :] = jnp.where(st > 0, uprev_ref[...], 0.0)
        ext_ref[POOL_HALO:POOL_HALO + rows, :] = u_ref[...]
        ext_ref[POOL_HALO + rows:, :] = jnp.where(st < tiles_per_seq - 1, unext_ref[...], 0.0)

    def pool_group(g):
        n_ext = rows + 2 * POOL_HALO
        w = POOL_WINDOWS[g]
        assert w // 2 <= POOL_HALO
        cs = slice(g * POOL_GROUP_DIM, (g + 1) * POOL_GROUP_DIM)
        ext = ext_ref[:, cs]
        f = ext
        k = 1
        while k < w // 2:
            f = f + pltpu.roll(f, n_ext - k, axis=0)
            k *= 2
        both = f + pltpu.roll(f, w // 2, axis=0)
        win = both[POOL_HALO:POOL_HALO + rows]

        def edge_mean(r0):
            t = st * rows + r0 + lax.broadcasted_iota(jnp.int32, (POOL_HALO, LANES), 0)
            lo = jnp.maximum(t - w // 2, 0)
            hi = jnp.minimum(t - w // 2 + w, seq)
            return win[r0:r0 + POOL_HALO] / (hi - lo).astype(F32)

        mean = jnp.concatenate([edge_mean(0), win[POOL_HALO:rows - POOL_HALO] * (1.0 / w),
                                edge_mean(rows - POOL_HALO)], axis=0)
        pooled = (mean - ext[POOL_HALO:POOL_HALO + rows]).astype(BF16)
        y = jnp.dot(pooled, pw_ref[g].astype(BF16), preferred_element_type=F32) + pb_ref[g]
        mixp_ref[:, cs] = (y * ps_ref[:, cs]).astype(BF16)

    def project_out():
        mix = jnp.concatenate([a_ref[0, g] for g in range(N_KV_HEADS)] + [mixp_ref[...]],
                              axis=1)
        x1 = x_ref[...] + jnp.dot(mix, wout_ref[...], preferred_element_type=F32)
        ms = jnp.mean(x1 * x1, axis=-1, keepdims=True)
        x1_ref[...] = x1
        return (x1 * lax.rsqrt(ms + EPS) * g2_ref[...]).astype(BF16)

    bounds = [(c0, min(c0 + FF_CHUNK, d_ff)) for c0 in range(0, d_ff, FF_CHUNK)]

    @pl.when(step == 0)
    def _():
        stage_windows()
        for g in range(N_POOL_GROUPS):
            pool_group(g)
        h_ref[...] = project_out()

    @pl.when(step > 0)
    def _():
        stage_windows()
        h = h_ref[...]
        acc = x1_ref[...]
        h_next = []
        between = [[functools.partial(pool_group, g)] for g in range(N_POOL_GROUPS)]
        between += [[lambda: h_next.append(project_out())], []]
        assert len(between) == len(bounds)
        for c, (c0, c1) in enumerate(bounds):
            gate = jnp.dot(h, wg_ref[:, c0:c1], preferred_element_type=F32)
            up = jnp.dot(h, wu_ref[:, c0:c1], preferred_element_type=F32)
            if c == len(bounds) - 1:
                h_ref[...] = h_next[0]
            act = (gate * jax.nn.sigmoid(gate) * up).astype(BF16)
            acc = acc + jnp.dot(act, wd_ref[c0:c1, :], preferred_element_type=F32)
            for piece in between[c]:
                piece()
        o_ref[...] = acc


def _const_spec(shape):
    nd = len(shape)
    return pl.BlockSpec(shape, lambda *_: (0,) * nd)


def _layer(x2, seq, norm1_g, w_in, q_norm_g, k_norm_g, pool_w, pool_b, pool_scale,
           w_out, norm2_g, w_gate, w_up, w_down):
    n_tok = x2.shape[0]
    batch = n_tok // seq
    d_ff = w_gate.shape[1]
    assert seq % ROW_TILE == 0 and seq % Q_TILE == 0 and ROW_TILE % POOL_HALO == 0
    assert seq % INPROJ_ROWS == 0
    n_row_tiles = n_tok // ROW_TILE
    n_in_tiles = n_tok // INPROJ_ROWS

    cos, sin = _rope_tables(seq)
    seg = jnp.asarray(_segment_ones(256), BF16)
    q_scale = HEAD_DIM ** -0.5 * math.log2(math.e)

    params = pltpu.CompilerParams(dimension_semantics=("arbitrary",),
                                  vmem_limit_bytes=VMEM_LIMIT_BYTES)

    def grouped_rows(width, rows, tile):
        per_seq = seq // rows
        return pl.BlockSpec((1, N_KV_HEADS, rows, width),
                            lambda i: (tile(i) // per_seq, 0, tile(i) % per_seq, 0))

    def lead_tile(n):
        return lambda i: jnp.minimum(i, n - 1)

    mix_tile = lead_tile(n_row_tiles)
    in_tile = lead_tile(n_in_tiles)

    def lag_tile(i):
        return jnp.maximum(i - 1, 0)

    def grouped_seq(width):
        return pl.BlockSpec((1, N_KV_HEADS, seq, width), lambda b: (b, 0, 0, 0))

    qkv, u = pl.pallas_call(
        functools.partial(_inproj_kernel, q_scale=q_scale, n_tiles=n_in_tiles, seq=seq),
        grid=(n_in_tiles + 1,),
        in_specs=[
            pl.BlockSpec((INPROJ_ROWS, D_MODEL), lambda i: (in_tile(i), 0)),
            _const_spec((1, D_MODEL)),
            _const_spec((D_MODEL, IN_WIDTH)),
            _const_spec((1, HEAD_DIM)),
            _const_spec((1, HEAD_DIM)),
            _const_spec((seq, LANES)),
            _const_spec((seq, LANES)),
            _const_spec((256, 256)),
        ],
        out_specs=[
            grouped_rows(QKV_WIDTH, INPROJ_ROWS, lag_tile),
            pl.BlockSpec((INPROJ_ROWS, POOL_WIDTH), lambda i: (in_tile(i), 0)),
        ],
        out_shape=[
            jax.ShapeDtypeStruct((batch, N_KV_HEADS, seq, QKV_WIDTH), BF16),
            jax.ShapeDtypeStruct((n_tok, POOL_WIDTH), F32),
        ],
        scratch_shapes=[pltpu.VMEM((INPROJ_ROWS, IN_WIDTH - POOL_WIDTH), F32),
                        pltpu.VMEM((INPROJ_ROWS, IN_WIDTH - POOL_WIDTH), F32),
                        pltpu.VMEM((D_MODEL, IN_WIDTH), BF16)],
        compiler_params=params,
        name="inproj",
    )(x2, norm1_g.astype(F32)[None, :], w_in.astype(F32),
      q_norm_g.astype(F32)[None, :], k_norm_g.astype(F32)[None, :],
      jnp.asarray(cos), jnp.asarray(sin), seg)

    late_weights = [w_out, w_gate, w_up, w_down]
    for w in late_weights:
        assert w.shape[0] % (batch * BF16_SUBLANES) == 0

    def next_lead(rows, lanes):
        assert lanes.start % (lanes.stop - lanes.start) == 0
        width = lanes.stop - lanes.start
        return pl.BlockSpec((1, 1, rows, width),
                            lambda b: (jnp.minimum(b + 1, batch - 1), 0, 0, lanes.start // width))

    def row_slab(w):
        return pl.BlockSpec((w.shape[0] // batch, w.shape[1]), lambda b: (b, 0))

    attn, w_out_b, w_gate_b, w_up_b, w_down_b = pl.pallas_call(
        functools.partial(_attention_kernel, seq=seq),
        grid=(batch,),
        in_specs=([grouped_seq(QKV_WIDTH),
                   next_lead(NEXT_TILES * Q_TILE, slice(0, GROUP_WIDTH)), next_lead(seq, KEY_LANES)]
                  + [row_slab(w) for w in late_weights]),
        out_specs=[grouped_seq(GROUP_WIDTH)] + [row_slab(w) for w in late_weights],
        out_shape=([jax.ShapeDtypeStruct((batch, N_KV_HEADS, seq, GROUP_WIDTH), BF16)]
                   + [jax.ShapeDtypeStruct(w.shape, BF16) for w in late_weights]),
        scratch_shapes=([pltpu.VMEM((Q_PER_KV * Q_TILE, seq), F32)] * N_SCORE_BUFS
                        + [pltpu.VMEM((Q_PER_KV * Q_TILE, LANES), F32)] * N_SCORE_BUFS
                        + [pltpu.VMEM((Q_PER_KV * Q_TILE, LANES), F32)] * N_SCORE_BUFS
                        + [pltpu.VMEM((Q_PER_KV * Q_TILE, LANES), F32)] * N_SCORE_BUFS),
        compiler_params=params,
        name="attention",
    )(qkv, qkv, qkv, *[w.astype(F32) for w in late_weights])

    halo_blocks = ROW_TILE // POOL_HALO
    last_halo = n_tok // POOL_HALO - 1
    out = pl.pallas_call(
        functools.partial(_mix_ffn_kernel, seq=seq, d_ff=d_ff, n_tiles=n_row_tiles),
        grid=(n_row_tiles + 1,),
        in_specs=[
            pl.BlockSpec((ROW_TILE, D_MODEL), lambda i: (mix_tile(i), 0)),
            grouped_rows(GROUP_WIDTH, ROW_TILE, mix_tile),
            pl.BlockSpec((ROW_TILE, POOL_WIDTH), lambda i: (mix_tile(i), 0)),
            pl.BlockSpec((POOL_HALO, POOL_WIDTH),
                         lambda i: (jnp.maximum(mix_tile(i) * halo_blocks - 1, 0), 0)),
            pl.BlockSpec((POOL_HALO, POOL_WIDTH),
                         lambda i: (jnp.minimum((mix_tile(i) + 1) * halo_blocks, last_halo), 0)),
            _const_spec((N_POOL_GROUPS, POOL_GROUP_DIM, POOL_GROUP_DIM)),
            _const_spec((N_POOL_GROUPS, 1, POOL_GROUP_DIM)),
            _const_spec((1, POOL_WIDTH)),
            _const_spec((D_MODEL, D_MODEL)),
            _const_spec((1, D_MODEL)),
            _const_spec((D_MODEL, d_ff)),
            _const_spec((D_MODEL, d_ff)),
            _const_spec((d_ff, D_MODEL)),
        ],
        out_specs=pl.BlockSpec((ROW_TILE, D_MODEL), lambda i: (jnp.maximum(i - 1, 0), 0)),
        out_shape=jax.ShapeDtypeStruct((n_tok, D_MODEL), F32),
        scratch_shapes=[pltpu.VMEM((ROW_TILE + 2 * POOL_HALO, POOL_WIDTH), F32),
                        pltpu.VMEM((ROW_TILE, POOL_WIDTH), BF16),
                        pltpu.VMEM((ROW_TILE, D_MODEL), F32), pltpu.VMEM((ROW_TILE, D_MODEL), BF16)],
        compiler_params=params,
        name="mix_ffn",
    )(x2, attn, u, u, u, pool_w.astype(F32), pool_b.astype(F32)[:, None, :],
      pool_scale.astype(F32)[None, :], w_out_b, norm2_g.astype(F32)[None, :],
      w_gate_b, w_up_b, w_down_b)
    return out


def kernel(x, norm1_g, w_in, q_norm_g, k_norm_g, pool_w, pool_b, pool_scale,
           w_out, norm2_g, w_gate, w_up, w_down):
    batch, seq, d_model = x.shape
    assert d_model == D_MODEL and w_in.shape[-1] == IN_WIDTH
    x2 = x.reshape(batch * seq, d_model)
    for l in range(norm1_g.shape[0]):
        x2 = _layer(x2, seq, norm1_g[l], w_in[l], q_norm_g[l], k_norm_g[l], pool_w[l],
                    pool_b[l], pool_scale[l], w_out[l], norm2_g[l], w_gate[l], w_up[l],
                    w_down[l])
    return x2.reshape(batch, seq, d_model)
```

```python
import functools
import math

import jax
import jax.numpy as jnp
import numpy as np
from jax import lax
from jax.experimental import pallas as pl
from jax.experimental.pallas import tpu as pltpu

D_MODEL = 1024
HEAD_DIM = 64
N_Q_HEADS = 8
N_KV_HEADS = 2
Q_PER_KV = N_Q_HEADS // N_KV_HEADS
ATTN_WIDTH = N_Q_HEADS * HEAD_DIM
KV_WIDTH = N_KV_HEADS * HEAD_DIM
GROUP_WIDTH = Q_PER_KV * HEAD_DIM
ROPE_PAIRS = HEAD_DIM // 4
ROPE_THETA = 10000.0
POOL_WINDOWS = (2, 4, 8, 16)
N_POOL_GROUPS = len(POOL_WINDOWS)
POOL_WIDTH = D_MODEL - ATTN_WIDTH
POOL_GROUP_DIM = POOL_WIDTH // N_POOL_GROUPS
IN_WIDTH = ATTN_WIDTH + 2 * KV_WIDTH + POOL_WIDTH
GRID_W = 64
EPS = 1e-6

LANES = 128
KEY_LANES = slice(GROUP_WIDTH, GROUP_WIDTH + LANES)
VALUE_LANES = slice(GROUP_WIDTH + LANES, GROUP_WIDTH + 2 * LANES)
QKV_WIDTH = GROUP_WIDTH + 2 * LANES
BF16_SUBLANES = 16
POOL_HALO = 8
VMEM_LIMIT_BYTES = 56 * 1024 * 1024

ROW_TILE = 512
INPROJ_ROWS = 1024
Q_TILE = 128
ATTN_SLOTS_PER_STEP = 4
N_SCORE_BUFS = 4
NEXT_TILES = 2
FF_CHUNK = 512
INPROJ_CHUNK = 256

F32 = jnp.float32
BF16 = jnp.bfloat16


def _lane_iota(shape):
    return lax.broadcasted_iota(jnp.int32, shape, len(shape) - 1)


def _rope_tables(seq):
    t = np.arange(seq)
    row = (t // GRID_W).astype(np.float64)
    col = (t % GRID_W).astype(np.float64)
    inv_freq = ROPE_THETA ** (-np.arange(ROPE_PAIRS, dtype=np.float64) / ROPE_PAIRS)
    ang_row = row[:, None] * inv_freq[None, :]
    ang_col = col[:, None] * inv_freq[None, :]
    ang = np.concatenate([ang_row, ang_row, ang_col, ang_col], axis=1)
    sign = np.concatenate([-np.ones(ROPE_PAIRS), np.ones(ROPE_PAIRS)] * 2)
    cos = np.tile(np.cos(ang), (1, 2)).astype(np.float32)
    sin = np.tile(np.sin(ang) * sign[None, :], (1, 2)).astype(np.float32)
    return cos, sin


def _segment_ones(width):
    idx = np.arange(width) // HEAD_DIM
    return (idx[:, None] == idx[None, :]).astype(np.float32)


def _head_sumsq(x, seg):
    sq = x * x
    hi = sq.astype(BF16)
    lo = (sq - hi.astype(F32)).astype(BF16)
    return (jnp.dot(hi, seg, preferred_element_type=F32)
            + jnp.dot(lo, seg, preferred_element_type=F32))


def _rope_partner(x):
    lane = _lane_iota(x.shape)
    fwd = pltpu.roll(x, LANES - ROPE_PAIRS, axis=1)
    bwd = pltpu.roll(x, ROPE_PAIRS, axis=1)
    return jnp.where((lane & ROPE_PAIRS) == 0, fwd, bwd)


def _norm_rope(x, sumsq, gain, cos, sin):
    y = x * lax.rsqrt(sumsq * (1.0 / HEAD_DIM) + EPS) * gain
    return y * cos + _rope_partner(y) * sin


def _inproj_kernel(x_ref, g1_ref, win_ref, qg_ref, kg_ref, cos_ref, sin_ref, seg_ref,
                   qkv_ref, u_ref, proja_ref, projb_ref, wbf_ref,
                   *, q_scale, n_tiles, seq):
    step = pl.program_id(0)
    rows = x_ref.shape[0]
    kv0 = ATTN_WIDTH
    u0 = ATTN_WIDTH + 2 * KV_WIDTH
    pos0 = pl.multiple_of((jnp.maximum(step - 1, 0) % (seq // rows)) * rows, rows)

    def rope_tables():
        return cos_ref[pl.ds(pos0, rows), :], sin_ref[pl.ds(pos0, rows), :]

    def head_pair_gain(g_ref):
        g = g_ref[...]
        return jnp.concatenate([g] * (LANES // HEAD_DIM), axis=1)

    def normed_input():
        x = x_ref[...]
        ms = jnp.mean(x * x, axis=-1, keepdims=True)
        return (x * lax.rsqrt(ms + EPS) * g1_ref[...]).astype(BF16)

    def finish_q(src_ref, g):
        qh = src_ref[:, g * GROUP_WIDTH:(g + 1) * GROUP_WIDTH]
        ss = _head_sumsq(qh, seg_ref[...])
        for j in range(GROUP_WIDTH // LANES):
            sl = slice(j * LANES, (j + 1) * LANES)
            qr = _norm_rope(qh[:, sl], ss[:, sl], head_pair_gain(qg_ref), *rope_tables()) * q_scale
            qkv_ref[0, g, :, sl] = qr.astype(BF16)

    def finish_kv(src_ref):
        k = src_ref[:, kv0:kv0 + KV_WIDTH]
        kss = _head_sumsq(k, seg_ref[:KV_WIDTH, :KV_WIDTH])
        kr = _norm_rope(k, kss, head_pair_gain(kg_ref), *rope_tables())
        first = _lane_iota(kr.shape) < HEAD_DIM
        kswap = pltpu.roll(kr, HEAD_DIM, axis=1)
        qkv_ref[0, 0, :, KEY_LANES] = jnp.where(first, kr, kswap).astype(BF16)
        qkv_ref[0, 1, :, KEY_LANES] = jnp.where(first, kswap, kr).astype(BF16)
        v = src_ref[:, kv0 + KV_WIDTH:kv0 + 2 * KV_WIDTH]
        vswap = pltpu.roll(v, HEAD_DIM, axis=1)
        qkv_ref[0, 0, :, VALUE_LANES] = jnp.where(first, v, 1.0).astype(BF16)
        qkv_ref[0, 1, :, VALUE_LANES] = jnp.where(first, vswap, 1.0).astype(BF16)

    def finish_pieces(src_ref):
        return [functools.partial(finish_q, src_ref, 0), functools.partial(finish_q, src_ref, 1),
                functools.partial(finish_kv, src_ref)]

    def project(dst_ref, between=()):
        assert u0 % INPROJ_CHUNK == 0
        h = normed_input()
        for c in range(IN_WIDTH // INPROJ_CHUNK):
            c0 = c * INPROJ_CHUNK
            res = jnp.dot(h, wbf_ref[:, c0:c0 + INPROJ_CHUNK], preferred_element_type=F32)
            if c0 < u0:
                dst_ref[:, c0:c0 + INPROJ_CHUNK] = res
            else:
                u_ref[:, c0 - u0:c0 - u0 + INPROJ_CHUNK] = res
            if c < len(between):
                between[c]()

    even = step % 2 == 0
    assert n_tiles % 2 == 0

    @pl.when(step == 0)
    def _():
        wbf_ref[...] = win_ref[...].astype(BF16)
        project(proja_ref)

    @pl.when(jnp.logical_and(even, jnp.logical_and(step > 0, step < n_tiles)))
    def _():
        project(proja_ref, finish_pieces(projb_ref))

    @pl.when(jnp.logical_not(even))
    def _():
        project(projb_ref, finish_pieces(proja_ref))

    @pl.when(step == n_tiles)
    def _():
        for piece in finish_pieces(projb_ref):
            piece()


def _attention_kernel(qkv_ref, qn_ref, kn_ref,
                      wf0_ref, wf1_ref, wf2_ref, wf3_ref,
                      o_ref, wb0_ref, wb1_ref, wb2_ref, wb3_ref, *scratch, seq):
    tiles_per_group = seq // Q_TILE
    n_tiles = N_KV_HEADS * tiles_per_group
    lane_o = _lane_iota((Q_TILE, LANES))

    def tile_pos(t):
        g = t // tiles_per_group
        r0 = pl.multiple_of((t % tiles_per_group) * Q_TILE, Q_TILE)
        return g, r0

    def scores(t, s_ref, mv_ref):
        if isinstance(t, int) and t >= n_tiles:
            assert t - n_tiles < NEXT_TILES
            qt = qn_ref[0, 0, (t - n_tiles) * Q_TILE:(t - n_tiles + 1) * Q_TILE, :]
            keys = kn_ref[0, 0]
        else:
            g, r0 = tile_pos(t)
            qt = qkv_ref[0, g, pl.ds(r0, Q_TILE), :GROUP_WIDTH]
            keys = qkv_ref[0, g, :, KEY_LANES]
        halves = [qt[:, j * LANES:(j + 1) * LANES] for j in range(GROUP_WIDTH // LANES)]
        zero = jnp.zeros_like(halves[0])
        per_half = LANES // HEAD_DIM
        qm = jnp.concatenate(
            [jnp.where((lane_o // HEAD_DIM) == hh % per_half, halves[hh // per_half], zero)
             for hh in range(Q_PER_KV)], axis=0)
        s = lax.dot_general(qm, keys, (((1,), (1,)), ((), ())),
                            preferred_element_type=F32)
        s_ref[...] = s
        mv_ref[...] = functools.reduce(
            jnp.maximum, [s[:, j * LANES:(j + 1) * LANES] for j in range(seq // LANES)])

    def row_max(mv_ref, m_ref):
        m_ref[...] = jnp.broadcast_to(jnp.max(mv_ref[...], axis=-1, keepdims=True), m_ref.shape)

    def output(t, s_ref, m_ref, oe_ref):
        g, _ = tile_pos(t)
        m = jnp.concatenate([m_ref[...]] * (seq // LANES), axis=1)
        p = jnp.exp2(s_ref[...] - m).astype(BF16)
        oe_ref[...] = jnp.dot(p, qkv_ref[0, g, :, VALUE_LANES], preferred_element_type=F32)

    def finalize(t, oe_ref):
        g, r0 = tile_pos(t)
        low = lane_o < HEAD_DIM
        for j in range(Q_PER_KV // 2):
            even = oe_ref[2 * j * Q_TILE:(2 * j + 1) * Q_TILE, :]
            odd = oe_ref[(2 * j + 1) * Q_TILE:(2 * j + 2) * Q_TILE, :]
            swapped = pltpu.roll(jnp.where(low, odd, even), HEAD_DIM, axis=1)
            pair = jnp.where(low, even, swapped) / jnp.where(low, swapped, odd)
            o_ref[0, g, pl.ds(r0, Q_TILE), j * LANES:(j + 1) * LANES] = pair.astype(BF16)

    n_buf = N_SCORE_BUFS
    assert n_tiles % n_buf == 0 and tiles_per_group >= NEXT_TILES and len(scratch) == 4 * n_buf
    s_refs, mv_refs = scratch[:n_buf], scratch[n_buf:2 * n_buf]
    m_refs, oe_refs = scratch[2 * n_buf:3 * n_buf], scratch[3 * n_buf:]

    def slot(t, k, first=False, side_work=None):
        row_max(mv_refs[(k + 1) % n_buf], m_refs[(k + 1) % n_buf])
        if not first:
            finalize(t - 1, oe_refs[(k - 1) % n_buf])
        scores(t + NEXT_TILES, s_refs[(k + NEXT_TILES) % n_buf], mv_refs[(k + NEXT_TILES) % n_buf])
        if side_work is not None:
            side_work()
        output(t, s_refs[k % n_buf], m_refs[k % n_buf], oe_refs[k % n_buf])

    def round_weights(wf_ref, wb_ref):
        wb_ref[...] = wf_ref[...].astype(BF16)

    side = [functools.partial(round_weights, wf, wb)
            for wf, wb in ((wf0_ref, wb0_ref), (wf1_ref, wb1_ref),
                           (wf2_ref, wb2_ref), (wf3_ref, wb3_ref))]

    @pl.when(pl.program_id(0) == 0)
    def _():
        for t in range(NEXT_TILES):
            scores(t, s_refs[t], mv_refs[t])
        row_max(mv_refs[0], m_refs[0])

    n_loop = (n_tiles - NEXT_TILES - 1) // ATTN_SLOTS_PER_STEP * ATTN_SLOTS_PER_STEP
    assert n_loop > 0 and ATTN_SLOTS_PER_STEP % n_buf == 0
    slot(0, 0, first=True, side_work=side.pop())

    def body(jj, carry):
        for i in range(ATTN_SLOTS_PER_STEP):
            slot(1 + jj * ATTN_SLOTS_PER_STEP + i, (1 + i) % n_buf)
        return carry

    lax.fori_loop(0, n_loop // ATTN_SLOTS_PER_STEP, body, 0)

    for t in range(1 + n_loop, n_tiles):
        slot(t, t % n_buf, side_work=side.pop() if side else None)
    assert not side
    finalize(n_tiles - 1, oe_refs[(n_tiles - 1) % n_buf])


def _mix_ffn_kernel(x_ref, a_ref, u_ref, uprev_ref, unext_ref, pw_ref, pb_ref, ps_ref,
                    wout_ref, g2_ref, wg_ref, wu_ref, wd_ref, o_ref,
                    ext_ref, mixp_ref, x1_ref, h_ref, *, seq, d_ff, n_tiles):
    rows = x_ref.shape[0]
    tiles_per_seq = seq // rows
    step = pl.program_id(0)

    st = jnp.minimum(step, n_tiles - 1) % tiles_per_seq

    def stage_windows():
        ext_ref[0:POOL_HALO, :] = jnp.where(st > 0, uprev_ref[...], 0.0)
        ext_ref[POOL_HALO:POOL_HALO + rows, :] = u_ref[...]
        ext_ref[POOL_HALO + rows:, :] = jnp.where(st < tiles_per_seq - 1, unext_ref[...], 0.0)

    def pool_group(g):
        n_ext = rows + 2 * POOL_HALO
        w = POOL_WINDOWS[g]
        assert w // 2 <= POOL_HALO
        cs = slice(g * POOL_GROUP_DIM, (g + 1) * POOL_GROUP_DIM)
        ext = ext_ref[:, cs]
        f = ext
        k = 1
        while k < w // 2:
            f = f + pltpu.roll(f, n_ext - k, axis=0)
            k *= 2
        both = f + pltpu.roll(f, w // 2, axis=0)
        win = both[POOL_HALO:POOL_HALO + rows]

        def edge_mean(r0):
            t = st * rows + r0 + lax.broadcasted_iota(jnp.int32, (POOL_HALO, LANES), 0)
            lo = jnp.maximum(t - w // 2, 0)
            hi = jnp.minimum(t - w // 2 + w, seq)
            return win[r0:r0 + POOL_HALO] / (hi - lo).astype(F32)

        mean = jnp.concatenate([edge_mean(0), win[POOL_HALO:rows - POOL_HALO] * (1.0 / w),
                                edge_mean(rows - POOL_HALO)], axis=0)
        pooled = (mean - ext[POOL_HALO:POOL_HALO + rows]).astype(BF16)
        y = jnp.dot(pooled, pw_ref[g].astype(BF16), preferred_element_type=F32) + pb_ref[g]
        mixp_ref[:, cs] = (y * ps_ref[:, cs]).astype(BF16)

    def project_out():
        mix = jnp.concatenate([a_ref[0, g] for g in range(N_KV_HEADS)] + [mixp_ref[...]],
                              axis=1)
        x1 = x_ref[...] + jnp.dot(mix, wout_ref[...], preferred_element_type=F32)
        ms = jnp.mean(x1 * x1, axis=-1, keepdims=True)
        x1_ref[...] = x1
        return (x1 * lax.rsqrt(ms + EPS) * g2_ref[...]).astype(BF16)

    bounds = [(c0, min(c0 + FF_CHUNK, d_ff)) for c0 in range(0, d_ff, FF_CHUNK)]

    @pl.when(step == 0)
    def _():
        stage_windows()
        for g in range(N_POOL_GROUPS):
            pool_group(g)
        h_ref[...] = project_out()

    @pl.when(step > 0)
    def _():
        stage_windows()
        h = h_ref[...]
        acc = x1_ref[...]
        h_next = []
        between = [[functools.partial(pool_group, g)] for g in range(N_POOL_GROUPS)]
        between += [[lambda: h_next.append(project_out())], []]
        assert len(between) == len(bounds)
        for c, (c0, c1) in enumerate(bounds):
            gate = jnp.dot(h, wg_ref[:, c0:c1], preferred_element_type=F32)
            up = jnp.dot(h, wu_ref[:, c0:c1], preferred_element_type=F32)
            if c == len(bounds) - 1:
                h_ref[...] = h_next[0]
            act = (gate * jax.nn.sigmoid(gate) * up).astype(BF16)
            acc = acc + jnp.dot(act, wd_ref[c0:c1, :], preferred_element_type=F32)
            for piece in between[c]:
                piece()
        o_ref[...] = acc


def _const_spec(shape):
    nd = len(shape)
    return pl.BlockSpec(shape, lambda *_: (0,) * nd)


def _layer(x2, seq, norm1_g, w_in, q_norm_g, k_norm_g, pool_w, pool_b, pool_scale,
           w_out, norm2_g, w_gate, w_up, w_down):
    n_tok = x2.shape[0]
    batch = n_tok // seq
    d_ff = w_gate.shape[1]
    assert seq % ROW_TILE == 0 and seq % Q_TILE == 0 and ROW_TILE % POOL_HALO == 0
    assert seq % INPROJ_ROWS == 0
    n_row_tiles = n_tok // ROW_TILE
    n_in_tiles = n_tok // INPROJ_ROWS

    cos, sin = _rope_tables(seq)
    seg = jnp.asarray(_segment_ones(256), BF16)
    q_scale = HEAD_DIM ** -0.5 * math.log2(math.e)

    params = pltpu.CompilerParams(dimension_semantics=("arbitrary",),
                                  vmem_limit_bytes=VMEM_LIMIT_BYTES)

    def grouped_rows(width, rows, tile):
        per_seq = seq // rows
        return pl.BlockSpec((1, N_KV_HEADS, rows, width),
                            lambda i: (tile(i) // per_seq, 0, tile(i) % per_seq, 0))

    def lead_tile(n):
        return lambda i: jnp.minimum(i, n - 1)

    mix_tile = lead_tile(n_row_tiles)
    in_tile = lead_tile(n_in_tiles)

    def lag_tile(i):
        return jnp.maximum(i - 1, 0)

    def grouped_seq(width):
        return pl.BlockSpec((1, N_KV_HEADS, seq, width), lambda b: (b, 0, 0, 0))

    qkv, u = pl.pallas_call(
        functools.partial(_inproj_kernel, q_scale=q_scale, n_tiles=n_in_tiles, seq=seq),
        grid=(n_in_tiles + 1,),
        in_specs=[
            pl.BlockSpec((INPROJ_ROWS, D_MODEL), lambda i: (in_tile(i), 0)),
            _const_spec((1, D_MODEL)),
            _const_spec((D_MODEL, IN_WIDTH)),
            _const_spec((1, HEAD_DIM)),
            _const_spec((1, HEAD_DIM)),
            _const_spec((seq, LANES)),
            _const_spec((seq, LANES)),
            _const_spec((256, 256)),
        ],
        out_specs=[
            grouped_rows(QKV_WIDTH, INPROJ_ROWS, lag_tile),
            pl.BlockSpec((INPROJ_ROWS, POOL_WIDTH), lambda i: (in_tile(i), 0)),
        ],
        out_shape=[
            jax.ShapeDtypeStruct((batch, N_KV_HEADS, seq, QKV_WIDTH), BF16),
            jax.ShapeDtypeStruct((n_tok, POOL_WIDTH), F32),
        ],
        scratch_shapes=[pltpu.VMEM((INPROJ_ROWS, IN_WIDTH - POOL_WIDTH), F32),
                        pltpu.VMEM((INPROJ_ROWS, IN_WIDTH - POOL_WIDTH), F32),
                        pltpu.VMEM((D_MODEL, IN_WIDTH), BF16)],
        compiler_params=params,
        name="inproj",
    )(x2, norm1_g.astype(F32)[None, :], w_in.astype(F32),
      q_norm_g.astype(F32)[None, :], k_norm_g.astype(F32)[None, :],
      jnp.asarray(cos), jnp.asarray(sin), seg)

    late_weights = [w_out, w_gate, w_up, w_down]
    for w in late_weights:
        assert w.shape[0] % (batch * BF16_SUBLANES) == 0

    def next_lead(rows, lanes):
        assert lanes.start % (lanes.stop - lanes.start) == 0
        width = lanes.stop - lanes.start
        return pl.BlockSpec((1, 1, rows, width),
                            lambda b: (jnp.minimum(b + 1, batch - 1), 0, 0, lanes.start // width))

    def row_slab(w):
        return pl.BlockSpec((w.shape[0] // batch, w.shape[1]), lambda b: (b, 0))

    attn, w_out_b, w_gate_b, w_up_b, w_down_b = pl.pallas_call(
        functools.partial(_attention_kernel, seq=seq),
        grid=(batch,),
        in_specs=([grouped_seq(QKV_WIDTH),
                   next_lead(NEXT_TILES * Q_TILE, slice(0, GROUP_WIDTH)), next_lead(seq, KEY_LANES)]
                  + [row_slab(w) for w in late_weights]),
        out_specs=[grouped_seq(GROUP_WIDTH)] + [row_slab(w) for w in late_weights],
        out_shape=([jax.ShapeDtypeStruct((batch, N_KV_HEADS, seq, GROUP_WIDTH), BF16)]
                   + [jax.ShapeDtypeStruct(w.shape, BF16) for w in late_weights]),
        scratch_shapes=([pltpu.VMEM((Q_PER_KV * Q_TILE, seq), F32)] * N_SCORE_BUFS
                        + [pltpu.VMEM((Q_PER_KV * Q_TILE, LANES), F32)] * N_SCORE_BUFS
                        + [pltpu.VMEM((Q_PER_KV * Q_TILE, LANES), F32)] * N_SCORE_BUFS
                        + [pltpu.VMEM((Q_PER_KV * Q_TILE, LANES), F32)] * N_SCORE_BUFS),
        compiler_params=params,
        name="attention",
    )(qkv, qkv, qkv, *[w.astype(F32) for w in late_weights])

    halo_blocks = ROW_TILE // POOL_HALO
    last_halo = n_tok // POOL_HALO - 1
    out = pl.pallas_call(
        functools.partial(_mix_ffn_kernel, seq=seq, d_ff=d_ff, n_tiles=n_row_tiles),
        grid=(n_row_tiles + 1,),
        in_specs=[
            pl.BlockSpec((ROW_TILE, D_MODEL), lambda i: (mix_tile(i), 0)),
            grouped_rows(GROUP_WIDTH, ROW_TILE, mix_tile),
            pl.BlockSpec((ROW_TILE, POOL_WIDTH), lambda i: (mix_tile(i), 0)),
            pl.BlockSpec((POOL_HALO, POOL_WIDTH),
                         lambda i: (jnp.maximum(mix_tile(i) * halo_blocks - 1, 0), 0)),
            pl.BlockSpec((POOL_HALO, POOL_WIDTH),
                         lambda i: (jnp.minimum((mix_tile(i) + 1) * halo_blocks, last_halo), 0)),
            _const_spec((N_POOL_GROUPS, POOL_GROUP_DIM, POOL_GROUP_DIM)),
            _const_spec((N_POOL_GROUPS, 1, POOL_GROUP_DIM)),
            _const_spec((1, POOL_WIDTH)),
            _const_spec((D_MODEL, D_MODEL)),
            _const_spec((1, D_MODEL)),
            _const_spec((D_MODEL, d_ff)),
            _const_spec((D_MODEL, d_ff)),
            _const_spec((d_ff, D_MODEL)),
        ],
        out_specs=pl.BlockSpec((ROW_TILE, D_MODEL), lambda i: (jnp.maximum(i - 1, 0), 0)),
        out_shape=jax.ShapeDtypeStruct((n_tok, D_MODEL), F32),
        scratch_shapes=[pltpu.VMEM((ROW_TILE + 2 * POOL_HALO, POOL_WIDTH), F32),
                        pltpu.VMEM((ROW_TILE, POOL_WIDTH), BF16),
                        pltpu.VMEM((ROW_TILE, D_MODEL), F32), pltpu.VMEM((ROW_TILE, D_MODEL), BF16)],
        compiler_params=params,
        name="mix_ffn",
    )(x2, attn, u, u, u, pool_w.astype(F32), pool_b.astype(F32)[:, None, :],
      pool_scale.astype(F32)[None, :], w_out_b, norm2_g.astype(F32)[None, :],
      w_gate_b, w_up_b, w_down_b)
    return out


def kernel(x, norm1_g, w_in, q_norm_g, k_norm_g, pool_w, pool_b, pool_scale,
           w_out, norm2_g, w_gate, w_up, w_down):
    batch, seq, d_model = x.shape
    assert d_model == D_MODEL and w_in.shape[-1] == IN_WIDTH
    x2 = x.reshape(batch * seq, d_model)
    for l in range(norm1_g.shape[0]):
        x2 = _layer(x2, seq, norm1_g[l], w_in[l], q_norm_g[l], k_norm_g[l], pool_w[l],
                    pool_b[l], pool_scale[l], w_out[l], norm2_g[l], w_gate[l], w_up[l],
                    w_down[l])
    return x2.reshape(batch, seq, d_model)
```

```python
import functools
import math

import jax
import jax.numpy as jnp
import numpy as np
from jax import lax
from jax.experimental import pallas as pl
from jax.experimental.pallas import tpu as pltpu

D_MODEL = 1024
HEAD_DIM = 64
N_Q_HEADS = 8
N_KV_HEADS = 2
Q_PER_KV = N_Q_HEADS // N_KV_HEADS
ATTN_WIDTH = N_Q_HEADS * HEAD_DIM
KV_WIDTH = N_KV_HEADS * HEAD_DIM
GROUP_WIDTH = Q_PER_KV * HEAD_DIM
ROPE_PAIRS = HEAD_DIM // 4
ROPE_THETA = 10000.0
POOL_WINDOWS = (2, 4, 8, 16)
N_POOL_GROUPS = len(POOL_WINDOWS)
POOL_WIDTH = D_MODEL - ATTN_WIDTH
POOL_GROUP_DIM = POOL_WIDTH // N_POOL_GROUPS
IN_WIDTH = ATTN_WIDTH + 2 * KV_WIDTH + POOL_WIDTH
GRID_W = 64
EPS = 1e-6

LANES = 128
KEY_LANES = slice(GROUP_WIDTH, GROUP_WIDTH + LANES)
VALUE_LANES = slice(GROUP_WIDTH + LANES, GROUP_WIDTH + 2 * LANES)
QKV_WIDTH = GROUP_WIDTH + 2 * LANES
BF16_SUBLANES = 16
POOL_HALO = 8
VMEM_LIMIT_BYTES = 56 * 1024 * 1024

ROW_TILE = 512
INPROJ_ROWS = 1024
Q_TILE = 128
ATTN_SLOTS_PER_STEP = 4
N_SCORE_BUFS = 4
NEXT_TILES = 2
FF_CHUNK = 512
INPROJ_CHUNK = 256

F32 = jnp.float32
BF16 = jnp.bfloat16


def _lane_iota(shape):
    return lax.broadcasted_iota(jnp.int32, shape, len(shape) - 1)


def _rope_tables(seq):
    t = np.arange(seq)
    row = (t // GRID_W).astype(np.float64)
    col = (t % GRID_W).astype(np.float64)
    inv_freq = ROPE_THETA ** (-np.arange(ROPE_PAIRS, dtype=np.float64) / ROPE_PAIRS)
    ang_row = row[:, None] * inv_freq[None, :]
    ang_col = col[:, None] * inv_freq[None, :]
    ang = np.concatenate([ang_row, ang_row, ang_col, ang_col], axis=1)
    sign = np.concatenate([-np.ones(ROPE_PAIRS), np.ones(ROPE_PAIRS)] * 2)
    cos = np.tile(np.cos(ang), (1, 2)).astype(np.float32)
    sin = np.tile(np.sin(ang) * sign[None, :], (1, 2)).astype(np.float32)
    return cos, sin


def _segment_ones(width):
    idx = np.arange(width) // HEAD_DIM
    return (idx[:, None] == idx[None, :]).astype(np.float32)


def _head_sumsq(x, seg):
    sq = x * x
    hi = sq.astype(BF16)
    lo = (sq - hi.astype(F32)).astype(BF16)
    return (jnp.dot(hi, seg, preferred_element_type=F32)
            + jnp.dot(lo, seg, preferred_element_type=F32))


def _rope_partner(x):
    lane = _lane_iota(x.shape)
    fwd = pltpu.roll(x, LANES - ROPE_PAIRS, axis=1)
    bwd = pltpu.roll(x, ROPE_PAIRS, axis=1)
    return jnp.where((lane & ROPE_PAIRS) == 0, fwd, bwd)


def _norm_rope(x, sumsq, gain, cos, sin):
    y = x * lax.rsqrt(sumsq * (1.0 / HEAD_DIM) + EPS) * gain
    return y * cos + _rope_partner(y) * sin


def _inproj_kernel(x_ref, g1_ref, win_ref, qg_ref, kg_ref, cos_ref, sin_ref, seg_ref,
                   qkv_ref, u_ref, proja_ref, projb_ref, wbf_ref,
                   *, q_scale, n_tiles, seq):
    step = pl.program_id(0)
    rows = x_ref.shape[0]
    kv0 = ATTN_WIDTH
    u0 = ATTN_WIDTH + 2 * KV_WIDTH
    pos0 = pl.multiple_of((jnp.maximum(step - 1, 0) % (seq // rows)) * rows, rows)

    def rope_tables():
        return cos_ref[pl.ds(pos0, rows), :], sin_ref[pl.ds(pos0, rows), :]

    def head_pair_gain(g_ref):
        g = g_ref[...]
        return jnp.concatenate([g] * (LANES // HEAD_DIM), axis=1)

    def normed_input():
        x = x_ref[...]
        ms = jnp.mean(x * x, axis=-1, keepdims=True)
        return (x * lax.rsqrt(ms + EPS) * g1_ref[...]).astype(BF16)

    def finish_q(src_ref, g):
        qh = src_ref[:, g * GROUP_WIDTH:(g + 1) * GROUP_WIDTH]
        ss = _head_sumsq(qh, seg_ref[...])
        for j in range(GROUP_WIDTH // LANES):
            sl = slice(j * LANES, (j + 1) * LANES)
            qr = _norm_rope(qh[:, sl], ss[:, sl], head_pair_gain(qg_ref), *rope_tables()) * q_scale
            qkv_ref[0, g, :, sl] = qr.astype(BF16)

    def finish_kv(src_ref):
        k = src_ref[:, kv0:kv0 + KV_WIDTH]
        kss = _head_sumsq(k, seg_ref[:KV_WIDTH, :KV_WIDTH])
        kr = _norm_rope(k, kss, head_pair_gain(kg_ref), *rope_tables())
        first = _lane_iota(kr.shape) < HEAD_DIM
        kswap = pltpu.roll(kr, HEAD_DIM, axis=1)
        qkv_ref[0, 0, :, KEY_LANES] = jnp.where(first, kr, kswap).astype(BF16)
        qkv_ref[0, 1, :, KEY_LANES] = jnp.where(first, kswap, kr).astype(BF16)
        v = src_ref[:, kv0 + KV_WIDTH:kv0 + 2 * KV_WIDTH]
        vswap = pltpu.roll(v, HEAD_DIM, axis=1)
        qkv_ref[0, 0, :, VALUE_LANES] = jnp.where(first, v, 1.0).astype(BF16)
        qkv_ref[0, 1, :, VALUE_LANES] = jnp.where(first, vswap, 1.0).astype(BF16)

    def finish_pieces(src_ref):
        return [functools.partial(finish_q, src_ref, 0), functools.partial(finish_q, src_ref, 1),
                functools.partial(finish_kv, src_ref)]

    def project(dst_ref, between=()):
        assert u0 % INPROJ_CHUNK == 0
        h = normed_input()
        for c in range(IN_WIDTH // INPROJ_CHUNK):
            c0 = c * INPROJ_CHUNK
            res = jnp.dot(h, wbf_ref[:, c0:c0 + INPROJ_CHUNK], preferred_element_type=F32)
            if c0 < u0:
                dst_ref[:, c0:c0 + INPROJ_CHUNK] = res
            else:
                u_ref[:, c0 - u0:c0 - u0 + INPROJ_CHUNK] = res
            if c < len(between):
                between[c]()

    even = step % 2 == 0
    assert n_tiles % 2 == 0

    @pl.when(step == 0)
    def _():
        wbf_ref[...] = win_ref[...].astype(BF16)
        project(proja_ref)

    @pl.when(jnp.logical_and(even, jnp.logical_and(step > 0, step < n_tiles)))
    def _():
        project(proja_ref, finish_pieces(projb_ref))

    @pl.when(jnp.logical_not(even))
    def _():
        project(projb_ref, finish_pieces(proja_ref))

    @pl.when(step == n_tiles)
    def _():
        for piece in finish_pieces(projb_ref):
            piece()


def _attention_kernel(qkv_ref, qn_ref, kn_ref,
                      wf0_ref, wf1_ref, wf2_ref, wf3_ref,
                      o_ref, wb0_ref, wb1_ref, wb2_ref, wb3_ref, *scratch, seq):
    tiles_per_group = seq // Q_TILE
    n_tiles = N_KV_HEADS * tiles_per_group
    lane_o = _lane_iota((Q_TILE, LANES))

    def tile_pos(t):
        g = t // tiles_per_group
        r0 = pl.multiple_of((t % tiles_per_group) * Q_TILE, Q_TILE)
        return g, r0

    def scores(t, s_ref, mv_ref):
        if isinstance(t, int) and t >= n_tiles:
            assert t - n_tiles < NEXT_TILES
            qt = qn_ref[0, 0, (t - n_tiles) * Q_TILE:(t - n_tiles + 1) * Q_TILE, :]
            keys = kn_ref[0, 0]
        else:
            g, r0 = tile_pos(t)
            qt = qkv_ref[0, g, pl.ds(r0, Q_TILE), :GROUP_WIDTH]
            keys = qkv_ref[0, g, :, KEY_LANES]
        halves = [qt[:, j * LANES:(j + 1) * LANES] for j in range(GROUP_WIDTH // LANES)]
        zero = jnp.zeros_like(halves[0])
        per_half = LANES // HEAD_DIM
        qm = jnp.concatenate(
            [jnp.where((lane_o // HEAD_DIM) == hh % per_half, halves[hh // per_half], zero)
             for hh in range(Q_PER_KV)], axis=0)
        half = qm.shape[0] // 2
        for r in range(2):
            rs = slice(r * half, (r + 1) * half)
            s = lax.dot_general(qm[rs], keys, (((1,), (1,)), ((), ())),
                                preferred_element_type=F32)
            s_ref[rs, :] = s
            mv_ref[rs, :] = functools.reduce(
                jnp.maximum, [s[:, j * LANES:(j + 1) * LANES] for j in range(seq // LANES)])

    def row_max(mv_ref, m_ref):
        m_ref[...] = jnp.broadcast_to(jnp.max(mv_ref[...], axis=-1, keepdims=True), m_ref.shape)

    def output(t, s_ref, m_ref, oe_ref):
        g, _ = tile_pos(t)
        m = jnp.concatenate([m_ref[...]] * (seq // LANES), axis=1)
        p = jnp.exp2(s_ref[...] - m).astype(BF16)
        oe_ref[...] = jnp.dot(p, qkv_ref[0, g, :, VALUE_LANES], preferred_element_type=F32)

    def finalize(t, oe_ref):
        g, r0 = tile_pos(t)
        low = lane_o < HEAD_DIM
        for j in range(Q_PER_KV // 2):
            even = oe_ref[2 * j * Q_TILE:(2 * j + 1) * Q_TILE, :]
            odd = oe_ref[(2 * j + 1) * Q_TILE:(2 * j + 2) * Q_TILE, :]
            even_sw = pltpu.roll(even, HEAD_DIM, axis=1)
            odd_sw = pltpu.roll(odd, HEAD_DIM, axis=1)
            pair = jnp.where(low, even, odd_sw) / jnp.where(low, even_sw, odd)
            o_ref[0, g, pl.ds(r0, Q_TILE), j * LANES:(j + 1) * LANES] = pair.astype(BF16)

    n_buf = N_SCORE_BUFS
    assert n_tiles % n_buf == 0 and tiles_per_group >= NEXT_TILES and len(scratch) == 4 * n_buf
    s_refs, mv_refs = scratch[:n_buf], scratch[n_buf:2 * n_buf]
    m_refs, oe_refs = scratch[2 * n_buf:3 * n_buf], scratch[3 * n_buf:]

    def slot(t, k, first=False, side_work=None):
        row_max(mv_refs[(k + 1) % n_buf], m_refs[(k + 1) % n_buf])
        if not first:
            finalize(t - 1, oe_refs[(k - 1) % n_buf])
        scores(t + NEXT_TILES, s_refs[(k + NEXT_TILES) % n_buf], mv_refs[(k + NEXT_TILES) % n_buf])
        if side_work is not None:
            side_work()
        output(t, s_refs[k % n_buf], m_refs[k % n_buf], oe_refs[k % n_buf])

    def round_weights(wf_ref, wb_ref):
        wb_ref[...] = wf_ref[...].astype(BF16)

    side = [functools.partial(round_weights, wf, wb)
            for wf, wb in ((wf0_ref, wb0_ref), (wf1_ref, wb1_ref),
                           (wf2_ref, wb2_ref), (wf3_ref, wb3_ref))]

    @pl.when(pl.program_id(0) == 0)
    def _():
        for t in range(NEXT_TILES):
            scores(t, s_refs[t], mv_refs[t])
        row_max(mv_refs[0], m_refs[0])

    n_loop = (n_tiles - NEXT_TILES - 1) // ATTN_SLOTS_PER_STEP * ATTN_SLOTS_PER_STEP
    assert n_loop > 0 and ATTN_SLOTS_PER_STEP % n_buf == 0
    slot(0, 0, first=True, side_work=side.pop())

    def body(jj, carry):
        for i in range(ATTN_SLOTS_PER_STEP):
            slot(1 + jj * ATTN_SLOTS_PER_STEP + i, (1 + i) % n_buf)
        return carry

    lax.fori_loop(0, n_loop // ATTN_SLOTS_PER_STEP, body, 0)

    for t in range(1 + n_loop, n_tiles):
        slot(t, t % n_buf, side_work=side.pop() if side else None)
    assert not side
    finalize(n_tiles - 1, oe_refs[(n_tiles - 1) % n_buf])


def _mix_ffn_kernel(x_ref, a_ref, u_ref, uprev_ref, unext_ref, pw_ref, pb_ref, ps_ref,
                    wout_ref, g2_ref, wg_ref, wu_ref, wd_ref, o_ref,
                    ext_ref, mixp_ref, x1_ref, h_ref, *, seq, d_ff, n_tiles):
    rows = x_ref.shape[0]
    tiles_per_seq = seq // rows
    step = pl.program_id(0)

    st = jnp.minimum(step, n_tiles - 1) % tiles_per_seq

    def stage_windows():
        ext_ref[0:POOL_HALO, :] = jnp.where(st > 0, uprev_ref[...], 0.0)
        ext_ref[POOL_HALO:POOL_HALO + rows, :] = u_ref[...]
        ext_ref[POOL_HALO + rows:, :] = jnp.where(st < tiles_per_seq - 1, unext_ref[...], 0.0)

    def pool_group(g):
        n_ext = rows + 2 * POOL_HALO
        w = POOL_WINDOWS[g]
        assert w // 2 <= POOL_HALO
        cs = slice(g * POOL_GROUP_DIM, (g + 1) * POOL_GROUP_DIM)
        ext = ext_ref[:, cs]
        f = ext
        k = 1
        while k < w // 2:
            f = f + pltpu.roll(f, n_ext - k, axis=0)
            k *= 2
        both = f + pltpu.roll(f, w // 2, axis=0)
        win = both[POOL_HALO:POOL_HALO + rows]

        def edge_mean(r0):
            t = st * rows + r0 + lax.broadcasted_iota(jnp.int32, (POOL_HALO, LANES), 0)
            lo = jnp.maximum(t - w // 2, 0)
            hi = jnp.minimum(t - w // 2 + w, seq)
            return win[r0:r0 + POOL_HALO] / (hi - lo).astype(F32)

        mean = jnp.concatenate([edge_mean(0), win[POOL_HALO:rows - POOL_HALO] * (1.0 / w),
                                edge_mean(rows - POOL_HALO)], axis=0)
        pooled = (mean - ext[POOL_HALO:POOL_HALO + rows]).astype(BF16)
        y = jnp.dot(pooled, pw_ref[g].astype(BF16), preferred_element_type=F32) + pb_ref[g]
        mixp_ref[:, cs] = (y * ps_ref[:, cs]).astype(BF16)

    def project_out():
        mix = jnp.concatenate([a_ref[0, g] for g in range(N_KV_HEADS)] + [mixp_ref[...]],
                              axis=1)
        x1 = x_ref[...] + jnp.dot(mix, wout_ref[...], preferred_element_type=F32)
        ms = jnp.mean(x1 * x1, axis=-1, keepdims=True)
        x1_ref[...] = x1
        return (x1 * lax.rsqrt(ms + EPS) * g2_ref[...]).astype(BF16)

    bounds = [(c0, min(c0 + FF_CHUNK, d_ff)) for c0 in range(0, d_ff, FF_CHUNK)]

    @pl.when(step == 0)
    def _():
        stage_windows()
        for g in range(N_POOL_GROUPS):
            pool_group(g)
        h_ref[...] = project_out()

    @pl.when(step > 0)
    def _():
        stage_windows()
        h = h_ref[...]
        acc = x1_ref[...]
        h_next = []
        between = [[functools.partial(pool_group, g)] for g in range(N_POOL_GROUPS)]
        between += [[lambda: h_next.append(project_out())], []]
        assert len(between) == len(bounds)
        for c, (c0, c1) in enumerate(bounds):
            gate = jnp.dot(h, wg_ref[:, c0:c1], preferred_element_type=F32)
            up = jnp.dot(h, wu_ref[:, c0:c1], preferred_element_type=F32)
            if c == len(bounds) - 1:
                h_ref[...] = h_next[0]
            act = (gate * jax.nn.sigmoid(gate) * up).astype(BF16)
            acc = acc + jnp.dot(act, wd_ref[c0:c1, :], preferred_element_type=F32)
            for piece in between[c]:
                piece()
        o_ref[...] = acc


def _const_spec(shape):
    nd = len(shape)
    return pl.BlockSpec(shape, lambda *_: (0,) * nd)


def _layer(x2, seq, norm1_g, w_in, q_norm_g, k_norm_g, pool_w, pool_b, pool_scale,
           w_out, norm2_g, w_gate, w_up, w_down):
    n_tok = x2.shape[0]
    batch = n_tok // seq
    d_ff = w_gate.shape[1]
    assert seq % ROW_TILE == 0 and seq % Q_TILE == 0 and ROW_TILE % POOL_HALO == 0
    assert seq % INPROJ_ROWS == 0
    n_row_tiles = n_tok // ROW_TILE
    n_in_tiles = n_tok // INPROJ_ROWS

    cos, sin = _rope_tables(seq)
    seg = jnp.asarray(_segment_ones(256), BF16)
    q_scale = HEAD_DIM ** -0.5 * math.log2(math.e)

    params = pltpu.CompilerParams(dimension_semantics=("arbitrary",),
                                  vmem_limit_bytes=VMEM_LIMIT_BYTES)

    def grouped_rows(width, rows, tile):
        per_seq = seq // rows
        return pl.BlockSpec((1, N_KV_HEADS, rows, width),
                            lambda i: (tile(i) // per_seq, 0, tile(i) % per_seq, 0))

    def lead_tile(n):
        return lambda i: jnp.minimum(i, n - 1)

    mix_tile = lead_tile(n_row_tiles)
    in_tile = lead_tile(n_in_tiles)

    def lag_tile(i):
        return jnp.maximum(i - 1, 0)

    def grouped_seq(width):
        return pl.BlockSpec((1, N_KV_HEADS, seq, width), lambda b: (b, 0, 0, 0))

    qkv, u = pl.pallas_call(
        functools.partial(_inproj_kernel, q_scale=q_scale, n_tiles=n_in_tiles, seq=seq),
        grid=(n_in_tiles + 1,),
        in_specs=[
            pl.BlockSpec((INPROJ_ROWS, D_MODEL), lambda i: (in_tile(i), 0)),
            _const_spec((1, D_MODEL)),
            _const_spec((D_MODEL, IN_WIDTH)),
            _const_spec((1, HEAD_DIM)),
            _const_spec((1, HEAD_DIM)),
            _const_spec((seq, LANES)),
            _const_spec((seq, LANES)),
            _const_spec((256, 256)),
        ],
        out_specs=[
            grouped_rows(QKV_WIDTH, INPROJ_ROWS, lag_tile),
            pl.BlockSpec((INPROJ_ROWS, POOL_WIDTH), lambda i: (in_tile(i), 0)),
        ],
        out_shape=[
            jax.ShapeDtypeStruct((batch, N_KV_HEADS, seq, QKV_WIDTH), BF16),
            jax.ShapeDtypeStruct((n_tok, POOL_WIDTH), F32),
        ],
        scratch_shapes=[pltpu.VMEM((INPROJ_ROWS, IN_WIDTH - POOL_WIDTH), F32),
                        pltpu.VMEM((INPROJ_ROWS, IN_WIDTH - POOL_WIDTH), F32),
                        pltpu.VMEM((D_MODEL, IN_WIDTH), BF16)],
        compiler_params=params,
        name="inproj",
    )(x2, norm1_g.astype(F32)[None, :], w_in.astype(F32),
      q_norm_g.astype(F32)[None, :], k_norm_g.astype(F32)[None, :],
      jnp.asarray(cos), jnp.asarray(sin), seg)

    late_weights = [w_out, w_gate, w_up, w_down]
    for w in late_weights:
        assert w.shape[0] % (batch * BF16_SUBLANES) == 0

    def next_lead(rows, lanes):
        assert lanes.start % (lanes.stop - lanes.start) == 0
        width = lanes.stop - lanes.start
        return pl.BlockSpec((1, 1, rows, width),
                            lambda b: (jnp.minimum(b + 1, batch - 1), 0, 0, lanes.start // width))

    def row_slab(w):
        return pl.BlockSpec((w.shape[0] // batch, w.shape[1]), lambda b: (b, 0))

    attn, w_out_b, w_gate_b, w_up_b, w_down_b = pl.pallas_call(
        functools.partial(_attention_kernel, seq=seq),
        grid=(batch,),
        in_specs=([grouped_seq(QKV_WIDTH),
                   next_lead(NEXT_TILES * Q_TILE, slice(0, GROUP_WIDTH)), next_lead(seq, KEY_LANES)]
                  + [row_slab(w) for w in late_weights]),
        out_specs=[grouped_seq(GROUP_WIDTH)] + [row_slab(w) for w in late_weights],
        out_shape=([jax.ShapeDtypeStruct((batch, N_KV_HEADS, seq, GROUP_WIDTH), BF16)]
                   + [jax.ShapeDtypeStruct(w.shape, BF16) for w in late_weights]),
        scratch_shapes=([pltpu.VMEM((Q_PER_KV * Q_TILE, seq), F32)] * N_SCORE_BUFS
                        + [pltpu.VMEM((Q_PER_KV * Q_TILE, LANES), F32)] * N_SCORE_BUFS
                        + [pltpu.VMEM((Q_PER_KV * Q_TILE, LANES), F32)] * N_SCORE_BUFS
                        + [pltpu.VMEM((Q_PER_KV * Q_TILE, LANES), F32)] * N_SCORE_BUFS),
        compiler_params=params,
        name="attention",
    )(qkv, qkv, qkv, *[w.astype(F32) for w in late_weights])

    halo_blocks = ROW_TILE // POOL_HALO
    last_halo = n_tok // POOL_HALO - 1
    out = pl.pallas_call(
        functools.partial(_mix_ffn_kernel, seq=seq, d_ff=d_ff, n_tiles=n_row_tiles),
        grid=(n_row_tiles + 1,),
        in_specs=[
            pl.BlockSpec((ROW_TILE, D_MODEL), lambda i: (mix_tile(i), 0)),
            grouped_rows(GROUP_WIDTH, ROW_TILE, mix_tile),
            pl.BlockSpec((ROW_TILE, POOL_WIDTH), lambda i: (mix_tile(i), 0)),
            pl.BlockSpec((POOL_HALO, POOL_WIDTH),
                         lambda i: (jnp.maximum(mix_tile(i) * halo_blocks - 1, 0), 0)),
            pl.BlockSpec((POOL_HALO, POOL_WIDTH),
                         lambda i: (jnp.minimum((mix_tile(i) + 1) * halo_blocks, last_halo), 0)),
            _const_spec((N_POOL_GROUPS, POOL_GROUP_DIM, POOL_GROUP_DIM)),
            _const_spec((N_POOL_GROUPS, 1, POOL_GROUP_DIM)),
            _const_spec((1, POOL_WIDTH)),
            _const_spec((D_MODEL, D_MODEL)),
            _const_spec((1, D_MODEL)),
            _const_spec((D_MODEL, d_ff)),
            _const_spec((D_MODEL, d_ff)),
            _const_spec((d_ff, D_MODEL)),
        ],
        out_specs=pl.BlockSpec((ROW_TILE, D_MODEL), lambda i: (jnp.maximum(i - 1, 0), 0)),
        out_shape=jax.ShapeDtypeStruct((n_tok, D_MODEL), F32),
        scratch_shapes=[pltpu.VMEM((ROW_TILE + 2 * POOL_HALO, POOL_WIDTH), F32),
                        pltpu.VMEM((ROW_TILE, POOL_WIDTH), BF16),
                        pltpu.VMEM((ROW_TILE, D_MODEL), F32), pltpu.VMEM((ROW_TILE, D_MODEL), BF16)],
        compiler_params=params,
        name="mix_ffn",
    )(x2, attn, u, u, u, pool_w.astype(F32), pool_b.astype(F32)[:, None, :],
      pool_scale.astype(F32)[None, :], w_out_b, norm2_g.astype(F32)[None, :],
      w_gate_b, w_up_b, w_down_b)
    return out


def kernel(x, norm1_g, w_in, q_norm_g, k_norm_g, pool_w, pool_b, pool_scale,
           w_out, norm2_g, w_gate, w_up, w_down):
    batch, seq, d_model = x.shape
    assert d_model == D_MODEL and w_in.shape[-1] == IN_WIDTH
    x2 = x.reshape(batch * seq, d_model)
    for l in range(norm1_g.shape[0]):
        x2 = _layer(x2, seq, norm1_g[l], w_in[l], q_norm_g[l], k_norm_g[l], pool_w[l],
                    pool_b[l], pool_scale[l], w_out[l], norm2_g[l], w_gate[l], w_up[l],
                    w_down[l])
    return x2.reshape(batch, seq, d_model)
```

```python
import functools
import math

import jax
import jax.numpy as jnp
import numpy as np
from jax import lax
from jax.experimental import pallas as pl
from jax.experimental.pallas import tpu as pltpu

D_MODEL = 1024
HEAD_DIM = 64
N_Q_HEADS = 8
N_KV_HEADS = 2
Q_PER_KV = N_Q_HEADS // N_KV_HEADS
ATTN_WIDTH = N_Q_HEADS * HEAD_DIM
KV_WIDTH = N_KV_HEADS * HEAD_DIM
GROUP_WIDTH = Q_PER_KV * HEAD_DIM
ROPE_PAIRS = HEAD_DIM // 4
ROPE_THETA = 10000.0
POOL_WINDOWS = (2, 4, 8, 16)
N_POOL_GROUPS = len(POOL_WINDOWS)
POOL_WIDTH = D_MODEL - ATTN_WIDTH
POOL_GROUP_DIM = POOL_WIDTH // N_POOL_GROUPS
IN_WIDTH = ATTN_WIDTH + 2 * KV_WIDTH + POOL_WIDTH
GRID_W = 64
EPS = 1e-6

LANES = 128
KEY_LANES = slice(GROUP_WIDTH, GROUP_WIDTH + LANES)
VALUE_LANES = slice(GROUP_WIDTH + LANES, GROUP_WIDTH + 2 * LANES)
QKV_WIDTH = GROUP_WIDTH + 2 * LANES
BF16_SUBLANES = 16
POOL_HALO = 8
VMEM_LIMIT_BYTES = 56 * 1024 * 1024

ROW_TILE = 512
INPROJ_ROWS = 1024
Q_TILE = 128
ATTN_SLOTS_PER_STEP = 4
N_SCORE_BUFS = 4
NEXT_TILES = 2
FF_CHUNK = 512
INPROJ_CHUNK = 256

F32 = jnp.float32
BF16 = jnp.bfloat16


def _lane_iota(shape):
    return lax.broadcasted_iota(jnp.int32, shape, len(shape) - 1)


def _rope_tables(seq):
    t = np.arange(seq)
    row = (t // GRID_W).astype(np.float64)
    col = (t % GRID_W).astype(np.float64)
    inv_freq = ROPE_THETA ** (-np.arange(ROPE_PAIRS, dtype=np.float64) / ROPE_PAIRS)
    ang_row = row[:, None] * inv_freq[None, :]
    ang_col = col[:, None] * inv_freq[None, :]
    ang = np.concatenate([ang_row, ang_row, ang_col, ang_col], axis=1)
    sign = np.concatenate([-np.ones(ROPE_PAIRS), np.ones(ROPE_PAIRS)] * 2)
    cos = np.tile(np.cos(ang), (1, 2)).astype(np.float32)
    sin = np.tile(np.sin(ang) * sign[None, :], (1, 2)).astype(np.float32)
    return cos, sin


def _segment_ones(width):
    idx = np.arange(width) // HEAD_DIM
    return (idx[:, None] == idx[None, :]).astype(np.float32)


def _head_sumsq(x, seg):
    sq = x * x
    hi = sq.astype(BF16)
    lo = (sq - hi.astype(F32)).astype(BF16)
    return (jnp.dot(hi, seg, preferred_element_type=F32)
            + jnp.dot(lo, seg, preferred_element_type=F32))


def _rope_partner(x):
    lane = _lane_iota(x.shape)
    fwd = pltpu.roll(x, LANES - ROPE_PAIRS, axis=1)
    bwd = pltpu.roll(x, ROPE_PAIRS, axis=1)
    return jnp.where((lane & ROPE_PAIRS) == 0, fwd, bwd)


def _norm_rope(x, sumsq, gain, cos, sin):
    y = x * lax.rsqrt(sumsq * (1.0 / HEAD_DIM) + EPS) * gain
    return y * cos + _rope_partner(y) * sin


def _inproj_kernel(x_ref, g1_ref, win_ref, qg_ref, kg_ref, cos_ref, sin_ref, seg_ref,
                   qkv_ref, u_ref, proja_ref, projb_ref, wbf_ref,
                   *, q_scale, n_tiles, seq):
    step = pl.program_id(0)
    rows = x_ref.shape[0]
    kv0 = ATTN_WIDTH
    u0 = ATTN_WIDTH + 2 * KV_WIDTH
    pos0 = pl.multiple_of((jnp.maximum(step - 1, 0) % (seq // rows)) * rows, rows)

    def rope_tables():
        return cos_ref[pl.ds(pos0, rows), :], sin_ref[pl.ds(pos0, rows), :]

    def head_pair_gain(g_ref):
        g = g_ref[...]
        return jnp.concatenate([g] * (LANES // HEAD_DIM), axis=1)

    def normed_input():
        x = x_ref[...]
        ms = jnp.mean(x * x, axis=-1, keepdims=True)
        return (x * lax.rsqrt(ms + EPS) * g1_ref[...]).astype(BF16)

    def finish_q(src_ref, g):
        qh = src_ref[:, g * GROUP_WIDTH:(g + 1) * GROUP_WIDTH]
        ss = _head_sumsq(qh, seg_ref[...])
        for j in range(GROUP_WIDTH // LANES):
            sl = slice(j * LANES, (j + 1) * LANES)
            qr = _norm_rope(qh[:, sl], ss[:, sl], head_pair_gain(qg_ref), *rope_tables()) * q_scale
            qkv_ref[0, g, :, sl] = qr.astype(BF16)

    def finish_kv(src_ref):
        k = src_ref[:, kv0:kv0 + KV_WIDTH]
        kss = _head_sumsq(k, seg_ref[:KV_WIDTH, :KV_WIDTH])
        kr = _norm_rope(k, kss, head_pair_gain(kg_ref), *rope_tables())
        first = _lane_iota(kr.shape) < HEAD_DIM
        kswap = pltpu.roll(kr, HEAD_DIM, axis=1)
        qkv_ref[0, 0, :, KEY_LANES] = jnp.where(first, kr, kswap).astype(BF16)
        qkv_ref[0, 1, :, KEY_LANES] = jnp.where(first, kswap, kr).astype(BF16)
        v = src_ref[:, kv0 + KV_WIDTH:kv0 + 2 * KV_WIDTH]
        vswap = pltpu.roll(v, HEAD_DIM, axis=1)
        qkv_ref[0, 0, :, VALUE_LANES] = jnp.where(first, v, 1.0).astype(BF16)
        qkv_ref[0, 1, :, VALUE_LANES] = jnp.where(first, vswap, 1.0).astype(BF16)

    def finish_pieces(src_ref):
        return [functools.partial(finish_q, src_ref, 0), functools.partial(finish_q, src_ref, 1),
                functools.partial(finish_kv, src_ref)]

    def project(dst_ref, between=()):
        assert u0 % INPROJ_CHUNK == 0
        h = normed_input()
        for c in range(IN_WIDTH // INPROJ_CHUNK):
            c0 = c * INPROJ_CHUNK
            res = jnp.dot(h, wbf_ref[:, c0:c0 + INPROJ_CHUNK], preferred_element_type=F32)
            if c0 < u0:
                dst_ref[:, c0:c0 + INPROJ_CHUNK] = res
            else:
                u_ref[:, c0 - u0:c0 - u0 + INPROJ_CHUNK] = res
            if c < len(between):
                between[c]()

    even = step % 2 == 0
    assert n_tiles % 2 == 0

    @pl.when(step == 0)
    def _():
        wbf_ref[...] = win_ref[...].astype(BF16)
        project(proja_ref)

    @pl.when(jnp.logical_and(even, jnp.logical_and(step > 0, step < n_tiles)))
    def _():
        project(proja_ref, finish_pieces(projb_ref))

    @pl.when(jnp.logical_not(even))
    def _():
        project(projb_ref, finish_pieces(proja_ref))

    @pl.when(step == n_tiles)
    def _():
        for piece in finish_pieces(projb_ref):
            piece()


def _attention_kernel(qkv_ref, qn_ref, kn_ref,
                      wf0_ref, wf1_ref, wf2_ref, wf3_ref,
                      o_ref, wb0_ref, wb1_ref, wb2_ref, wb3_ref, *scratch, seq):
    tiles_per_group = seq // Q_TILE
    n_tiles = N_KV_HEADS * tiles_per_group
    lane_o = _lane_iota((Q_TILE, LANES))

    def tile_pos(t):
        g = t // tiles_per_group
        r0 = pl.multiple_of((t % tiles_per_group) * Q_TILE, Q_TILE)
        return g, r0

    def scores(t, s_ref, mv_ref):
        if isinstance(t, int) and t >= n_tiles:
            assert t - n_tiles < NEXT_TILES
            qt = qn_ref[0, 0, (t - n_tiles) * Q_TILE:(t - n_tiles + 1) * Q_TILE, :]
            keys = kn_ref[0, 0]
        else:
            g, r0 = tile_pos(t)
            qt = qkv_ref[0, g, pl.ds(r0, Q_TILE), :GROUP_WIDTH]
            keys = qkv_ref[0, g, :, KEY_LANES]
        halves = [qt[:, j * LANES:(j + 1) * LANES] for j in range(GROUP_WIDTH // LANES)]
        zero = jnp.zeros_like(halves[0])
        per_half = LANES // HEAD_DIM
        qm = jnp.concatenate(
            [jnp.where((lane_o // HEAD_DIM) == hh % per_half, halves[hh // per_half], zero)
             for hh in range(Q_PER_KV)], axis=0)
        s = lax.dot_general(qm, keys, (((1,), (1,)), ((), ())),
                            preferred_element_type=F32)
        s_ref[...] = s
        mv_ref[...] = functools.reduce(
            jnp.maximum, [s[:, j * LANES:(j + 1) * LANES] for j in range(seq // LANES)])

    def row_max(mv_ref, m_ref):
        m_ref[...] = jnp.broadcast_to(jnp.max(mv_ref[...], axis=-1, keepdims=True), m_ref.shape)

    def output(t, s_ref, m_ref, oe_ref):
        g, _ = tile_pos(t)
        m = jnp.concatenate([m_ref[...]] * (seq // LANES), axis=1)
        p = jnp.exp2(s_ref[...] - m).astype(BF16)
        oe_ref[...] = jnp.dot(p, qkv_ref[0, g, :, VALUE_LANES], preferred_element_type=F32)

    def finalize(t, oe_ref):
        g, r0 = tile_pos(t)
        low = lane_o < HEAD_DIM
        for j in range(Q_PER_KV // 2):
            even = oe_ref[2 * j * Q_TILE:(2 * j + 1) * Q_TILE, :]
            odd = oe_ref[(2 * j + 1) * Q_TILE:(2 * j + 2) * Q_TILE, :]
            even_sw = pltpu.roll(even, HEAD_DIM, axis=1)
            odd_sw = pltpu.roll(odd, HEAD_DIM, axis=1)
            pair = jnp.where(low, even, odd_sw) / jnp.where(low, even_sw, odd)
            o_ref[0, g, pl.ds(r0, Q_TILE), j * LANES:(j + 1) * LANES] = pair.astype(BF16)

    n_buf = N_SCORE_BUFS
    assert n_tiles % n_buf == 0 and tiles_per_group >= NEXT_TILES and len(scratch) == 4 * n_buf
    s_refs, mv_refs = scratch[:n_buf], scratch[n_buf:2 * n_buf]
    m_refs, oe_refs = scratch[2 * n_buf:3 * n_buf], scratch[3 * n_buf:]

    def slot(t, k, first=False, side_work=None):
        row_max(mv_refs[(k + 1) % n_buf], m_refs[(k + 1) % n_buf])
        if not first:
            finalize(t - 1, oe_refs[(k - 1) % n_buf])
        scores(t + NEXT_TILES, s_refs[(k + NEXT_TILES) % n_buf], mv_refs[(k + NEXT_TILES) % n_buf])
        if side_work is not None:
            side_work()
        output(t, s_refs[k % n_buf], m_refs[k % n_buf], oe_refs[k % n_buf])

    def round_weights(wf_ref, wb_ref):
        wb_ref[...] = wf_ref[...].astype(BF16)

    side = [functools.partial(round_weights, wf, wb)
            for wf, wb in ((wf0_ref, wb0_ref), (wf1_ref, wb1_ref),
                           (wf2_ref, wb2_ref), (wf3_ref, wb3_ref))]

    @pl.when(pl.program_id(0) == 0)
    def _():
        for t in range(NEXT_TILES):
            scores(t, s_refs[t], mv_refs[t])
        row_max(mv_refs[0], m_refs[0])

    n_loop = (n_tiles - NEXT_TILES - 1) // ATTN_SLOTS_PER_STEP * ATTN_SLOTS_PER_STEP
    assert n_loop > 0 and ATTN_SLOTS_PER_STEP % n_buf == 0
    slot(0, 0, first=True, side_work=side.pop())

    def body(jj, carry):
        for i in range(ATTN_SLOTS_PER_STEP):
            slot(1 + jj * ATTN_SLOTS_PER_STEP + i, (1 + i) % n_buf)
        return carry

    lax.fori_loop(0, n_loop // ATTN_SLOTS_PER_STEP, body, 0)

    for t in range(1 + n_loop, n_tiles):
        slot(t, t % n_buf, side_work=side.pop() if side else None)
    assert not side
    finalize(n_tiles - 1, oe_refs[(n_tiles - 1) % n_buf])


def _mix_ffn_kernel(x_ref, a_ref, u_ref, uprev_ref, unext_ref, pw_ref, pb_ref, ps_ref,
                    wout_ref, g2_ref, wg_hbm, wu_hbm, wd_hbm, o_ref,
                    ext_ref, mixp_ref, x1_ref, h_ref, wg_ref, wu_ref, wd_ref, w_sem,
                    *, seq, d_ff, n_tiles):
    rows = x_ref.shape[0]
    tiles_per_seq = seq // rows
    step = pl.program_id(0)

    st = jnp.minimum(step, n_tiles - 1) % tiles_per_seq

    def stage_windows():
        ext_ref[0:POOL_HALO, :] = jnp.where(st > 0, uprev_ref[...], 0.0)
        ext_ref[POOL_HALO:POOL_HALO + rows, :] = u_ref[...]
        ext_ref[POOL_HALO + rows:, :] = jnp.where(st < tiles_per_seq - 1, unext_ref[...], 0.0)

    def pool_group(g):
        n_ext = rows + 2 * POOL_HALO
        w = POOL_WINDOWS[g]
        assert w // 2 <= POOL_HALO
        cs = slice(g * POOL_GROUP_DIM, (g + 1) * POOL_GROUP_DIM)
        ext = ext_ref[:, cs]
        f = ext
        k = 1
        while k < w // 2:
            f = f + pltpu.roll(f, n_ext - k, axis=0)
            k *= 2
        both = f + pltpu.roll(f, w // 2, axis=0)
        win = both[POOL_HALO:POOL_HALO + rows]

        def edge_mean(r0):
            t = st * rows + r0 + lax.broadcasted_iota(jnp.int32, (POOL_HALO, LANES), 0)
            lo = jnp.maximum(t - w // 2, 0)
            hi = jnp.minimum(t - w // 2 + w, seq)
            return win[r0:r0 + POOL_HALO] / (hi - lo).astype(F32)

        mean = jnp.concatenate([edge_mean(0), win[POOL_HALO:rows - POOL_HALO] * (1.0 / w),
                                edge_mean(rows - POOL_HALO)], axis=0)
        pooled = (mean - ext[POOL_HALO:POOL_HALO + rows]).astype(BF16)
        y = jnp.dot(pooled, pw_ref[g].astype(BF16), preferred_element_type=F32) + pb_ref[g]
        mixp_ref[:, cs] = (y * ps_ref[:, cs]).astype(BF16)

    def project_out():
        mix = jnp.concatenate([a_ref[0, g] for g in range(N_KV_HEADS)] + [mixp_ref[...]],
                              axis=1)
        x1 = x_ref[...] + jnp.dot(mix, wout_ref[...], preferred_element_type=F32)
        ms = jnp.mean(x1 * x1, axis=-1, keepdims=True)
        x1_ref[...] = x1
        return (x1 * lax.rsqrt(ms + EPS) * g2_ref[...]).astype(BF16)

    bounds = [(c0, min(c0 + FF_CHUNK, d_ff)) for c0 in range(0, d_ff, FF_CHUNK)]

    def ffn_weight_copies():
        pairs = ((wg_hbm, wg_ref), (wu_hbm, wu_ref), (wd_hbm, wd_ref))
        return [pltpu.make_async_copy(src, dst, w_sem.at[i]) for i, (src, dst) in enumerate(pairs)]

    @pl.when(step == 0)
    def _():
        for copy in ffn_weight_copies():
            copy.start()
        stage_windows()
        for g in range(N_POOL_GROUPS):
            pool_group(g)
        h_ref[...] = project_out()

    @pl.when(step == 1)
    def _():
        for copy in ffn_weight_copies():
            copy.wait()

    @pl.when(step > 0)
    def _():
        stage_windows()
        h = h_ref[...]
        acc = x1_ref[...]
        h_next = []
        between = [[functools.partial(pool_group, g)] for g in range(N_POOL_GROUPS)]
        between += [[lambda: h_next.append(project_out())], []]
        assert len(between) == len(bounds)
        for c, (c0, c1) in enumerate(bounds):
            gate = jnp.dot(h, wg_ref[:, c0:c1], preferred_element_type=F32)
            up = jnp.dot(h, wu_ref[:, c0:c1], preferred_element_type=F32)
            if c == len(bounds) - 1:
                h_ref[...] = h_next[0]
            act = (gate * jax.nn.sigmoid(gate) * up).astype(BF16)
            acc = acc + jnp.dot(act, wd_ref[c0:c1, :], preferred_element_type=F32)
            for piece in between[c]:
                piece()
        o_ref[...] = acc


def _const_spec(shape):
    nd = len(shape)
    return pl.BlockSpec(shape, lambda *_: (0,) * nd)


def _layer(x2, seq, norm1_g, w_in, q_norm_g, k_norm_g, pool_w, pool_b, pool_scale,
           w_out, norm2_g, w_gate, w_up, w_down):
    n_tok = x2.shape[0]
    batch = n_tok // seq
    d_ff = w_gate.shape[1]
    assert seq % ROW_TILE == 0 and seq % Q_TILE == 0 and ROW_TILE % POOL_HALO == 0
    assert seq % INPROJ_ROWS == 0
    n_row_tiles = n_tok // ROW_TILE
    n_in_tiles = n_tok // INPROJ_ROWS

    cos, sin = _rope_tables(seq)
    seg = jnp.asarray(_segment_ones(256), BF16)
    q_scale = HEAD_DIM ** -0.5 * math.log2(math.e)

    params = pltpu.CompilerParams(dimension_semantics=("arbitrary",),
                                  vmem_limit_bytes=VMEM_LIMIT_BYTES)

    def grouped_rows(width, rows, tile):
        per_seq = seq // rows
        return pl.BlockSpec((1, N_KV_HEADS, rows, width),
                            lambda i: (tile(i) // per_seq, 0, tile(i) % per_seq, 0))

    def lead_tile(n):
        return lambda i: jnp.minimum(i, n - 1)

    mix_tile = lead_tile(n_row_tiles)
    in_tile = lead_tile(n_in_tiles)

    def lag_tile(i):
        return jnp.maximum(i - 1, 0)

    def grouped_seq(width):
        return pl.BlockSpec((1, N_KV_HEADS, seq, width), lambda b: (b, 0, 0, 0))

    qkv, u = pl.pallas_call(
        functools.partial(_inproj_kernel, q_scale=q_scale, n_tiles=n_in_tiles, seq=seq),
        grid=(n_in_tiles + 1,),
        in_specs=[
            pl.BlockSpec((INPROJ_ROWS, D_MODEL), lambda i: (in_tile(i), 0)),
            _const_spec((1, D_MODEL)),
            _const_spec((D_MODEL, IN_WIDTH)),
            _const_spec((1, HEAD_DIM)),
            _const_spec((1, HEAD_DIM)),
            _const_spec((seq, LANES)),
            _const_spec((seq, LANES)),
            _const_spec((256, 256)),
        ],
        out_specs=[
            grouped_rows(QKV_WIDTH, INPROJ_ROWS, lag_tile),
            pl.BlockSpec((INPROJ_ROWS, POOL_WIDTH), lambda i: (in_tile(i), 0)),
        ],
        out_shape=[
            jax.ShapeDtypeStruct((batch, N_KV_HEADS, seq, QKV_WIDTH), BF16),
            jax.ShapeDtypeStruct((n_tok, POOL_WIDTH), F32),
        ],
        scratch_shapes=[pltpu.VMEM((INPROJ_ROWS, IN_WIDTH - POOL_WIDTH), F32),
                        pltpu.VMEM((INPROJ_ROWS, IN_WIDTH - POOL_WIDTH), F32),
                        pltpu.VMEM((D_MODEL, IN_WIDTH), BF16)],
        compiler_params=params,
        name="inproj",
    )(x2, norm1_g.astype(F32)[None, :], w_in.astype(F32),
      q_norm_g.astype(F32)[None, :], k_norm_g.astype(F32)[None, :],
      jnp.asarray(cos), jnp.asarray(sin), seg)

    late_weights = [w_out, w_gate, w_up, w_down]
    for w in late_weights:
        assert w.shape[0] % (batch * BF16_SUBLANES) == 0

    def next_lead(rows, lanes):
        assert lanes.start % (lanes.stop - lanes.start) == 0
        width = lanes.stop - lanes.start
        return pl.BlockSpec((1, 1, rows, width),
                            lambda b: (jnp.minimum(b + 1, batch - 1), 0, 0, lanes.start // width))

    def row_slab(w):
        return pl.BlockSpec((w.shape[0] // batch, w.shape[1]), lambda b: (b, 0))

    attn, w_out_b, w_gate_b, w_up_b, w_down_b = pl.pallas_call(
        functools.partial(_attention_kernel, seq=seq),
        grid=(batch,),
        in_specs=([grouped_seq(QKV_WIDTH),
                   next_lead(NEXT_TILES * Q_TILE, slice(0, GROUP_WIDTH)), next_lead(seq, KEY_LANES)]
                  + [row_slab(w) for w in late_weights]),
        out_specs=[grouped_seq(GROUP_WIDTH)] + [row_slab(w) for w in late_weights],
        out_shape=([jax.ShapeDtypeStruct((batch, N_KV_HEADS, seq, GROUP_WIDTH), BF16)]
                   + [jax.ShapeDtypeStruct(w.shape, BF16) for w in late_weights]),
        scratch_shapes=([pltpu.VMEM((Q_PER_KV * Q_TILE, seq), F32)] * N_SCORE_BUFS
                        + [pltpu.VMEM((Q_PER_KV * Q_TILE, LANES), F32)] * N_SCORE_BUFS
                        + [pltpu.VMEM((Q_PER_KV * Q_TILE, LANES), F32)] * N_SCORE_BUFS
                        + [pltpu.VMEM((Q_PER_KV * Q_TILE, LANES), F32)] * N_SCORE_BUFS),
        compiler_params=params,
        name="attention",
    )(qkv, qkv, qkv, *[w.astype(F32) for w in late_weights])

    halo_blocks = ROW_TILE // POOL_HALO
    last_halo = n_tok // POOL_HALO - 1
    out = pl.pallas_call(
        functools.partial(_mix_ffn_kernel, seq=seq, d_ff=d_ff, n_tiles=n_row_tiles),
        grid=(n_row_tiles + 1,),
        in_specs=[
            pl.BlockSpec((ROW_TILE, D_MODEL), lambda i: (mix_tile(i), 0)),
            grouped_rows(GROUP_WIDTH, ROW_TILE, mix_tile),
            pl.BlockSpec((ROW_TILE, POOL_WIDTH), lambda i: (mix_tile(i), 0)),
            pl.BlockSpec((POOL_HALO, POOL_WIDTH),
                         lambda i: (jnp.maximum(mix_tile(i) * halo_blocks - 1, 0), 0)),
            pl.BlockSpec((POOL_HALO, POOL_WIDTH),
                         lambda i: (jnp.minimum((mix_tile(i) + 1) * halo_blocks, last_halo), 0)),
            _const_spec((N_POOL_GROUPS, POOL_GROUP_DIM, POOL_GROUP_DIM)),
            _const_spec((N_POOL_GROUPS, 1, POOL_GROUP_DIM)),
            _const_spec((1, POOL_WIDTH)),
            _const_spec((D_MODEL, D_MODEL)),
            _const_spec((1, D_MODEL)),
            pl.BlockSpec(memory_space=pl.ANY),
            pl.BlockSpec(memory_space=pl.ANY),
            pl.BlockSpec(memory_space=pl.ANY),
        ],
        out_specs=pl.BlockSpec((ROW_TILE, D_MODEL), lambda i: (jnp.maximum(i - 1, 0), 0)),
        out_shape=jax.ShapeDtypeStruct((n_tok, D_MODEL), F32),
        scratch_shapes=[pltpu.VMEM((ROW_TILE + 2 * POOL_HALO, POOL_WIDTH), F32),
                        pltpu.VMEM((ROW_TILE, POOL_WIDTH), BF16),
                        pltpu.VMEM((ROW_TILE, D_MODEL), F32), pltpu.VMEM((ROW_TILE, D_MODEL), BF16),
                        pltpu.VMEM((D_MODEL, d_ff), BF16), pltpu.VMEM((D_MODEL, d_ff), BF16),
                        pltpu.VMEM((d_ff, D_MODEL), BF16), pltpu.SemaphoreType.DMA((3,))],
        compiler_params=params,
        name="mix_ffn",
    )(x2, attn, u, u, u, pool_w.astype(F32), pool_b.astype(F32)[:, None, :],
      pool_scale.astype(F32)[None, :], w_out_b, norm2_g.astype(F32)[None, :],
      w_gate_b, w_up_b, w_down_b)
    return out


def kernel(x, norm1_g, w_in, q_norm_g, k_norm_g, pool_w, pool_b, pool_scale,
           w_out, norm2_g, w_gate, w_up, w_down):
    batch, seq, d_model = x.shape
    assert d_model == D_MODEL and w_in.shape[-1] == IN_WIDTH
    x2 = x.reshape(batch * seq, d_model)
    for l in range(norm1_g.shape[0]):
        x2 = _layer(x2, seq, norm1_g[l], w_in[l], q_norm_g[l], k_norm_g[l], pool_w[l],
                    pool_b[l], pool_scale[l], w_out[l], norm2_g[l], w_gate[l], w_up[l],
                    w_down[l])
    return x2.reshape(batch, seq, d_model)
```
